```python
import math
import jax, jax.numpy as jnp
from jax import lax
import numpy as np

D_MODEL = 1024
BATCH = 8
SEQ = 4096
DEPTH = 1

CHUNK = 64
PLE_DIM = 256
RMS_EPS = 1e-6

DA_HEADS = 8
DA_QK_DIM = 64
DA_V_DIM = 2 * DA_QK_DIM
Q_BLOCK = 128

SSD_EXPAND = 2
SSD_D_INNER = SSD_EXPAND * D_MODEL
SSD_HEAD_DIM = 64
SSD_HEADS = SSD_D_INNER // SSD_HEAD_DIM
SSD_GROUPS = 4
SSD_HEADS_PER_GROUP = SSD_HEADS // SSD_GROUPS
SSD_STATE = 128
SSD_CONV = 4
SSD_CONV_DIM = SSD_D_INNER + 2 * SSD_GROUPS * SSD_STATE

N_EXPERTS = 32
TOP_K = 4
D_EXPERT = D_MODEL
SWIGLU_LIMIT = 7.0
SWIGLU_ALPHA = 1.702

Q_W = DA_HEADS * 2 * DA_QK_DIM
K_W = DA_HEADS * 2 * DA_QK_DIM
V_W = DA_HEADS * DA_V_DIM
Z_W = SSD_D_INNER
XBC_W = SSD_CONV_DIM
DT_W = SSD_HEADS
GATE_W = D_MODEL
IN_W = Q_W + K_W + V_W + Z_W + XBC_W + DT_W + 2 * GATE_W
IN_SPLITS = (
    Q_W,
    Q_W + K_W,
    Q_W + K_W + V_W,
    Q_W + K_W + V_W + Z_W,
    Q_W + K_W + V_W + Z_W + XBC_W,
    Q_W + K_W + V_W + Z_W + XBC_W + DT_W,
    Q_W + K_W + V_W + Z_W + XBC_W + DT_W + GATE_W,
)

kernel_name = "hybrid_diffattn_ssd_moe_streaming_block"


def rms_norm(x, w):
    x32 = x.astype(jnp.float32)
    y = x32 * lax.rsqrt(jnp.mean(x32 * x32, axis=-1, keepdims=True) + RMS_EPS)
    return (y * w.astype(jnp.float32)).astype(x.dtype)


def alibi_slopes(n_heads):
    return jnp.asarray([2.0 ** (-8.0 * (h + 1) / n_heads) for h in range(n_heads)], dtype=jnp.float32)


def diff_attention(q, k, v, lam, head_norm_w, lambda_init):
    b, s = q.shape[0], q.shape[1]
    n_blocks = s // Q_BLOCK
    scale = DA_QK_DIM ** -0.5
    slopes = alibi_slopes(DA_HEADS)
    kpos = jnp.arange(s, dtype=jnp.int32)
    q_blocks = jnp.moveaxis(q.reshape(b, n_blocks, Q_BLOCK, DA_HEADS, 2, DA_QK_DIM), 1, 0)

    def one_block(args):
        q_blk, blk = args
        qpos = blk * Q_BLOCK + jnp.arange(Q_BLOCK, dtype=jnp.int32)
        scores = jnp.einsum('bqhmd,bkhmd->bhmqk', q_blk, k,
                            preferred_element_type=jnp.float32) * scale
        dist = jnp.abs(qpos[:, None] - kpos[None, :]).astype(jnp.float32)
        allowed = (kpos[None, :] // CHUNK) <= (qpos[:, None] // CHUNK)
        bias = jnp.where(allowed[None], -slopes[:, None, None] * dist[None], -jnp.inf)
        probs = jax.nn.softmax(scores + bias[None, :, None], axis=-1)
        attn = probs[:, :, 0] - lam * probs[:, :, 1]
        return jnp.einsum('bhqk,bkhd->bqhd', attn.astype(v.dtype), v)

    out = lax.map(one_block, (q_blocks, jnp.arange(n_blocks, dtype=jnp.int32)))
    out = jnp.moveaxis(out, 0, 1).reshape(b, s, DA_HEADS, DA_V_DIM)
    out = rms_norm(out, head_norm_w) * (1.0 - lambda_init)
    return out.reshape(b, s, DA_HEADS * DA_V_DIM)


def causal_depthwise_conv(u, w, bias):
    c = u.shape[-1]
    out = lax.conv_general_dilated(
        u, w.reshape(SSD_CONV, 1, c).astype(u.dtype),
        window_strides=(1,), padding=((SSD_CONV - 1, 0),),
        dimension_numbers=('NWC', 'WIO', 'NWC'), feature_group_count=c)
    return out + bias


def ssd_chunked_scan(xs, dt, a, bm, cm):
    b, s = xs.shape[0], xs.shape[1]
    nc = s // CHUNK
    def to_chunks(t):
        return t.reshape(b, nc, CHUNK, *t.shape[2:])
    x_c, dt_c, b_c, c_c = to_chunks(xs), to_chunks(dt), to_chunks(bm), to_chunks(cm)
    acum = jnp.cumsum(dt_c * a, axis=2)
    xdt = x_c * dt_c[..., None]

    idx = jnp.arange(CHUNK)
    causal = (idx[:, None] >= idx[None, :])[:, :, None, None]
    seg = acum[:, :, :, None] - acum[:, :, None, :]
    decay = jnp.exp(jnp.where(causal, seg, -jnp.inf))
    cb = jnp.einsum('bclgn,bcsgn->bclsg', c_c, b_c)
    y_diag = jnp.einsum('bclsgr,bcsgrp->bclgrp', cb[..., None] * decay, xdt)

    decay_to_end = jnp.exp(acum[:, :, -1:] - acum)
    decay_from_start = jnp.exp(acum)
    chunk_decay = jnp.exp(acum[:, :, -1])

    def step(state, inp):
        c_k, b_k, xdt_k, dte_k, dfs_k, cd_k = inp
        y_off = jnp.einsum('blgn,bgrpn,blgr->blgrp', c_k, state, dfs_k)
        new_state = state * cd_k[..., None, None] + jnp.einsum('blgn,blgr,blgrp->bgrpn', b_k, dte_k, xdt_k)
        return new_state, y_off

    state0 = jnp.zeros((b, SSD_GROUPS, SSD_HEADS_PER_GROUP, SSD_HEAD_DIM, SSD_STATE), jnp.float32)
    xs_scan = tuple(jnp.moveaxis(t, 1, 0) for t in (c_c, b_c, xdt, decay_to_end, decay_from_start, chunk_decay))
    _, y_off = lax.scan(step, state0, xs_scan)
    y = y_diag + jnp.moveaxis(y_off, 0, 1)
    return y.reshape(b, s, SSD_GROUPS, SSD_HEADS_PER_GROUP, SSD_HEAD_DIM)


def ssd_mixer(z, xbc, dt_raw, conv_w, conv_b, dt_bias, a_log, d_skip, norm_w):
    f32 = jnp.float32
    b, s = z.shape[0], z.shape[1]
    xbc = jax.nn.silu(causal_depthwise_conv(xbc, conv_w, conv_b))
    xs, bm, cm = jnp.split(xbc, [SSD_D_INNER, SSD_D_INNER + SSD_GROUPS * SSD_STATE], axis=-1)
    xs = xs.reshape(b, s, SSD_GROUPS, SSD_HEADS_PER_GROUP, SSD_HEAD_DIM).astype(f32)
    bm = bm.reshape(b, s, SSD_GROUPS, SSD_STATE).astype(f32)
    cm = cm.reshape(b, s, SSD_GROUPS, SSD_STATE).astype(f32)
    dt = jax.nn.softplus(dt_raw.astype(f32) + dt_bias.astype(f32)).reshape(b, s, SSD_GROUPS, SSD_HEADS_PER_GROUP)
    a = -jnp.exp(a_log.astype(f32)).reshape(SSD_GROUPS, SSD_HEADS_PER_GROUP)
    y = ssd_chunked_scan(xs, dt, a, bm, cm)
    y = y + d_skip.astype(f32).reshape(SSD_GROUPS, SSD_HEADS_PER_GROUP)[:, :, None] * xs
    y = y.reshape(b, s, SSD_D_INNER).astype(z.dtype) * jax.nn.silu(z)
    gsz = SSD_D_INNER // SSD_GROUPS
    y = rms_norm(y.reshape(b, s, SSD_GROUPS, gsz), norm_w.reshape(SSD_GROUPS, gsz))
    return y.reshape(b, s, SSD_D_INNER)


def moe_ffn(h, router_w, router_b, w_up, b_up, w_down, b_down):
    b, s, d = h.shape
    tok = h.reshape(-1, d)
    n_tok = tok.shape[0]
    logits = (tok @ router_w + router_b).astype(jnp.float32)
    top_vals, top_idx = lax.top_k(logits, TOP_K)
    gate_w = jax.nn.softmax(top_vals, axis=-1)
    flat_e = top_idx.reshape(-1)
    order = jnp.argsort(flat_e)
    token_of = order // TOP_K
    e_sorted = flat_e[order]
    group_sizes = jnp.bincount(flat_e, length=N_EXPERTS).astype(jnp.int32)
    x_sorted = tok[token_of]
    gu = lax.ragged_dot(x_sorted, w_up, group_sizes) + b_up[e_sorted]
    g, u = gu[:, :D_EXPERT], gu[:, D_EXPERT:]
    g = jnp.minimum(g, SWIGLU_LIMIT)
    u = jnp.clip(u, -SWIGLU_LIMIT, SWIGLU_LIMIT)
    act = (u + 1.0) * (g * jax.nn.sigmoid(SWIGLU_ALPHA * g))
    out = lax.ragged_dot(act, w_down, group_sizes) + b_down[e_sorted]
    out = out * gate_w.reshape(-1)[order][:, None].astype(out.dtype)
    y = jax.ops.segment_sum(out, token_of, num_segments=n_tok)
    return y.reshape(b, s, d)


def setup_inputs(seed: int = 0) -> dict:
    key = jax.random.key(seed)
    ks = jax.random.split(key, 32)
    f32 = jnp.float32
    L, D = DEPTH, D_MODEL

    def nrm(k, shape, scale):
        return jax.random.normal(k, shape, f32) * scale

    dt_init = jnp.exp(jax.random.uniform(ks[12], (L, SSD_HEADS), f32, math.log(1e-3), math.log(1e-1)))
    return {
        "x": nrm(ks[0], (BATCH, SEQ, D), 1.0),
        "p": nrm(ks[1], (DEPTH, BATCH, SEQ, PLE_DIM), 1.0),
        "mix_norm_w": 1.0 + nrm(ks[2], (L, D), 0.02),
        "w_in": nrm(ks[3], (L, D, IN_W), D ** -0.5),
        "lambda_q1": nrm(ks[4], (L, DA_QK_DIM), 0.1),
        "lambda_k1": nrm(ks[5], (L, DA_QK_DIM), 0.1),
        "lambda_q2": nrm(ks[6], (L, DA_QK_DIM), 0.1),
        "lambda_k2": nrm(ks[7], (L, DA_QK_DIM), 0.1),
        "da_head_norm_w": 1.0 + nrm(ks[8], (L, DA_V_DIM), 0.02),
        "w_attn_branch": nrm(ks[9], (L, V_W, D), V_W ** -0.5),
        "conv_w": nrm(ks[10], (L, SSD_CONV, SSD_CONV_DIM), SSD_CONV ** -0.5),
        "conv_b": nrm(ks[11], (L, SSD_CONV_DIM), 0.02),
        "dt_bias": dt_init + jnp.log(-jnp.expm1(-dt_init)),
        "a_log": jnp.log(jax.random.uniform(ks[13], (L, SSD_HEADS), f32, 1.0, 16.0)),
        "d_skip": 1.0 + nrm(ks[14], (L, SSD_HEADS), 0.02),
        "ssd_norm_w": 1.0 + nrm(ks[15], (L, SSD_D_INNER), 0.02),
        "w_ssd_branch": nrm(ks[16], (L, SSD_D_INNER, D), SSD_D_INNER ** -0.5),
        "w_out": nrm(ks[17], (L, D, D), D ** -0.5),
        "moe_norm_w": 1.0 + nrm(ks[18], (L, D), 0.02),
        "router_w": nrm(ks[19], (L, D, N_EXPERTS), D ** -0.5),
        "router_b": nrm(ks[20], (L, N_EXPERTS), 0.01),
        "w_up": nrm(ks[21], (L, N_EXPERTS, D, 2 * D_EXPERT), D ** -0.5),
        "b_up": nrm(ks[22], (L, N_EXPERTS, 2 * D_EXPERT), 0.01),
        "w_down": nrm(ks[23], (L, N_EXPERTS, D_EXPERT, D), D_EXPERT ** -0.5),
        "b_down": nrm(ks[24], (L, N_EXPERTS, D), 0.01),
        "ple_norm_w": 1.0 + nrm(ks[25], (L, D), 0.02),
        "w_ple_gate": nrm(ks[26], (L, D, D), D ** -0.5),
        "w_ple_proj": nrm(ks[27], (L, PLE_DIM, D), PLE_DIM ** -0.5),
        "final_norm_w": 1.0 + nrm(ks[28], (D,), 0.02),
    }


def reference(x, p, mix_norm_w, w_in, lambda_q1, lambda_k1, lambda_q2, lambda_k2,
              da_head_norm_w, w_attn_branch, conv_w, conv_b, dt_bias, a_log, d_skip,
              ssd_norm_w, w_ssd_branch, w_out, moe_norm_w, router_w, router_b,
              w_up, b_up, w_down, b_down, ple_norm_w, w_ple_gate, w_ple_proj, final_norm_w):
    b, s, _ = x.shape
    f32 = jnp.float32
    for i in range(DEPTH):
        h = rms_norm(x, mix_norm_w[i])
        proj = h @ w_in[i]
        q, k, v, z, xbc, dt_raw, gate_a, gate_s = jnp.split(proj, IN_SPLITS, axis=-1)

        q = q.reshape(b, s, DA_HEADS, 2, DA_QK_DIM)
        k = k.reshape(b, s, DA_HEADS, 2, DA_QK_DIM)
        v = v.reshape(b, s, DA_HEADS, DA_V_DIM)
        lambda_init = 0.8 - 0.6 * math.exp(-0.3 * i)
        lam = (jnp.exp(jnp.sum(lambda_q1[i].astype(f32) * lambda_k1[i].astype(f32)))
               - jnp.exp(jnp.sum(lambda_q2[i].astype(f32) * lambda_k2[i].astype(f32)))
               + lambda_init)
        y_attn = diff_attention(q, k, v, lam, da_head_norm_w[i], lambda_init)

        y_ssd = ssd_mixer(z, xbc, dt_raw, conv_w[i], conv_b[i], dt_bias[i], a_log[i],
                          d_skip[i], ssd_norm_w[i])

        merged = (jax.nn.sigmoid(gate_a) * (y_attn @ w_attn_branch[i])
                  + jax.nn.sigmoid(gate_s) * (y_ssd @ w_ssd_branch[i]))
        x = x + merged @ w_out[i]

        x = x + moe_ffn(rms_norm(x, moe_norm_w[i]), router_w[i], router_b[i],
                        w_up[i], b_up[i], w_down[i], b_down[i])

        ple_gate = jax.nn.sigmoid(rms_norm(x, ple_norm_w[i]) @ w_ple_gate[i])
        x = x + ple_gate * (p[i] @ w_ple_proj[i])
    return rms_norm(x, final_norm_w)
```

```python
import functools
import math

import jax
import jax.numpy as jnp
from jax import lax
from jax.experimental import pallas as pl
from jax.experimental.pallas import tpu as pltpu

F32 = jnp.float32
BF16 = jnp.bfloat16

D_MODEL = 1024
CHUNK = 64
PLE_DIM = 256
RMS_EPS = 1e-6

DA_HEADS = 8
DA_QK_DIM = 64
DA_V_DIM = 128

SSD_D_INNER = 2048
SSD_HEAD_DIM = 64
SSD_HEADS = 32
SSD_GROUPS = 4
SSD_STATE = 128
SSD_CONV = 4
SSD_CONV_DIM = 3072

N_EXPERTS = 32
TOP_K = 4
D_EXPERT = 1024
SWIGLU_LIMIT = 7.0
SWIGLU_ALPHA = 1.702

LANES = 128

COL_XBC = 0
COL_Q = 3072
COL_K = 4096
COL_V = 5120
COL_Z = 6144
COL_GA = 8192
COL_GS = 9216
PROJ_W = 10240

_R_Q, _R_K, _R_V, _R_Z, _R_XBC, _R_DT, _R_GA, _R_GS = 0, 1024, 2048, 3072, 5120, 8192, 8224, 9248

VMEM_LIMIT = 56 * 1024 * 1024


def _cparams(sem):
    return pltpu.CompilerParams(dimension_semantics=sem, vmem_limit_bytes=VMEM_LIMIT)


def _rms(x, w):
    return x * lax.rsqrt(jnp.mean(x * x, axis=-1, keepdims=True) + RMS_EPS) * w


def _sigmoid(x):
    return 1.0 / (1.0 + jnp.exp(-x))


def _inproj_kernel(x_ref, nw_ref, w_ref, wdt_ref, o_ref, dt_ref, h_ref):
    @pl.when(pl.program_id(1) == 0)
    def _():
        hb = _rms(x_ref[...], nw_ref[...]).astype(BF16)
        h_ref[...] = hb
        dt_ref[...] = jnp.dot(hb, wdt_ref[...], preferred_element_type=F32)

    o_ref[...] = jnp.dot(h_ref[...], w_ref[...], preferred_element_type=F32).astype(BF16)


def _in_proj(x2d, norm_w, w_main, w_dt, tm=1024, tn=1024):
    t = x2d.shape[0]
    tm = min(tm, t)
    return pl.pallas_call(
        _inproj_kernel,
        grid=(t // tm, PROJ_W // tn),
        in_specs=[
            pl.BlockSpec((tm, D_MODEL), lambda i, j: (i, 0)),
            pl.BlockSpec((1, D_MODEL), lambda i, j: (0, 0)),
            pl.BlockSpec((D_MODEL, tn), lambda i, j: (0, j)),
            pl.BlockSpec((D_MODEL, LANES), lambda i, j: (0, 0)),
        ],
        out_specs=[
            pl.BlockSpec((tm, tn), lambda i, j: (i, j)),
            pl.BlockSpec((tm, LANES), lambda i, j: (i, 0)),
        ],
        out_shape=[
            jax.ShapeDtypeStruct((t, PROJ_W), BF16),
            jax.ShapeDtypeStruct((t, LANES), F32),
        ],
        scratch_shapes=[pltpu.VMEM((tm, D_MODEL), BF16)],
        compiler_params=_cparams(("parallel", "arbitrary")),
        name="in_proj",
    )(x2d, norm_w, w_main, w_dt)


def _attn_kernel(slopes_ref, lam_ref, q_ref, k_ref, v_ref, nw_ref, o_ref,
                 m1_ref, l1_ref, a1_ref, m2_ref, l2_ref, a2_ref, *, tq, out_scale):
    h = pl.program_id(1)
    i = pl.program_id(2)
    slope = slopes_ref[h]
    lam = lam_ref[0]

    q = q_ref[...] * (DA_QK_DIM ** -0.5)
    lane = lax.broadcasted_iota(jnp.int32, q.shape, 1)
    zero = jnp.zeros_like(q)
    q1 = jnp.where(lane < DA_QK_DIM, q, zero)
    q2 = jnp.where(lane >= DA_QK_DIM, q, zero)

    row = lax.broadcasted_iota(jnp.int32, (tq, tq), 0)
    col = lax.broadcasted_iota(jnp.int32, (tq, tq), 1)
    d0 = (row - col).astype(F32)
    bias_off = -slope * d0
    allowed = (col // CHUNK) <= (row // CHUNK)
    bias_diag = jnp.where(allowed, -slope * jnp.abs(d0), -jnp.inf)

    def load_kv(j):
        start = pl.multiple_of(j * tq, tq)
        return k_ref[pl.ds(start, tq), :], v_ref[pl.ds(start, tq), :]

    def qk(qm, k):
        return lax.dot_general(qm, k, (((1,), (1,)), ((), ())), preferred_element_type=F32)

    def first(s, v, m_ref, l_ref, a_ref):
        m = jnp.max(s, axis=-1, keepdims=True)
        p = jnp.exp(s - m)
        m_ref[...] = m
        l_ref[...] = jnp.sum(p, axis=-1, keepdims=True)
        a_ref[...] = jnp.dot(p.astype(BF16), v, preferred_element_type=F32)

    def update(s, cj, v, m_ref, l_ref, a_ref):
        m_prev = m_ref[...]
        m_cur = jnp.maximum(m_prev, jnp.max(s, axis=-1, keepdims=True) + cj)
        alpha = jnp.exp(m_prev - m_cur)
        p = jnp.exp(s + (cj - m_cur))
        m_ref[...] = m_cur
        l_ref[...] = alpha * l_ref[...] + jnp.sum(p, axis=-1, keepdims=True)
        a_ref[...] = alpha * a_ref[...] + jnp.dot(p.astype(BF16), v, preferred_element_type=F32)

    k, v = load_kv(i)
    first(qk(q1, k) + bias_diag, v, m1_ref, l1_ref, a1_ref)
    first(qk(q2, k) + bias_diag, v, m2_ref, l2_ref, a2_ref)

    def body(j, carry):
        k, v = load_kv(j)
        cj = -slope * ((i - j) * tq).astype(F32)
        update(qk(q1, k) + bias_off, cj, v, m1_ref, l1_ref, a1_ref)
        update(qk(q2, k) + bias_off, cj, v, m2_ref, l2_ref, a2_ref)
        return carry

    lax.fori_loop(0, i, body, 0)

    o = a1_ref[...] / l1_ref[...] - lam * (a2_ref[...] / l2_ref[...])
    o_ref[...] = (_rms(o, nw_ref[...]) * out_scale).astype(BF16)


def _diff_attn(proj, slopes, lam, head_norm_w, batch, seq, lambda_init, tq=256):
    nq = seq // tq
    t = batch * seq
    qb, kb, vb = COL_Q // LANES, COL_K // LANES, COL_V // LANES
    kern = functools.partial(_attn_kernel, tq=tq, out_scale=1.0 - lambda_init)
    smem = pl.BlockSpec(memory_space=pltpu.SMEM)
    return pl.pallas_call(
        kern,
        grid=(batch, DA_HEADS, nq),
        in_specs=[
            smem, smem,
            pl.BlockSpec((tq, LANES), lambda b, h, i: (b * nq + i, qb + h)),
            pl.BlockSpec((seq, LANES), lambda b, h, i: (b, kb + h)),
            pl.BlockSpec((seq, LANES), lambda b, h, i: (b, vb + h)),
            pl.BlockSpec((1, LANES), lambda b, h, i: (0, 0)),
        ],
        out_specs=pl.BlockSpec((tq, LANES), lambda b, h, i: (b * nq + i, h)),
        out_shape=jax.ShapeDtypeStruct((t, DA_HEADS * DA_V_DIM), BF16),
        scratch_shapes=[
            pltpu.VMEM((tq, 1), F32), pltpu.VMEM((tq, 1), F32), pltpu.VMEM((tq, LANES), F32),
            pltpu.VMEM((tq, 1), F32), pltpu.VMEM((tq, 1), F32), pltpu.VMEM((tq, LANES), F32),
        ],
        compiler_params=_cparams(("parallel", "parallel", "arbitrary")),
        name="diff_attn",
    )(slopes, lam, proj, proj, proj, head_norm_w)


def _split2(x):
    hi = x.astype(BF16)
    lo = (x - hi.astype(F32)).astype(BF16)
    return hi, lo


def _split3(x):
    hi = x.astype(BF16)
    r = x - hi.astype(F32)
    mid = r.astype(BF16)
    lo = (r - mid.astype(F32)).astype(BF16)
    return hi, mid, lo


def _ssd_kernel(xbc_ref, z_ref, dt_ref, cw_ref, cb_ref, dtb_ref, alog_ref, dsk_ref, nw_ref, e_ref,
                o_ref, ubuf, xs_scr, bc_scr, acum_scr, acb, dfs_scr, xdt_scr, xw_scr, y_scr, state, *, tr):
    i = pl.program_id(1)
    nch = tr // CHUNK
    gw = SSD_D_INNER // SSD_GROUPS
    pad = 8

    @pl.when(i == 0)
    def _():
        ubuf[0:pad, :] = jnp.zeros((pad, SSD_CONV_DIM), F32)
        state[...] = jnp.zeros_like(state)

    ubuf[pad:pad + tr, :] = xbc_ref[...].astype(F32)
    cs = 512
    for c0 in range(0, SSD_CONV_DIM, cs):
        acc = jnp.broadcast_to(cb_ref[:, c0:c0 + cs], (tr, cs))
        for k in range(SSD_CONV):
            acc = acc + cw_ref[k:k + 1, c0:c0 + cs] * ubuf[pad - (SSD_CONV - 1) + k:pad - (SSD_CONV - 1) + k + tr, c0:c0 + cs]
        u = acc * _sigmoid(acc)
        if c0 < SSD_D_INNER:
            xs_scr[:, c0:c0 + cs] = u
        else:
            bc_scr[:, c0 - SSD_D_INNER:c0 - SSD_D_INNER + cs] = u.astype(BF16)
    ubuf[0:pad, :] = ubuf[tr:tr + pad, :]

    xdt_in = dt_ref[...] + dtb_ref[...]
    dtv = jnp.maximum(xdt_in, 0.0) + jnp.log1p(jnp.exp(-jnp.abs(xdt_in)))
    da = dtv * (-jnp.exp(alog_ref[...]))
    r_i = lax.broadcasted_iota(jnp.int32, (tr, tr), 0)
    c_i = lax.broadcasted_iota(jnp.int32, (tr, tr), 1)
    tril = jnp.where((c_i <= r_i) & ((c_i // CHUNK) == (r_i // CHUNK)), 1.0, 0.0).astype(BF16)
    acum = sum(jnp.dot(tril, piece, preferred_element_type=F32) for piece in _split3(da))
    acum_scr[...] = acum

    a_hi, a_lo = _split2(acum)
    d_hi, d_lo = _split2(dtv)
    for c0 in range(0, SSD_D_INNER, cs):
        e = e_ref[:, c0:c0 + cs]
        ab = jnp.dot(a_hi, e, preferred_element_type=F32) + jnp.dot(a_lo, e, preferred_element_type=F32)
        db = jnp.dot(d_hi, e, preferred_element_type=F32) + jnp.dot(d_lo, e, preferred_element_type=F32)
        acb[:, c0:c0 + cs] = ab
        dfs_scr[:, c0:c0 + cs] = jnp.exp(ab)
        xdt = xs_scr[:, c0:c0 + cs] * db
        xdt_scr[:, c0:c0 + cs] = xdt.astype(BF16)
        for c in range(nch):
            r0 = c * CHUNK
            last = ab[r0 + CHUNK - 1:r0 + CHUNK, :]
            xw_scr[r0:r0 + CHUNK, c0:c0 + cs] = (xdt[r0:r0 + CHUNK, :] * jnp.exp(last - ab[r0:r0 + CHUNK, :])).astype(BF16)

    lane2 = lax.broadcasted_iota(jnp.int32, (CHUNK, LANES), 1)
    row2 = lax.broadcasted_iota(jnp.int32, (CHUNK, LANES), 0)
    causal2 = row2 >= (lane2 % SSD_HEAD_DIM)
    lo_half = lane2 < SSD_HEAD_DIM
    zpad = jnp.zeros((LANES - CHUNK, LANES), F32)

    def chunk_body(c, carry):
        r0 = pl.multiple_of(c * CHUNK, CHUNK)
        rows = pl.ds(r0, CHUNK)
        a_c = acum_scr[rows, :]
        a_t = jnp.concatenate([a_c, zpad], axis=0).T
        a_t_r = pltpu.roll(a_t, SSD_HEAD_DIM, 1)
        last_row = acb[pl.ds(r0 + CHUNK - 1, 1), :]
        for g in range(SSD_GROUPS):
            bg = bc_scr[rows, g * SSD_STATE:(g + 1) * SSD_STATE]
            cg = bc_scr[rows, SSD_GROUPS * SSD_STATE + g * SSD_STATE:SSD_GROUPS * SSD_STATE + (g + 1) * SSD_STATE]
            cbt = lax.dot_general(cg, jnp.concatenate([bg, bg], axis=0), (((1,), (1,)), ((), ())),
                                  preferred_element_type=F32)
            g0 = g * gw
            st = state[g]
            yoff = jnp.dot(cg, st.astype(BF16), preferred_element_type=F32) * dfs_scr[rows, g0:g0 + gw]
            for jj in range(gw // LANES):
                pidx = g * (gw // LANES) + jj
                l0 = pidx * LANES
                arow = a_t[2 * pidx:2 * pidx + 1, :] + a_t_r[2 * pidx + 1:2 * pidx + 2, :]
                seg = acb[rows, l0:l0 + LANES] - arow
                decay = jnp.exp(jnp.where(causal2, seg, -jnp.inf))
                mp = (cbt * decay).astype(BF16)
                xd = xdt_scr[rows, l0:l0 + LANES]
                zb = jnp.zeros_like(xd)
                bd = jnp.concatenate([jnp.where(lo_half, xd, zb), jnp.where(lo_half, zb, xd)], axis=0)
                yd = jnp.dot(mp, bd, preferred_element_type=F32)
                y_scr[rows, l0:l0 + LANES] = yd + yoff[:, jj * LANES:(jj + 1) * LANES]
            cd = jnp.exp(last_row[:, g0:g0 + gw])
            upd = lax.dot_general(bg, xw_scr[rows, g0:g0 + gw], (((0,), (0,)), ((), ())),
                                  preferred_element_type=F32)
            state[g] = st * cd + upd
        return carry

    lax.fori_loop(0, nch, chunk_body, 0)

    for g in range(SSD_GROUPS):
        g0 = g * gw
        y = y_scr[:, g0:g0 + gw] + dsk_ref[:, g0:g0 + gw] * xs_scr[:, g0:g0 + gw]
        zz = z_ref[:, g0:g0 + gw].astype(F32)
        y = y * (zz * _sigmoid(zz))
        o_ref[:, g0:g0 + gw] = _rms(y, nw_ref[:, g0:g0 + gw]).astype(BF16)


def _ssd(proj, dt_raw, conv_w, conv_b, dt_bias, a_log, d_skip_b, norm_w, expand, batch, seq, tr=256):
    t = batch * seq
    nr = seq // tr
    kern = functools.partial(_ssd_kernel, tr=tr)
    const = lambda shape: pl.BlockSpec(shape, lambda b, i: (0, 0))
    return pl.pallas_call(
        kern,
        grid=(batch, nr),
        in_specs=[
            pl.BlockSpec((tr, SSD_CONV_DIM), lambda b, i: (b * nr + i, COL_XBC // SSD_CONV_DIM)),
            pl.BlockSpec((tr, SSD_D_INNER), lambda b, i: (b * nr + i, COL_Z // SSD_D_INNER)),
            pl.BlockSpec((tr, LANES), lambda b, i: (b * nr + i, 0)),
            const((SSD_CONV, SSD_CONV_DIM)),
            const((1, SSD_CONV_DIM)),
            const((1, LANES)),
            const((1, LANES)),
            const((1, SSD_D_INNER)),
            const((1, SSD_D_INNER)),
            const((LANES, SSD_D_INNER)),
        ],
        out_specs=pl.BlockSpec((tr, SSD_D_INNER), lambda b, i: (b * nr + i, 0)),
        out_shape=jax.ShapeDtypeStruct((t, SSD_D_INNER), BF16),
        scratch_shapes=[
            pltpu.VMEM((tr + 8, SSD_CONV_DIM), F32),
            pltpu.VMEM((tr, SSD_D_INNER), F32),
            pltpu.VMEM((tr, 2 * SSD_GROUPS * SSD_STATE), BF16),
            pltpu.VMEM((tr, LANES), F32),
            pltpu.VMEM((tr, SSD_D_INNER), F32),
            pltpu.VMEM((tr, SSD_D_INNER), F32),
            pltpu.VMEM((tr, SSD_D_INNER), BF16),
            pltpu.VMEM((tr, SSD_D_INNER), BF16),
            pltpu.VMEM((tr, SSD_D_INNER), F32),
            pltpu.VMEM((SSD_GROUPS, SSD_STATE, SSD_D_INNER // SSD_GROUPS), F32),
        ],
        compiler_params=_cparams(("parallel", "arbitrary")),
        name="ssd",
    )(proj, proj, dt_raw, conv_w, conv_b, dt_bias, a_log, d_skip_b, norm_w, expand)


def _merge_kernel(ya_ref, ys_ref, ga_ref, gs_ref, x_ref, wa_ref, wb_ref, wo_ref, o_ref):
    a = jnp.dot(ya_ref[...], wa_ref[...], preferred_element_type=F32)
    s = jnp.dot(ys_ref[...], wb_ref[...], preferred_element_type=F32)
    merged = _sigmoid(ga_ref[...].astype(F32)) * a + _sigmoid(gs_ref[...].astype(F32)) * s
    o_ref[...] = x_ref[...] + jnp.dot(merged.astype(BF16), wo_ref[...], preferred_element_type=F32)


def _merge(y_attn, y_ssd, proj, x2d, w_a, w_b, w_o, tm=512):
    t = x2d.shape[0]
    tm = min(tm, t)
    full = lambda r, c: pl.BlockSpec((r, c), lambda i: (0, 0))
    return pl.pallas_call(
        _merge_kernel,
        grid=(t // tm,),
        in_specs=[
            pl.BlockSpec((tm, D_MODEL), lambda i: (i, 0)),
            pl.BlockSpec((tm, SSD_D_INNER), lambda i: (i, 0)),
            pl.BlockSpec((tm, D_MODEL), lambda i: (i, COL_GA // D_MODEL)),
            pl.BlockSpec((tm, D_MODEL), lambda i: (i, COL_GS // D_MODEL)),
            pl.BlockSpec((tm, D_MODEL), lambda i: (i, 0)),
            full(D_MODEL, D_MODEL), full(SSD_D_INNER, D_MODEL), full(D_MODEL, D_MODEL),
        ],
        out_specs=pl.BlockSpec((tm, D_MODEL), lambda i: (i, 0)),
        out_shape=jax.ShapeDtypeStruct((t, D_MODEL), F32),
        compiler_params=_cparams(("parallel",)),
        name="merge",
    )(y_attn, y_ssd, proj, proj, x2d, w_a, w_b, w_o)


def _router_kernel(x_ref, nw_ref, rw_ref, rb_ref, h_ref, idx_ref, gate_ref, rank_ref, cnt_ref, carry_ref, *, tt):
    i = pl.program_id(0)

    @pl.when(i == 0)
    def _():
        carry_ref[...] = jnp.zeros_like(carry_ref)

    h = _rms(x_ref[...], nw_ref[...])
    h_ref[...] = h
    logits = jnp.dot(h, rw_ref[...], preferred_element_type=F32, precision=lax.Precision.HIGHEST) + rb_ref[...]
    lane = lax.broadcasted_iota(jnp.int32, (tt, LANES), 1)
    work = jnp.where(lane < N_EXPERTS, logits, -jnp.inf)

    vals, idxs, hots = [], [], []
    for _ in range(TOP_K):
        m = jnp.max(work, axis=-1, keepdims=True)
        idx = jnp.min(jnp.where(work == m, lane, LANES), axis=-1, keepdims=True)
        hot = lane == idx
        vals.append(m)
        idxs.append(idx)
        hots.append(hot)
        work = jnp.where(hot, -jnp.inf, work)

    exps = [jnp.exp(v - vals[0]) for v in vals]
    denom = exps[0] + exps[1] + exps[2] + exps[3]

    hot_sum = sum(jnp.where(hot, 1.0, 0.0) for hot in hots)
    r_i = lax.broadcasted_iota(jnp.int32, (tt, tt), 0)
    c_i = lax.broadcasted_iota(jnp.int32, (tt, tt), 1)
    strict = jnp.where(c_i < r_i, 1.0, 0.0).astype(BF16)
    prefix = jnp.dot(strict, hot_sum.astype(BF16), preferred_element_type=F32) + carry_ref[...]

    idx_out = jnp.zeros((tt, LANES), jnp.int32)
    rank_out = jnp.zeros((tt, LANES), jnp.int32)
    gate_out = jnp.zeros((tt, LANES), F32)
    for k in range(TOP_K):
        rank_k = jnp.sum(jnp.where(hots[k], prefix, 0.0), axis=-1, keepdims=True).astype(jnp.int32)
        sel = lane == k
        idx_out = jnp.where(sel, idxs[k], idx_out)
        rank_out = jnp.where(sel, rank_k, rank_out)
        gate_out = jnp.where(sel, exps[k] / denom, gate_out)
    idx_ref[...] = idx_out
    rank_ref[...] = rank_out
    gate_ref[...] = gate_out

    carry_ref[...] = carry_ref[...] + jnp.sum(hot_sum, axis=0, keepdims=True)
    cnt_ref[...] = carry_ref[...]


def _router(x1, norm_w, router_w_pad, router_b_pad, tt=512):
    t = x1.shape[0]
    tt = min(tt, t)
    kern = functools.partial(_router_kernel, tt=tt)
    row = pl.BlockSpec((tt, LANES), lambda i: (i, 0))
    return pl.pallas_call(
        kern,
        grid=(t // tt,),
        in_specs=[
            pl.BlockSpec((tt, D_MODEL), lambda i: (i, 0)),
            pl.BlockSpec((1, D_MODEL), lambda i: (0, 0)),
            pl.BlockSpec((D_MODEL, LANES), lambda i: (0, 0)),
            pl.BlockSpec((1, LANES), lambda i: (0, 0)),
        ],
        out_specs=[pl.BlockSpec((tt, D_MODEL), lambda i: (i, 0)), row, row, row,
                   pl.BlockSpec((1, LANES), lambda i: (0, 0))],
        out_shape=[
            jax.ShapeDtypeStruct((t, D_MODEL), F32),
            jax.ShapeDtypeStruct((t, LANES), jnp.int32),
            jax.ShapeDtypeStruct((t, LANES), F32),
            jax.ShapeDtypeStruct((t, LANES), jnp.int32),
            jax.ShapeDtypeStruct((1, LANES), F32),
        ],
        scratch_shapes=[pltpu.VMEM((1, LANES), F32)],
        compiler_params=_cparams(("arbitrary",)),
        name="router",
    )(x1, norm_w, router_w_pad, router_b_pad)


def _dispatch_kernel(pos_ref, h_ref, o_ref, sem, *, tt):
    def copy(t, k):
        return pltpu.make_async_copy(h_ref.at[pl.ds(t, 1), :], o_ref.at[pl.ds(pos_ref[t * TOP_K + k], 1), :], sem)

    def issue(t, c):
        for k in range(TOP_K):
            copy(t, k).start()
        return c

    lax.fori_loop(0, tt, issue, 0)

    def drain(t, c):
        for k in range(TOP_K):
            copy(t, k).wait()
        return c

    lax.fori_loop(0, tt, drain, 0)


def _dispatch(h2, pos_flat, tt=256):
    t = h2.shape[0]
    tt = min(tt, t)
    kern = functools.partial(_dispatch_kernel, tt=tt)
    return pl.pallas_call(
        kern,
        grid=(t // tt,),
        in_specs=[
            pl.BlockSpec((tt * TOP_K,), lambda i: (i,), memory_space=pltpu.SMEM),
            pl.BlockSpec((tt, D_MODEL), lambda i: (i, 0)),
        ],
        out_specs=pl.BlockSpec(memory_space=pl.ANY),
        out_shape=jax.ShapeDtypeStruct((t * TOP_K, D_MODEL), F32),
        scratch_shapes=[pltpu.SemaphoreType.DMA(())],
        compiler_params=_cparams(("arbitrary",)),
        name="dispatch",
    )(pos_flat, h2)


def _experts_kernel(tile_ref, exp_ref, first_ref, valid_ref, start_ref,
                    x_ref, wu_ref, bu_ref, wd_ref, bd_ref, o_ref, *, tm):
    w = pl.program_id(0)

    @pl.when(valid_ref[w] == 1)
    def _():
        e = exp_ref[w]
        gu = jnp.dot(x_ref[...].astype(BF16), wu_ref[0], preferred_element_type=F32) + bu_ref[0]
        g = jnp.minimum(gu[:, :D_EXPERT], SWIGLU_LIMIT)
        u = jnp.clip(gu[:, D_EXPERT:], -SWIGLU_LIMIT, SWIGLU_LIMIT)
        act = (u + 1.0) * (g * _sigmoid(SWIGLU_ALPHA * g))
        out = jnp.dot(act.astype(BF16), wd_ref[0], preferred_element_type=F32) + bd_ref[0]
        rows = tile_ref[w] * tm + lax.broadcasted_iota(jnp.int32, (tm, 1), 0)
        mine = (rows >= start_ref[e]) & (rows < start_ref[e + 1])

        @pl.when(first_ref[w] == 1)
        def _():
            o_ref[...] = jnp.where(mine, out, 0.0)

        @pl.when(first_ref[w] == 0)
        def _():
            o_ref[...] = jnp.where(mine, out, o_ref[...])


def _experts(x_sorted, item_tile, item_exp, item_first, item_valid, starts, w_up, b_up, w_down, b_down, tm):
    n = x_sorted.shape[0]
    n_items = item_tile.shape[0]
    kern = functools.partial(_experts_kernel, tm=tm)
    grid_spec = pltpu.PrefetchScalarGridSpec(
        num_scalar_prefetch=5,
        grid=(n_items,),
        in_specs=[
            pl.BlockSpec((tm, D_MODEL), lambda w, tl, ex, fi, va, st: (tl[w], 0)),
            pl.BlockSpec((1, D_MODEL, 2 * D_EXPERT), lambda w, tl, ex, fi, va, st: (ex[w], 0, 0)),
            pl.BlockSpec((1, 1, 2 * D_EXPERT), lambda w, tl, ex, fi, va, st: (ex[w], 0, 0)),
            pl.BlockSpec((1, D_EXPERT, D_MODEL), lambda w, tl, ex, fi, va, st: (ex[w], 0, 0)),
            pl.BlockSpec((1, 1, D_MODEL), lambda w, tl, ex, fi, va, st: (ex[w], 0, 0)),
        ],
        out_specs=pl.BlockSpec((tm, D_MODEL), lambda w, tl, ex, fi, va, st: (tl[w], 0)),
    )
    return pl.pallas_call(
        kern,
        grid_spec=grid_spec,
        out_shape=jax.ShapeDtypeStruct((n, D_MODEL), F32),
        compiler_params=_cparams(("arbitrary",)),
        name="experts",
    )(item_tile, item_exp, item_first, item_valid, starts, x_sorted, w_up, b_up, w_down, b_down)


def _combine_kernel(pos_ref, gate_ref, x_ref, p_ref, pnw_ref, wpg_ref, wpp_ref, fnw_ref, ys_ref, o_ref,
                    rows_ref, sem, *, tt):
    def copy(t, k):
        return pltpu.make_async_copy(ys_ref.at[pl.ds(pos_ref[t * TOP_K + k], 1), :],
                                     rows_ref.at[k, pl.ds(t, 1), :], sem)

    def issue(t, c):
        for k in range(TOP_K):
            copy(t, k).start()
        return c

    lax.fori_loop(0, tt, issue, 0)

    def drain(t, c):
        for k in range(TOP_K):
            copy(t, k).wait()
        return c

    lax.fori_loop(0, tt, drain, 0)

    gate = gate_ref[...]
    x = x_ref[...]
    for k in range(TOP_K):
        x = x + gate[:, k:k + 1] * rows_ref[k]

    hg = _rms(x, pnw_ref[...]).astype(BF16)
    pg = _sigmoid(jnp.dot(hg, wpg_ref[...], preferred_element_type=F32))
    x = x + pg * jnp.dot(p_ref[...].astype(BF16), wpp_ref[...], preferred_element_type=F32)
    o_ref[...] = _rms(x, fnw_ref[...])


def _combine(pos_flat, gate, x1, p2d, ple_nw, w_pg, w_pp, final_nw, y_sorted, tt=256):
    t = x1.shape[0]
    tt = min(tt, t)
    kern = functools.partial(_combine_kernel, tt=tt)
    vec = pl.BlockSpec((1, D_MODEL), lambda i: (0, 0))
    return pl.pallas_call(
        kern,
        grid=(t // tt,),
        in_specs=[
            pl.BlockSpec((tt * TOP_K,), lambda i: (i,), memory_space=pltpu.SMEM),
            pl.BlockSpec((tt, LANES), lambda i: (i, 0)),
            pl.BlockSpec((tt, D_MODEL), lambda i: (i, 0)),
            pl.BlockSpec((tt, PLE_DIM), lambda i: (i, 0)),
            vec,
            pl.BlockSpec((D_MODEL, D_MODEL), lambda i: (0, 0)),
            pl.BlockSpec((PLE_DIM, D_MODEL), lambda i: (0, 0)),
            vec,
            pl.BlockSpec(memory_space=pl.ANY),
        ],
        out_specs=pl.BlockSpec((tt, D_MODEL), lambda i: (i, 0)),
        out_shape=jax.ShapeDtypeStruct((t, D_MODEL), F32),
        scratch_shapes=[pltpu.VMEM((TOP_K, tt, D_MODEL), F32), pltpu.SemaphoreType.DMA(())],
        compiler_params=_cparams(("arbitrary",)),
        name="combine",
    )(pos_flat, gate, x1, p2d, ple_nw, w_pg, w_pp, final_nw, y_sorted)


def _routing_plan(counts, idx, rank, tm, n_rows):
    counts = counts.astype(jnp.int32)
    ends = jnp.cumsum(counts)
    starts = ends - counts
    pos = (starts[idx] + rank).reshape(-1)
    n_tiles = n_rows // tm
    n_items = n_tiles + N_EXPERTS - 1
    first_tile = starts // tm
    last_tile = jnp.maximum(ends - 1, 0) // tm
    items_e = jnp.where(counts > 0, last_tile - first_tile + 1, 0)
    item_end = jnp.cumsum(items_e)
    item_start = item_end - items_e
    total = item_end[-1]
    w = jnp.arange(n_items, dtype=jnp.int32)
    e_w = jnp.minimum(jnp.searchsorted(item_end, w, side="right").astype(jnp.int32), N_EXPERTS - 1)
    tile_w = first_tile[e_w] + (w - item_start[e_w])
    valid = w < total
    last = jnp.maximum(total - 1, 0)
    e_w = jnp.where(valid, e_w, e_w[last])
    tile_w = jnp.where(valid, tile_w, tile_w[last])
    prev_tile = jnp.concatenate([jnp.full((1,), -1, jnp.int32), tile_w[:-1]])
    first = (tile_w != prev_tile) & valid
    starts_ext = jnp.concatenate([starts, ends[-1:]]).astype(jnp.int32)
    return pos.astype(jnp.int32), tile_w.astype(jnp.int32), e_w, first.astype(jnp.int32), valid.astype(jnp.int32), starts_ext


def kernel(x, p, mix_norm_w, w_in, lambda_q1, lambda_k1, lambda_q2, lambda_k2, da_head_norm_w, w_attn_branch, conv_w, conv_b, dt_bias, a_log, d_skip, ssd_norm_w, w_ssd_branch, w_out, moe_norm_w, router_w, router_b, w_up, b_up, w_down, b_down, ple_norm_w, w_ple_gate, w_ple_proj, final_norm_w):
    batch, seq, _ = x.shape
    t = batch * seq
    depth = w_in.shape[0]
    assert depth == 1, "the final RMSNorm is fused into the layer's last kernel"
    x2d = x.reshape(t, D_MODEL)

    slopes = jnp.asarray([2.0 ** (-8.0 * (h + 1) / DA_HEADS) for h in range(DA_HEADS)], F32)
    head_ids = jnp.arange(SSD_D_INNER, dtype=jnp.int32) // SSD_HEAD_DIM
    expand = (jnp.arange(LANES, dtype=jnp.int32)[:, None] == head_ids[None, :]).astype(BF16)

    def pad_lanes(v, fill=0.0):
        return jnp.pad(v.reshape(1, -1), ((0, 0), (0, LANES - v.shape[-1])), constant_values=fill)

    for i in range(depth):
        wi = w_in[i]
        w_main = jnp.concatenate([
            wi[:, _R_XBC:_R_XBC + SSD_CONV_DIM], wi[:, _R_Q:_R_Q + 1024], wi[:, _R_K:_R_K + 1024],
            wi[:, _R_V:_R_V + 1024], wi[:, _R_Z:_R_Z + SSD_D_INNER], wi[:, _R_GA:_R_GA + 1024],
            wi[:, _R_GS:_R_GS + 1024]], axis=1).astype(BF16)
        w_dt = jnp.pad(wi[:, _R_DT:_R_DT + SSD_HEADS], ((0, 0), (0, LANES - SSD_HEADS))).astype(BF16)

        proj, dt_raw = _in_proj(x2d, mix_norm_w[i].reshape(1, -1), w_main, w_dt)

        lambda_init = 0.8 - 0.6 * math.exp(-0.3 * i)
        lam = (jnp.exp(jnp.sum(lambda_q1[i] * lambda_k1[i])) - jnp.exp(jnp.sum(lambda_q2[i] * lambda_k2[i]))
               + lambda_init).reshape(1).astype(F32)
        y_attn = _diff_attn(proj, slopes, lam, da_head_norm_w[i].reshape(1, -1), batch, seq, lambda_init)

        y_ssd = _ssd(proj, dt_raw, conv_w[i], conv_b[i].reshape(1, -1), pad_lanes(dt_bias[i]), pad_lanes(a_log[i]),
                     jnp.repeat(d_skip[i], SSD_HEAD_DIM).reshape(1, -1), ssd_norm_w[i].reshape(1, -1), expand,
                     batch, seq)

        x1 = _merge(y_attn, y_ssd, proj, x2d, w_attn_branch[i].astype(BF16), w_ssd_branch[i].astype(BF16),
                    w_out[i].astype(BF16))

        rw = jnp.pad(router_w[i], ((0, 0), (0, LANES - N_EXPERTS)))
        h2, idx, gate, rank, counts = _router(x1, moe_norm_w[i].reshape(1, -1), rw, pad_lanes(router_b[i]))

        tm = 512
        n_rows = t * TOP_K
        pos, item_tile, item_exp, item_first, item_valid, starts = _routing_plan(
            counts[0, :N_EXPERTS], idx[:, :TOP_K], rank[:, :TOP_K], tm, n_rows)

        x_sorted = _dispatch(h2, pos)
        y_sorted = _experts(x_sorted, item_tile, item_exp, item_first, item_valid, starts,
                            w_up[i].astype(BF16), b_up[i].reshape(N_EXPERTS, 1, -1),
                            w_down[i].astype(BF16), b_down[i].reshape(N_EXPERTS, 1, -1), tm)

        x2d = _combine(pos, gate, x1, p[i].reshape(t, PLE_DIM), ple_norm_w[i].reshape(1, -1),
                       w_ple_gate[i].astype(BF16), w_ple_proj[i].astype(BF16), final_norm_w.reshape(1, -1), y_sorted)
    return x2d.reshape(batch, seq, D_MODEL)
```

```python
import functools
import math

import jax
import jax.numpy as jnp
from jax import lax
from jax.experimental import pallas as pl
from jax.experimental.pallas import tpu as pltpu

F32 = jnp.float32
BF16 = jnp.bfloat16

D_MODEL = 1024
CHUNK = 64
PLE_DIM = 256
RMS_EPS = 1e-6

DA_HEADS = 8
DA_QK_DIM = 64
DA_V_DIM = 128

SSD_D_INNER = 2048
SSD_HEAD_DIM = 64
SSD_HEADS = 32
SSD_GROUPS = 4
SSD_STATE = 128
SSD_CONV = 4
SSD_CONV_DIM = 3072

N_EXPERTS = 32
TOP_K = 4
D_EXPERT = 1024
SWIGLU_LIMIT = 7.0
SWIGLU_ALPHA = 1.702

LANES = 128

COL_Z = 0
COL_Q = 2048
COL_K = 3072
COL_GA = 4096
COL_GS = 5120
COL_XBC = 6144
PROJ_W = 9216

ATTN_TILE = 256

_R_Q, _R_K, _R_V, _R_Z, _R_XBC, _R_DT, _R_GA, _R_GS = 0, 1024, 2048, 3072, 5120, 8192, 8224, 9248

VMEM_LIMIT = 56 * 1024 * 1024


def _cparams(sem):
    return pltpu.CompilerParams(dimension_semantics=sem, vmem_limit_bytes=VMEM_LIMIT)


def _rms(x, w):
    return x * lax.rsqrt(jnp.mean(x * x, axis=-1, keepdims=True) + RMS_EPS) * w


def _sigmoid(x):
    return 1.0 / (1.0 + jnp.exp(-x))


def _inproj_kernel(x_ref, nw_ref, w_ref, wdt_ref, wvt_ref, o_ref, dt_ref, vt_ref, h_ref, *, tm):
    @pl.when(pl.program_id(1) == 0)
    def _():
        hb = _rms(x_ref[...], nw_ref[...]).astype(BF16)
        h_ref[...] = hb
        dt_ref[...] = jnp.dot(hb, wdt_ref[...], preferred_element_type=F32)
        vt = lax.dot_general(wvt_ref[...], hb, (((1,), (1,)), ((), ())), preferred_element_type=F32)
        for c in range(tm // ATTN_TILE):
            vt_ref[c] = vt[:, c * ATTN_TILE:(c + 1) * ATTN_TILE].astype(BF16)

    o_ref[...] = jnp.dot(h_ref[...], w_ref[...], preferred_element_type=F32).astype(BF16)


def _in_proj(x2d, norm_w, w_main, w_dt, w_vt, tm=1024, tn=1024):
    t = x2d.shape[0]
    tm = min(tm, t)
    v_w = w_vt.shape[0]
    return pl.pallas_call(
        functools.partial(_inproj_kernel, tm=tm),
        grid=(t // tm, PROJ_W // tn),
        in_specs=[
            pl.BlockSpec((tm, D_MODEL), lambda i, j: (i, 0)),
            pl.BlockSpec((1, D_MODEL), lambda i, j: (0, 0)),
            pl.BlockSpec((D_MODEL, tn), lambda i, j: (0, j)),
            pl.BlockSpec((D_MODEL, LANES), lambda i, j: (0, 0)),
            pl.BlockSpec((v_w, D_MODEL), lambda i, j: (0, 0)),
        ],
        out_specs=[
            pl.BlockSpec((tm, tn), lambda i, j: (i, j)),
            pl.BlockSpec((tm, LANES), lambda i, j: (i, 0)),
            pl.BlockSpec((tm // ATTN_TILE, v_w, ATTN_TILE), lambda i, j: (i, 0, 0)),
        ],
        out_shape=[
            jax.ShapeDtypeStruct((t, PROJ_W), BF16),
            jax.ShapeDtypeStruct((t, LANES), F32),
            jax.ShapeDtypeStruct((t // ATTN_TILE, v_w, ATTN_TILE), BF16),
        ],
        scratch_shapes=[pltpu.VMEM((tm, D_MODEL), BF16)],
        compiler_params=_cparams(("parallel", "arbitrary")),
        name="in_proj",
    )(x2d, norm_w, w_main, w_dt, w_vt)


def _attn_kernel(slopes_ref, lam_ref, q_ref, k_ref, vt_ref, nw_ref, o_ref,
                 bias_ref, qc_ref, s_ref, p_ref, al_ref, m_ref, l_ref, a_ref, *, tq, out_scale):
    h = pl.program_id(1)
    i = pl.program_id(2)
    slope = slopes_ref[h]
    lam = lam_ref[0]

    @pl.when(i == 0)
    def _():
        kk = lax.broadcasted_iota(jnp.int32, (tq, tq), 0)
        qq = lax.broadcasted_iota(jnp.int32, (tq, tq), 1)
        d0 = (qq - kk).astype(F32)
        off = -slope * d0
        diag = jnp.where((kk // CHUNK) <= (qq // CHUNK), -slope * jnp.abs(d0), -jnp.inf)
        bias_ref[0] = jnp.concatenate([off, off], axis=1)
        bias_ref[1] = jnp.concatenate([diag, diag], axis=1)

    q = q_ref[...] * (DA_QK_DIM ** -0.5)
    lane = lax.broadcasted_iota(jnp.int32, q.shape, 1)
    zero = jnp.zeros_like(q)
    qc_ref[...] = jnp.concatenate([jnp.where(lane < DA_QK_DIM, q, zero), jnp.where(lane >= DA_QK_DIM, q, zero)], axis=0)

    m_ref[...] = jnp.full(m_ref.shape, -jnp.inf, F32)
    l_ref[...] = jnp.zeros(l_ref.shape, F32)
    a_ref[...] = jnp.zeros(a_ref.shape, F32)
    al_ref[...] = jnp.zeros(al_ref.shape, F32)
    p_ref[...] = jnp.zeros(p_ref.shape, BF16)

    def scores(j):
        k = k_ref[pl.ds(pl.multiple_of(j * tq, tq), tq), :]
        return lax.dot_general(k, qc_ref[...], (((1,), (1,)), ((), ())), preferred_element_type=F32)

    def softmax_step(s, cj):
        m_prev = m_ref[...]
        m_cur = jnp.maximum(m_prev, jnp.max(s, axis=0, keepdims=True) + cj)
        alpha = jnp.exp(m_prev - m_cur)
        p = jnp.exp(s + (cj - m_cur))
        m_ref[...] = m_cur
        l_ref[...] = alpha * l_ref[...] + jnp.sum(p, axis=0, keepdims=True)
        al_ref[...] = alpha
        p_ref[...] = p.astype(BF16)

    def pv_step(j):
        a_ref[...] = al_ref[...] * a_ref[...] + jnp.dot(vt_ref[j], p_ref[...], preferred_element_type=F32)

    def past_bias(j):
        return -slope * ((i - j) * tq).astype(F32)

    s_ref[...] = scores(0)

    def body(j, carry):
        s_next = scores(jnp.minimum(j + 1, i))
        pv_step(jnp.maximum(j - 1, 0))
        softmax_step(s_ref[...] + bias_ref[(j == i).astype(jnp.int32)], past_bias(j))
        s_ref[...] = s_next
        return carry

    lax.fori_loop(0, i + 1, body, 0)
    pv_step(i)

    on = a_ref[...] / l_ref[...]
    ot = on[:, :tq] - lam * on[:, tq:]
    ot = ot * lax.rsqrt(jnp.mean(ot * ot, axis=0, keepdims=True) + RMS_EPS)
    o_ref[...] = (ot.T * nw_ref[...] * out_scale).astype(BF16)


def _diff_attn(proj, v_t, slopes, lam, head_norm_w, batch, seq, lambda_init):
    tq = ATTN_TILE
    nq = seq // tq
    t = batch * seq
    qb, kb = COL_Q // LANES, COL_K // LANES
    kern = functools.partial(_attn_kernel, tq=tq, out_scale=1.0 - lambda_init)
    smem = pl.BlockSpec(memory_space=pltpu.SMEM)
    return pl.pallas_call(
        kern,
        grid=(batch, DA_HEADS, nq),
        in_specs=[
            smem, smem,
            pl.BlockSpec((tq, LANES), lambda b, h, i: (b * nq + i, qb + h)),
            pl.BlockSpec((seq, LANES), lambda b, h, i: (b, kb + h)),
            pl.BlockSpec((nq, DA_V_DIM, tq), lambda b, h, i: (b, h, 0)),
            pl.BlockSpec((1, LANES), lambda b, h, i: (0, 0)),
        ],
        out_specs=pl.BlockSpec((tq, LANES), lambda b, h, i: (b * nq + i, h)),
        out_shape=jax.ShapeDtypeStruct((t, DA_HEADS * DA_V_DIM), BF16),
        scratch_shapes=[
            pltpu.VMEM((2, tq, 2 * tq), F32),
            pltpu.VMEM((2 * tq, LANES), BF16),
            pltpu.VMEM((tq, 2 * tq), F32),
            pltpu.VMEM((tq, 2 * tq), BF16),
            pltpu.VMEM((1, 2 * tq), F32),
            pltpu.VMEM((1, 2 * tq), F32),
            pltpu.VMEM((1, 2 * tq), F32),
            pltpu.VMEM((DA_V_DIM, 2 * tq), F32),
        ],
        compiler_params=_cparams(("parallel", "parallel", "arbitrary")),
        name="diff_attn",
    )(slopes, lam, proj, proj, v_t, head_norm_w)


def _split2(x):
    hi = x.astype(BF16)
    lo = (x - hi.astype(F32)).astype(BF16)
    return hi, lo


def _split3(x):
    hi = x.astype(BF16)
    r = x - hi.astype(F32)
    mid = r.astype(BF16)
    lo = (r - mid.astype(F32)).astype(BF16)
    return hi, mid, lo


def _ssd_kernel(xbc_ref, z_ref, dt_ref, cw_ref, cb_ref, dtb_ref, alog_ref, dsk_ref, nw_ref, e_ref,
                o_ref, ubuf, xs_scr, bc_scr, acum_scr, acb, dfs_scr, xdt_scr, xw_scr, y_scr, state, *, tr):
    i = pl.program_id(1)
    nch = tr // CHUNK
    gw = SSD_D_INNER // SSD_GROUPS
    pad = 8

    @pl.when(i == 0)
    def _():
        ubuf[0:pad, :] = jnp.zeros((pad, SSD_CONV_DIM), F32)
        state[...] = jnp.zeros_like(state)

    ubuf[pad:pad + tr, :] = xbc_ref[...].astype(F32)
    cs = 512
    for c0 in range(0, SSD_CONV_DIM, cs):
        acc = jnp.broadcast_to(cb_ref[:, c0:c0 + cs], (tr, cs))
        for k in range(SSD_CONV):
            acc = acc + cw_ref[k:k + 1, c0:c0 + cs] * ubuf[pad - (SSD_CONV - 1) + k:pad - (SSD_CONV - 1) + k + tr, c0:c0 + cs]
        u = acc * _sigmoid(acc)
        if c0 < SSD_D_INNER:
            xs_scr[:, c0:c0 + cs] = u
        else:
            bc_scr[:, c0 - SSD_D_INNER:c0 - SSD_D_INNER + cs] = u.astype(BF16)
    ubuf[0:pad, :] = ubuf[tr:tr + pad, :]

    xdt_in = dt_ref[...] + dtb_ref[...]
    dtv = jnp.maximum(xdt_in, 0.0) + jnp.log1p(jnp.exp(-jnp.abs(xdt_in)))
    da = dtv * (-jnp.exp(alog_ref[...]))
    r_i = lax.broadcasted_iota(jnp.int32, (tr, tr), 0)
    c_i = lax.broadcasted_iota(jnp.int32, (tr, tr), 1)
    tril = jnp.where((c_i <= r_i) & ((c_i // CHUNK) == (r_i // CHUNK)), 1.0, 0.0).astype(BF16)
    acum = sum(jnp.dot(tril, piece, preferred_element_type=F32) for piece in _split3(da))
    acum_scr[...] = acum

    a_hi, a_lo = _split2(acum)
    d_hi, d_lo = _split2(dtv)
    for c0 in range(0, SSD_D_INNER, cs):
        e = e_ref[:, c0:c0 + cs]
        ab = jnp.dot(a_hi, e, preferred_element_type=F32) + jnp.dot(a_lo, e, preferred_element_type=F32)
        db = jnp.dot(d_hi, e, preferred_element_type=F32) + jnp.dot(d_lo, e, preferred_element_type=F32)
        acb[:, c0:c0 + cs] = ab
        dfs_scr[:, c0:c0 + cs] = jnp.exp(ab)
        xdt = xs_scr[:, c0:c0 + cs] * db
        xdt_scr[:, c0:c0 + cs] = xdt.astype(BF16)
        for c in range(nch):
            r0 = c * CHUNK
            last = ab[r0 + CHUNK - 1:r0 + CHUNK, :]
            xw_scr[r0:r0 + CHUNK, c0:c0 + cs] = (xdt[r0:r0 + CHUNK, :] * jnp.exp(last - ab[r0:r0 + CHUNK, :])).astype(BF16)

    lane2 = lax.broadcasted_iota(jnp.int32, (CHUNK, LANES), 1)
    row2 = lax.broadcasted_iota(jnp.int32, (CHUNK, LANES), 0)
    causal2 = row2 >= (lane2 % SSD_HEAD_DIM)
    lo_half = lane2 < SSD_HEAD_DIM
    zpad = jnp.zeros((LANES - CHUNK, LANES), F32)

    def chunk_body(c, carry):
        r0 = pl.multiple_of(c * CHUNK, CHUNK)
        rows = pl.ds(r0, CHUNK)
        a_c = acum_scr[rows, :]
        a_t = jnp.concatenate([a_c, zpad], axis=0).T
        a_t_r = pltpu.roll(a_t, SSD_HEAD_DIM, 1)
        last_row = acb[pl.ds(r0 + CHUNK - 1, 1), :]
        for g in range(SSD_GROUPS):
            bg = bc_scr[rows, g * SSD_STATE:(g + 1) * SSD_STATE]
            cg = bc_scr[rows, SSD_GROUPS * SSD_STATE + g * SSD_STATE:SSD_GROUPS * SSD_STATE + (g + 1) * SSD_STATE]
            cbt = lax.dot_general(cg, jnp.concatenate([bg, bg], axis=0), (((1,), (1,)), ((), ())),
                                  preferred_element_type=F32)
            g0 = g * gw
            st = state[g]
            yoff = jnp.dot(cg, st.astype(BF16), preferred_element_type=F32) * dfs_scr[rows, g0:g0 + gw]
            for jj in range(gw // LANES):
                pidx = g * (gw // LANES) + jj
                l0 = pidx * LANES
                arow = a_t[2 * pidx:2 * pidx + 1, :] + a_t_r[2 * pidx + 1:2 * pidx + 2, :]
                seg = acb[rows, l0:l0 + LANES] - arow
                decay = jnp.exp(jnp.where(causal2, seg, -jnp.inf))
                mp = (cbt * decay).astype(BF16)
                xd = xdt_scr[rows, l0:l0 + LANES]
                zb = jnp.zeros_like(xd)
                bd = jnp.concatenate([jnp.where(lo_half, xd, zb), jnp.where(lo_half, zb, xd)], axis=0)
                yd = jnp.dot(mp, bd, preferred_element_type=F32)
                y_scr[rows, l0:l0 + LANES] = yd + yoff[:, jj * LANES:(jj + 1) * LANES]
            cd = jnp.exp(last_row[:, g0:g0 + gw])
            upd = lax.dot_general(bg, xw_scr[rows, g0:g0 + gw], (((0,), (0,)), ((), ())),
                                  preferred_element_type=F32)
            state[g] = st * cd + upd
        return carry

    lax.fori_loop(0, nch, chunk_body, 0)

    for g in range(SSD_GROUPS):
        g0 = g * gw
        y = y_scr[:, g0:g0 + gw] + dsk_ref[:, g0:g0 + gw] * xs_scr[:, g0:g0 + gw]
        zz = z_ref[:, g0:g0 + gw].astype(F32)
        y = y * (zz * _sigmoid(zz))
        o_ref[:, g0:g0 + gw] = _rms(y, nw_ref[:, g0:g0 + gw]).astype(BF16)


def _ssd(proj, dt_raw, conv_w, conv_b, dt_bias, a_log, d_skip_b, norm_w, expand, batch, seq, tr=256):
    t = batch * seq
    nr = seq // tr
    kern = functools.partial(_ssd_kernel, tr=tr)
    const = lambda shape: pl.BlockSpec(shape, lambda b, i: (0, 0))
    return pl.pallas_call(
        kern,
        grid=(batch, nr),
        in_specs=[
            pl.BlockSpec((tr, SSD_CONV_DIM), lambda b, i: (b * nr + i, COL_XBC // SSD_CONV_DIM)),
            pl.BlockSpec((tr, SSD_D_INNER), lambda b, i: (b * nr + i, COL_Z // SSD_D_INNER)),
            pl.BlockSpec((tr, LANES), lambda b, i: (b * nr + i, 0)),
            const((SSD_CONV, SSD_CONV_DIM)),
            const((1, SSD_CONV_DIM)),
            const((1, LANES)),
            const((1, LANES)),
            const((1, SSD_D_INNER)),
            const((1, SSD_D_INNER)),
            const((LANES, SSD_D_INNER)),
        ],
        out_specs=pl.BlockSpec((tr, SSD_D_INNER), lambda b, i: (b * nr + i, 0)),
        out_shape=jax.ShapeDtypeStruct((t, SSD_D_INNER), BF16),
        scratch_shapes=[
            pltpu.VMEM((tr + 8, SSD_CONV_DIM), F32),
            pltpu.VMEM((tr, SSD_D_INNER), F32),
            pltpu.VMEM((tr, 2 * SSD_GROUPS * SSD_STATE), BF16),
            pltpu.VMEM((tr, LANES), F32),
            pltpu.VMEM((tr, SSD_D_INNER), F32),
            pltpu.VMEM((tr, SSD_D_INNER), F32),
            pltpu.VMEM((tr, SSD_D_INNER), BF16),
            pltpu.VMEM((tr, SSD_D_INNER), BF16),
            pltpu.VMEM((tr, SSD_D_INNER), F32),
            pltpu.VMEM((SSD_GROUPS, SSD_STATE, SSD_D_INNER // SSD_GROUPS), F32),
        ],
        compiler_params=_cparams(("parallel", "arbitrary")),
        name="ssd",
    )(proj, proj, dt_raw, conv_w, conv_b, dt_bias, a_log, d_skip_b, norm_w, expand)


def _merge_kernel(ya_ref, ys_ref, ga_ref, gs_ref, x_ref, wa_ref, wb_ref, wo_ref, o_ref):
    a = jnp.dot(ya_ref[...], wa_ref[...], preferred_element_type=F32)
    s = jnp.dot(ys_ref[...], wb_ref[...], preferred_element_type=F32)
    merged = _sigmoid(ga_ref[...].astype(F32)) * a + _sigmoid(gs_ref[...].astype(F32)) * s
    o_ref[...] = x_ref[...] + jnp.dot(merged.astype(BF16), wo_ref[...], preferred_element_type=F32)


def _merge(y_attn, y_ssd, proj, x2d, w_a, w_b, w_o, tm=512):
    t = x2d.shape[0]
    tm = min(tm, t)
    full = lambda r, c: pl.BlockSpec((r, c), lambda i: (0, 0))
    return pl.pallas_call(
        _merge_kernel,
        grid=(t // tm,),
        in_specs=[
            pl.BlockSpec((tm, D_MODEL), lambda i: (i, 0)),
            pl.BlockSpec((tm, SSD_D_INNER), lambda i: (i, 0)),
            pl.BlockSpec((tm, D_MODEL), lambda i: (i, COL_GA // D_MODEL)),
            pl.BlockSpec((tm, D_MODEL), lambda i: (i, COL_GS // D_MODEL)),
            pl.BlockSpec((tm, D_MODEL), lambda i: (i, 0)),
            full(D_MODEL, D_MODEL), full(SSD_D_INNER, D_MODEL), full(D_MODEL, D_MODEL),
        ],
        out_specs=pl.BlockSpec((tm, D_MODEL), lambda i: (i, 0)),
        out_shape=jax.ShapeDtypeStruct((t, D_MODEL), F32),
        compiler_params=_cparams(("parallel",)),
        name="merge",
    )(y_attn, y_ssd, proj, proj, x2d, w_a, w_b, w_o)


def _router_kernel(x_ref, nw_ref, rw_ref, rb_ref, h_ref, idx_ref, gate_ref, rank_ref, cnt_ref, carry_ref, *, tt):
    i = pl.program_id(0)

    @pl.when(i == 0)
    def _():
        carry_ref[...] = jnp.zeros_like(carry_ref)

    h = _rms(x_ref[...], nw_ref[...])
    h_ref[...] = h
    logits = jnp.dot(h, rw_ref[...], preferred_element_type=F32, precision=lax.Precision.HIGHEST) + rb_ref[...]
    lane = lax.broadcasted_iota(jnp.int32, (tt, LANES), 1)
    work = jnp.where(lane < N_EXPERTS, logits, -jnp.inf)

    vals, idxs, hots = [], [], []
    for _ in range(TOP_K):
        m = jnp.max(work, axis=-1, keepdims=True)
        idx = jnp.min(jnp.where(work == m, lane, LANES), axis=-1, keepdims=True)
        hot = lane == idx
        vals.append(m)
        idxs.append(idx)
        hots.append(hot)
        work = jnp.where(hot, -jnp.inf, work)

    exps = [jnp.exp(v - vals[0]) for v in vals]
    denom = exps[0] + exps[1] + exps[2] + exps[3]

    hot_sum = sum(jnp.where(hot, 1.0, 0.0) for hot in hots)
    r_i = lax.broadcasted_iota(jnp.int32, (tt, tt), 0)
    c_i = lax.broadcasted_iota(jnp.int32, (tt, tt), 1)
    strict = jnp.where(c_i < r_i, 1.0, 0.0).astype(BF16)
    prefix = jnp.dot(strict, hot_sum.astype(BF16), preferred_element_type=F32) + carry_ref[...]

    idx_out = jnp.zeros((tt, LANES), jnp.int32)
    rank_out = jnp.zeros((tt, LANES), jnp.int32)
    gate_out = jnp.zeros((tt, LANES), F32)
    for k in range(TOP_K):
        rank_k = jnp.sum(jnp.where(hots[k], prefix, 0.0), axis=-1, keepdims=True).astype(jnp.int32)
        sel = lane == k
        idx_out = jnp.where(sel, idxs[k], idx_out)
        rank_out = jnp.where(sel, rank_k, rank_out)
        gate_out = jnp.where(sel, exps[k] / denom, gate_out)
    idx_ref[...] = idx_out
    rank_ref[...] = rank_out
    gate_ref[...] = gate_out

    carry_ref[...] = carry_ref[...] + jnp.sum(hot_sum, axis=0, keepdims=True)
    cnt_ref[...] = carry_ref[...]


def _router(x1, norm_w, router_w_pad, router_b_pad, tt=512):
    t = x1.shape[0]
    tt = min(tt, t)
    kern = functools.partial(_router_kernel, tt=tt)
    row = pl.BlockSpec((tt, LANES), lambda i: (i, 0))
    return pl.pallas_call(
        kern,
        grid=(t // tt,),
        in_specs=[
            pl.BlockSpec((tt, D_MODEL), lambda i: (i, 0)),
            pl.BlockSpec((1, D_MODEL), lambda i: (0, 0)),
            pl.BlockSpec((D_MODEL, LANES), lambda i: (0, 0)),
            pl.BlockSpec((1, LANES), lambda i: (0, 0)),
        ],
        out_specs=[pl.BlockSpec((tt, D_MODEL), lambda i: (i, 0)), row, row, row,
                   pl.BlockSpec((1, LANES), lambda i: (0, 0))],
        out_shape=[
            jax.ShapeDtypeStruct((t, D_MODEL), F32),
            jax.ShapeDtypeStruct((t, LANES), jnp.int32),
            jax.ShapeDtypeStruct((t, LANES), F32),
            jax.ShapeDtypeStruct((t, LANES), jnp.int32),
            jax.ShapeDtypeStruct((1, LANES), F32),
        ],
        scratch_shapes=[pltpu.VMEM((1, LANES), F32)],
        compiler_params=_cparams(("arbitrary",)),
        name="router",
    )(x1, norm_w, router_w_pad, router_b_pad)


def _dispatch_kernel(pos_ref, h_ref, o_ref, sem, *, tt):
    def copy(t, k):
        return pltpu.make_async_copy(h_ref.at[pl.ds(t, 1), :], o_ref.at[pl.ds(pos_ref[t * TOP_K + k], 1), :], sem)

    def issue(t, c):
        for k in range(TOP_K):
            copy(t, k).start()
        return c

    lax.fori_loop(0, tt, issue, 0)

    def drain(t, c):
        for k in range(TOP_K):
            copy(t, k).wait()
        return c

    lax.fori_loop(0, tt, drain, 0)


def _dispatch(h2, pos_flat, tt=256):
    t = h2.shape[0]
    tt = min(tt, t)
    kern = functools.partial(_dispatch_kernel, tt=tt)
    return pl.pallas_call(
        kern,
        grid=(t // tt,),
        in_specs=[
            pl.BlockSpec((tt * TOP_K,), lambda i: (i,), memory_space=pltpu.SMEM),
            pl.BlockSpec((tt, D_MODEL), lambda i: (i, 0)),
        ],
        out_specs=pl.BlockSpec(memory_space=pl.ANY),
        out_shape=jax.ShapeDtypeStruct((t * TOP_K, D_MODEL), F32),
        scratch_shapes=[pltpu.SemaphoreType.DMA(())],
        compiler_params=_cparams(("arbitrary",)),
        name="dispatch",
    )(pos_flat, h2)


def _experts_kernel(tile_ref, exp_ref, first_ref, valid_ref, start_ref,
                    x_ref, wu_ref, bu_ref, wd_ref, bd_ref, o_ref, *, tm):
    w = pl.program_id(0)

    @pl.when(valid_ref[w] == 1)
    def _():
        e = exp_ref[w]
        gu = jnp.dot(x_ref[...].astype(BF16), wu_ref[0], preferred_element_type=F32) + bu_ref[0]
        g = jnp.minimum(gu[:, :D_EXPERT], SWIGLU_LIMIT)
        u = jnp.clip(gu[:, D_EXPERT:], -SWIGLU_LIMIT, SWIGLU_LIMIT)
        act = (u + 1.0) * (g * _sigmoid(SWIGLU_ALPHA * g))
        out = jnp.dot(act.astype(BF16), wd_ref[0], preferred_element_type=F32) + bd_ref[0]
        rows = tile_ref[w] * tm + lax.broadcasted_iota(jnp.int32, (tm, 1), 0)
        mine = (rows >= start_ref[e]) & (rows < start_ref[e + 1])

        @pl.when(first_ref[w] == 1)
        def _():
            o_ref[...] = jnp.where(mine, out, 0.0)

        @pl.when(first_ref[w] == 0)
        def _():
            o_ref[...] = jnp.where(mine, out, o_ref[...])


def _experts(x_sorted, item_tile, item_exp, item_first, item_valid, starts, w_up, b_up, w_down, b_down, tm):
    n = x_sorted.shape[0]
    n_items = item_tile.shape[0]
    kern = functools.partial(_experts_kernel, tm=tm)
    grid_spec = pltpu.PrefetchScalarGridSpec(
        num_scalar_prefetch=5,
        grid=(n_items,),
        in_specs=[
            pl.BlockSpec((tm, D_MODEL), lambda w, tl, ex, fi, va, st: (tl[w], 0)),
            pl.BlockSpec((1, D_MODEL, 2 * D_EXPERT), lambda w, tl, ex, fi, va, st: (ex[w], 0, 0)),
            pl.BlockSpec((1, 1, 2 * D_EXPERT), lambda w, tl, ex, fi, va, st: (ex[w], 0, 0)),
            pl.BlockSpec((1, D_EXPERT, D_MODEL), lambda w, tl, ex, fi, va, st: (ex[w], 0, 0)),
            pl.BlockSpec((1, 1, D_MODEL), lambda w, tl, ex, fi, va, st: (ex[w], 0, 0)),
        ],
        out_specs=pl.BlockSpec((tm, D_MODEL), lambda w, tl, ex, fi, va, st: (tl[w], 0)),
    )
    return pl.pallas_call(
        kern,
        grid_spec=grid_spec,
        out_shape=jax.ShapeDtypeStruct((n, D_MODEL), F32),
        compiler_params=_cparams(("arbitrary",)),
        name="experts",
    )(item_tile, item_exp, item_first, item_valid, starts, x_sorted, w_up, b_up, w_down, b_down)


def _combine_kernel(pos_ref, gate_ref, x_ref, p_ref, pnw_ref, wpg_ref, wpp_ref, fnw_ref, ys_ref, o_ref,
                    rows_ref, sem, *, tt):
    def copy(t, k):
        return pltpu.make_async_copy(ys_ref.at[pl.ds(pos_ref[t * TOP_K + k], 1), :],
                                     rows_ref.at[k, pl.ds(t, 1), :], sem)

    def issue(t, c):
        for k in range(TOP_K):
            copy(t, k).start()
        return c

    lax.fori_loop(0, tt, issue, 0)

    def drain(t, c):
        for k in range(TOP_K):
            copy(t, k).wait()
        return c

    lax.fori_loop(0, tt, drain, 0)

    gate = gate_ref[...]
    x = x_ref[...]
    for k in range(TOP_K):
        x = x + gate[:, k:k + 1] * rows_ref[k]

    hg = _rms(x, pnw_ref[...]).astype(BF16)
    pg = _sigmoid(jnp.dot(hg, wpg_ref[...], preferred_element_type=F32))
    x = x + pg * jnp.dot(p_ref[...].astype(BF16), wpp_ref[...], preferred_element_type=F32)
    o_ref[...] = _rms(x, fnw_ref[...])


def _combine(pos_flat, gate, x1, p2d, ple_nw, w_pg, w_pp, final_nw, y_sorted, tt=256):
    t = x1.shape[0]
    tt = min(tt, t)
    kern = functools.partial(_combine_kernel, tt=tt)
    vec = pl.BlockSpec((1, D_MODEL), lambda i: (0, 0))
    return pl.pallas_call(
        kern,
        grid=(t // tt,),
        in_specs=[
            pl.BlockSpec((tt * TOP_K,), lambda i: (i,), memory_space=pltpu.SMEM),
            pl.BlockSpec((tt, LANES), lambda i: (i, 0)),
            pl.BlockSpec((tt, D_MODEL), lambda i: (i, 0)),
            pl.BlockSpec((tt, PLE_DIM), lambda i: (i, 0)),
            vec,
            pl.BlockSpec((D_MODEL, D_MODEL), lambda i: (0, 0)),
            pl.BlockSpec((PLE_DIM, D_MODEL), lambda i: (0, 0)),
            vec,
            pl.BlockSpec(memory_space=pl.ANY),
        ],
        out_specs=pl.BlockSpec((tt, D_MODEL), lambda i: (i, 0)),
        out_shape=jax.ShapeDtypeStruct((t, D_MODEL), F32),
        scratch_shapes=[pltpu.VMEM((TOP_K, tt, D_MODEL), F32), pltpu.SemaphoreType.DMA(())],
        compiler_params=_cparams(("arbitrary",)),
        name="combine",
    )(pos_flat, gate, x1, p2d, ple_nw, w_pg, w_pp, final_nw, y_sorted)


def _routing_plan(counts, idx, rank, tm, n_rows):
    counts = counts.astype(jnp.int32)
    ends = jnp.cumsum(counts)
    starts = ends - counts
    pos = (starts[idx] + rank).reshape(-1)
    n_tiles = n_rows // tm
    n_items = n_tiles + N_EXPERTS - 1
    first_tile = starts // tm
    last_tile = jnp.maximum(ends - 1, 0) // tm
    items_e = jnp.where(counts > 0, last_tile - first_tile + 1, 0)
    item_end = jnp.cumsum(items_e)
    item_start = item_end - items_e
    total = item_end[-1]
    w = jnp.arange(n_items, dtype=jnp.int32)
    e_w = jnp.minimum(jnp.searchsorted(item_end, w, side="right").astype(jnp.int32), N_EXPERTS - 1)
    tile_w = first_tile[e_w] + (w - item_start[e_w])
    valid = w < total
    last = jnp.maximum(total - 1, 0)
    e_w = jnp.where(valid, e_w, e_w[last])
    tile_w = jnp.where(valid, tile_w, tile_w[last])
    prev_tile = jnp.concatenate([jnp.full((1,), -1, jnp.int32), tile_w[:-1]])
    first = (tile_w != prev_tile) & valid
    starts_ext = jnp.concatenate([starts, ends[-1:]]).astype(jnp.int32)
    return pos.astype(jnp.int32), tile_w.astype(jnp.int32), e_w, first.astype(jnp.int32), valid.astype(jnp.int32), starts_ext


def kernel(x, p, mix_norm_w, w_in, lambda_q1, lambda_k1, lambda_q2, lambda_k2, da_head_norm_w, w_attn_branch, conv_w, conv_b, dt_bias, a_log, d_skip, ssd_norm_w, w_ssd_branch, w_out, moe_norm_w, router_w, router_b, w_up, b_up, w_down, b_down, ple_norm_w, w_ple_gate, w_ple_proj, final_norm_w):
    batch, seq, _ = x.shape
    t = batch * seq
    depth = w_in.shape[0]
    assert depth == 1, "the final RMSNorm is fused into the layer's last kernel"
    x2d = x.reshape(t, D_MODEL)

    slopes = jnp.asarray([2.0 ** (-8.0 * (h + 1) / DA_HEADS) for h in range(DA_HEADS)], F32)
    head_ids = jnp.arange(SSD_D_INNER, dtype=jnp.int32) // SSD_HEAD_DIM
    expand = (jnp.arange(LANES, dtype=jnp.int32)[:, None] == head_ids[None, :]).astype(BF16)

    def pad_lanes(v, fill=0.0):
        return jnp.pad(v.reshape(1, -1), ((0, 0), (0, LANES - v.shape[-1])), constant_values=fill)

    for i in range(depth):
        wi = w_in[i]
        w_main = jnp.concatenate([
            wi[:, _R_Z:_R_Z + SSD_D_INNER], wi[:, _R_Q:_R_Q + 1024], wi[:, _R_K:_R_K + 1024],
            wi[:, _R_GA:_R_GA + 1024], wi[:, _R_GS:_R_GS + 1024], wi[:, _R_XBC:_R_XBC + SSD_CONV_DIM]],
            axis=1).astype(BF16)
        w_dt = jnp.pad(wi[:, _R_DT:_R_DT + SSD_HEADS], ((0, 0), (0, LANES - SSD_HEADS))).astype(BF16)
        w_vt = wi[:, _R_V:_R_V + 1024].T.astype(BF16)

        proj, dt_raw, v_t = _in_proj(x2d, mix_norm_w[i].reshape(1, -1), w_main, w_dt, w_vt)

        lambda_init = 0.8 - 0.6 * math.exp(-0.3 * i)
        lam = (jnp.exp(jnp.sum(lambda_q1[i] * lambda_k1[i])) - jnp.exp(jnp.sum(lambda_q2[i] * lambda_k2[i]))
               + lambda_init).reshape(1).astype(F32)
        y_attn = _diff_attn(proj, v_t, slopes, lam, da_head_norm_w[i].reshape(1, -1), batch, seq, lambda_init)

        y_ssd = _ssd(proj, dt_raw, conv_w[i], conv_b[i].reshape(1, -1), pad_lanes(dt_bias[i]), pad_lanes(a_log[i]),
                     jnp.repeat(d_skip[i], SSD_HEAD_DIM).reshape(1, -1), ssd_norm_w[i].reshape(1, -1), expand,
                     batch, seq)

        x1 = _merge(y_attn, y_ssd, proj, x2d, w_attn_branch[i].astype(BF16), w_ssd_branch[i].astype(BF16),
                    w_out[i].astype(BF16))

        rw = jnp.pad(router_w[i], ((0, 0), (0, LANES - N_EXPERTS)))
        h2, idx, gate, rank, counts = _router(x1, moe_norm_w[i].reshape(1, -1), rw, pad_lanes(router_b[i]))

        tm = 512
        n_rows = t * TOP_K
        pos, item_tile, item_exp, item_first, item_valid, starts = _routing_plan(
            counts[0, :N_EXPERTS], idx[:, :TOP_K], rank[:, :TOP_K], tm, n_rows)

        x_sorted = _dispatch(h2, pos)
        y_sorted = _experts(x_sorted, item_tile, item_exp, item_first, item_valid, starts,
                            w_up[i].astype(BF16), b_up[i].reshape(N_EXPERTS, 1, -1),
                            w_down[i].astype(BF16), b_down[i].reshape(N_EXPERTS, 1, -1), tm)

        x2d = _combine(pos, gate, x1, p[i].reshape(t, PLE_DIM), ple_norm_w[i].reshape(1, -1),
                       w_ple_gate[i].astype(BF16), w_ple_proj[i].astype(BF16), final_norm_w.reshape(1, -1), y_sorted)
    return x2d.reshape(batch, seq, D_MODEL)
```

```python
import functools
import math

import jax
import jax.numpy as jnp
from jax import lax
from jax.experimental import pallas as pl
from jax.experimental.pallas import tpu as pltpu

F32 = jnp.float32
BF16 = jnp.bfloat16

D_MODEL = 1024
CHUNK = 64
PLE_DIM = 256
RMS_EPS = 1e-6

DA_HEADS = 8
DA_QK_DIM = 64
DA_V_DIM = 128

SSD_D_INNER = 2048
SSD_HEAD_DIM = 64
SSD_HEADS = 32
SSD_GROUPS = 4
SSD_STATE = 128
SSD_CONV = 4
SSD_CONV_DIM = 3072

N_EXPERTS = 32
TOP_K = 4
D_EXPERT = 1024
SWIGLU_LIMIT = 7.0
SWIGLU_ALPHA = 1.702

LANES = 128

COL_Z = 0
COL_Q = 2048
COL_K = 3072
COL_GA = 4096
COL_GS = 5120
COL_XBC = 6144
PROJ_W = 9216

ATTN_TILE = 256
ATTN_Q_TILE = 512

_R_Q, _R_K, _R_V, _R_Z, _R_XBC, _R_DT, _R_GA, _R_GS = 0, 1024, 2048, 3072, 5120, 8192, 8224, 9248

VMEM_LIMIT = 56 * 1024 * 1024


def _cparams(sem):
    return pltpu.CompilerParams(dimension_semantics=sem, vmem_limit_bytes=VMEM_LIMIT)


def _rms(x, w):
    return x * lax.rsqrt(jnp.mean(x * x, axis=-1, keepdims=True) + RMS_EPS) * w


def _sigmoid(x):
    return 1.0 / (1.0 + jnp.exp(-x))


def _inproj_kernel(x_ref, nw_ref, w_ref, wdt_ref, wvt_ref, o_ref, dt_ref, vt_ref, h_ref, *, tm):
    @pl.when(pl.program_id(1) == 0)
    def _():
        hb = _rms(x_ref[...], nw_ref[...]).astype(BF16)
        h_ref[...] = hb
        dt_ref[...] = jnp.dot(hb, wdt_ref[...], preferred_element_type=F32)
        vt = lax.dot_general(wvt_ref[...], hb, (((1,), (1,)), ((), ())), preferred_element_type=F32)
        for c in range(tm // ATTN_TILE):
            vt_ref[c] = vt[:, c * ATTN_TILE:(c + 1) * ATTN_TILE].astype(BF16)

    o_ref[...] = jnp.dot(h_ref[...], w_ref[...], preferred_element_type=F32).astype(BF16)


def _in_proj(x2d, norm_w, w_main, w_dt, w_vt, tm=1024, tn=1024):
    t = x2d.shape[0]
    tm = min(tm, t)
    v_w = w_vt.shape[0]
    return pl.pallas_call(
        functools.partial(_inproj_kernel, tm=tm),
        grid=(t // tm, PROJ_W // tn),
        in_specs=[
            pl.BlockSpec((tm, D_MODEL), lambda i, j: (i, 0)),
            pl.BlockSpec((1, D_MODEL), lambda i, j: (0, 0)),
            pl.BlockSpec((D_MODEL, tn), lambda i, j: (0, j)),
            pl.BlockSpec((D_MODEL, LANES), lambda i, j: (0, 0)),
            pl.BlockSpec((v_w, D_MODEL), lambda i, j: (0, 0)),
        ],
        out_specs=[
            pl.BlockSpec((tm, tn), lambda i, j: (i, j)),
            pl.BlockSpec((tm, LANES), lambda i, j: (i, 0)),
            pl.BlockSpec((tm // ATTN_TILE, v_w, ATTN_TILE), lambda i, j: (i, 0, 0)),
        ],
        out_shape=[
            jax.ShapeDtypeStruct((t, PROJ_W), BF16),
            jax.ShapeDtypeStruct((t, LANES), F32),
            jax.ShapeDtypeStruct((t // ATTN_TILE, v_w, ATTN_TILE), BF16),
        ],
        scratch_shapes=[pltpu.VMEM((tm, D_MODEL), BF16)],
        compiler_params=_cparams(("parallel", "arbitrary")),
        name="in_proj",
    )(x2d, norm_w, w_main, w_dt, w_vt)


def _attn_kernel(slopes_ref, lam_ref, q_ref, k_ref, vt_ref, nw_ref, o_ref,
                 bias_ref, qc_ref, s_ref, p_ref, al_ref, m_ref, l_ref, a_ref, *, tq, tk, out_scale):
    h = pl.program_id(1)
    i = pl.program_id(2)
    slope = slopes_ref[h]
    lam = lam_ref[0]
    r = tq // tk
    n_tiles = r * (i + 1)

    @pl.when(i == 0)
    def _():
        kk = lax.broadcasted_iota(jnp.int32, (tk, tq), 0)
        qq = lax.broadcasted_iota(jnp.int32, (tk, tq), 1)
        bias_ref[0] = -slope * (qq - kk).astype(F32)
        for d in range(r):
            kd = kk + d * tk
            bias_ref[1 + d] = jnp.where((kd // CHUNK) <= (qq // CHUNK), -slope * jnp.abs(qq - kd).astype(F32), -jnp.inf)

    q = q_ref[...]
    lane = lax.broadcasted_iota(jnp.int32, q.shape, 1)
    zero = jnp.zeros_like(q)
    qc_ref[...] = jnp.concatenate([jnp.where(lane < DA_QK_DIM, q, zero), jnp.where(lane >= DA_QK_DIM, q, zero)], axis=0)

    m_ref[...] = jnp.full(m_ref.shape, -jnp.inf, F32)
    l_ref[...] = jnp.zeros(l_ref.shape, F32)
    a_ref[...] = jnp.zeros(a_ref.shape, F32)
    al_ref[...] = jnp.zeros(al_ref.shape, F32)
    p_ref[...] = jnp.zeros(p_ref.shape, BF16)

    def scores(j):
        k = k_ref[pl.ds(pl.multiple_of(j * tk, tk), tk), :]
        return lax.dot_general(k, qc_ref[...], (((1,), (1,)), ((), ())), preferred_element_type=F32)

    def softmax_step(sel, cj):
        for g in range(2 * tq // LANES):
            cols = slice(g * LANES, (g + 1) * LANES)
            bcols = slice((g * LANES) % tq, (g * LANES) % tq + LANES)
            s = s_ref[:, cols] + bias_ref[sel, :, bcols]
            m_prev = m_ref[:, cols]
            m_cur = jnp.maximum(m_prev, jnp.max(s, axis=0, keepdims=True) + cj)
            alpha = jnp.exp2(m_prev - m_cur)
            p = jnp.exp2(s + (cj - m_cur))
            m_ref[:, cols] = m_cur
            l_ref[:, cols] = alpha * l_ref[:, cols] + jnp.sum(p, axis=0, keepdims=True)
            al_ref[:, cols] = alpha
            p_ref[:, cols] = p.astype(BF16)

    def pv_step(j):
        vt = vt_ref[j]
        for g in range(2 * tq // 256):
            cols = slice(g * 256, (g + 1) * 256)
            a_ref[:, cols] = al_ref[:, cols] * a_ref[:, cols] + jnp.dot(vt, p_ref[:, cols], preferred_element_type=F32)

    s_ref[...] = scores(0)

    def body(j, carry):
        s_next = scores(jnp.minimum(j + 1, n_tiles - 1))
        pv_step(jnp.maximum(j - 1, 0))
        past = j < r * i
        cj = jnp.where(past, -slope * (i * tq - j * tk).astype(F32), 0.0)
        softmax_step(jnp.maximum(j - r * i + 1, 0), cj)
        s_ref[...] = s_next
        return carry

    lax.fori_loop(0, n_tiles, body, 0)
    pv_step(n_tiles - 1)

    on = a_ref[...] / l_ref[...]
    ot = on[:, :tq] - lam * on[:, tq:]
    ot = ot * lax.rsqrt(jnp.mean(ot * ot, axis=0, keepdims=True) + RMS_EPS)
    o_ref[...] = (ot.T * nw_ref[...] * out_scale).astype(BF16)


def _diff_attn(proj, v_t, slopes, lam, head_norm_w, batch, seq, lambda_init):
    tk = ATTN_TILE
    tq = min(ATTN_Q_TILE, seq)
    nq = seq // tq
    nk = seq // tk
    t = batch * seq
    qb, kb = COL_Q // LANES, COL_K // LANES
    assert tq % tk == 0
    kern = functools.partial(_attn_kernel, tq=tq, tk=tk, out_scale=1.0 - lambda_init)
    smem = pl.BlockSpec(memory_space=pltpu.SMEM)
    return pl.pallas_call(
        kern,
        grid=(batch, DA_HEADS, nq),
        in_specs=[
            smem, smem,
            pl.BlockSpec((tq, LANES), lambda b, h, i: (b * nq + i, qb + h)),
            pl.BlockSpec((seq, LANES), lambda b, h, i: (b, kb + h)),
            pl.BlockSpec((nk, DA_V_DIM, tk), lambda b, h, i: (b, h, 0)),
            pl.BlockSpec((1, LANES), lambda b, h, i: (0, 0)),
        ],
        out_specs=pl.BlockSpec((tq, LANES), lambda b, h, i: (b * nq + i, h)),
        out_shape=jax.ShapeDtypeStruct((t, DA_HEADS * DA_V_DIM), BF16),
        scratch_shapes=[
            pltpu.VMEM((1 + tq // tk, tk, tq), F32),
            pltpu.VMEM((2 * tq, LANES), BF16),
            pltpu.VMEM((tk, 2 * tq), F32),
            pltpu.VMEM((tk, 2 * tq), BF16),
            pltpu.VMEM((1, 2 * tq), F32),
            pltpu.VMEM((1, 2 * tq), F32),
            pltpu.VMEM((1, 2 * tq), F32),
            pltpu.VMEM((DA_V_DIM, 2 * tq), F32),
        ],
        compiler_params=_cparams(("parallel", "parallel", "arbitrary")),
        name="diff_attn",
    )(slopes, lam, proj, proj, v_t, head_norm_w)


def _split2(x):
    hi = x.astype(BF16)
    lo = (x - hi.astype(F32)).astype(BF16)
    return hi, lo


def _split3(x):
    hi = x.astype(BF16)
    r = x - hi.astype(F32)
    mid = r.astype(BF16)
    lo = (r - mid.astype(F32)).astype(BF16)
    return hi, mid, lo


def _ssd_kernel(xbc_ref, z_ref, dt_ref, cw_ref, cb_ref, dtb_ref, alog_ref, dsk_ref, nw_ref, e_ref,
                o_ref, ubuf, xs_scr, bc_scr, acum_scr, acb, dfs_scr, xdt_scr, xw_scr, y_scr, state, *, tr):
    i = pl.program_id(1)
    nch = tr // CHUNK
    gw = SSD_D_INNER // SSD_GROUPS
    pad = 8

    @pl.when(i == 0)
    def _():
        ubuf[0:pad, :] = jnp.zeros((pad, SSD_CONV_DIM), F32)
        state[...] = jnp.zeros_like(state)

    ubuf[pad:pad + tr, :] = xbc_ref[...].astype(F32)
    cs = 512
    for c0 in range(0, SSD_CONV_DIM, cs):
        acc = jnp.broadcast_to(cb_ref[:, c0:c0 + cs], (tr, cs))
        for k in range(SSD_CONV):
            acc = acc + cw_ref[k:k + 1, c0:c0 + cs] * ubuf[pad - (SSD_CONV - 1) + k:pad - (SSD_CONV - 1) + k + tr, c0:c0 + cs]
        u = acc * _sigmoid(acc)
        if c0 < SSD_D_INNER:
            xs_scr[:, c0:c0 + cs] = u
        else:
            bc_scr[:, c0 - SSD_D_INNER:c0 - SSD_D_INNER + cs] = u.astype(BF16)
    ubuf[0:pad, :] = ubuf[tr:tr + pad, :]

    xdt_in = dt_ref[...] + dtb_ref[...]
    dtv = jnp.maximum(xdt_in, 0.0) + jnp.log1p(jnp.exp(-jnp.abs(xdt_in)))
    da = dtv * (-jnp.exp(alog_ref[...]))
    r_i = lax.broadcasted_iota(jnp.int32, (tr, tr), 0)
    c_i = lax.broadcasted_iota(jnp.int32, (tr, tr), 1)
    tril = jnp.where((c_i <= r_i) & ((c_i // CHUNK) == (r_i // CHUNK)), 1.0, 0.0).astype(BF16)
    acum = sum(jnp.dot(tril, piece, preferred_element_type=F32) for piece in _split3(da))
    acum_scr[...] = acum

    a_hi, a_lo = _split2(acum)
    d_hi, d_lo = _split2(dtv)
    for c0 in range(0, SSD_D_INNER, cs):
        e = e_ref[:, c0:c0 + cs]
        ab = jnp.dot(a_hi, e, preferred_element_type=F32) + jnp.dot(a_lo, e, preferred_element_type=F32)
        db = jnp.dot(d_hi, e, preferred_element_type=F32) + jnp.dot(d_lo, e, preferred_element_type=F32)
        acb[:, c0:c0 + cs] = ab
        dfs_scr[:, c0:c0 + cs] = jnp.exp(ab)
        xdt = xs_scr[:, c0:c0 + cs] * db
        xdt_scr[:, c0:c0 + cs] = xdt.astype(BF16)
        for c in range(nch):
            r0 = c * CHUNK
            last = ab[r0 + CHUNK - 1:r0 + CHUNK, :]
            xw_scr[r0:r0 + CHUNK, c0:c0 + cs] = (xdt[r0:r0 + CHUNK, :] * jnp.exp(last - ab[r0:r0 + CHUNK, :])).astype(BF16)

    lane2 = lax.broadcasted_iota(jnp.int32, (CHUNK, LANES), 1)
    row2 = lax.broadcasted_iota(jnp.int32, (CHUNK, LANES), 0)
    causal2 = row2 >= (lane2 % SSD_HEAD_DIM)
    lo_half = lane2 < SSD_HEAD_DIM
    zpad = jnp.zeros((LANES - CHUNK, LANES), F32)

    def chunk_body(c, carry):
        r0 = pl.multiple_of(c * CHUNK, CHUNK)
        rows = pl.ds(r0, CHUNK)
        a_c = acum_scr[rows, :]
        a_t = jnp.concatenate([a_c, zpad], axis=0).T
        a_t_r = pltpu.roll(a_t, SSD_HEAD_DIM, 1)
        last_row = acb[pl.ds(r0 + CHUNK - 1, 1), :]
        for g in range(SSD_GROUPS):
            bg = bc_scr[rows, g * SSD_STATE:(g + 1) * SSD_STATE]
            cg = bc_scr[rows, SSD_GROUPS * SSD_STATE + g * SSD_STATE:SSD_GROUPS * SSD_STATE + (g + 1) * SSD_STATE]
            cbt = lax.dot_general(cg, jnp.concatenate([bg, bg], axis=0), (((1,), (1,)), ((), ())),
                                  preferred_element_type=F32)
            g0 = g * gw
            st = state[g]
            yoff = jnp.dot(cg, st.astype(BF16), preferred_element_type=F32) * dfs_scr[rows, g0:g0 + gw]
            for jj in range(gw // LANES):
                pidx = g * (gw // LANES) + jj
                l0 = pidx * LANES
                arow = a_t[2 * pidx:2 * pidx + 1, :] + a_t_r[2 * pidx + 1:2 * pidx + 2, :]
                seg = acb[rows, l0:l0 + LANES] - arow
                decay = jnp.exp(jnp.where(causal2, seg, -jnp.inf))
                mp = (cbt * decay).astype(BF16)
                xd = xdt_scr[rows, l0:l0 + LANES]
                zb = jnp.zeros_like(xd)
                bd = jnp.concatenate([jnp.where(lo_half, xd, zb), jnp.where(lo_half, zb, xd)], axis=0)
                yd = jnp.dot(mp, bd, preferred_element_type=F32)
                y_scr[rows, l0:l0 + LANES] = yd + yoff[:, jj * LANES:(jj + 1) * LANES]
            cd = jnp.exp(last_row[:, g0:g0 + gw])
            upd = lax.dot_general(bg, xw_scr[rows, g0:g0 + gw], (((0,), (0,)), ((), ())),
                                  preferred_element_type=F32)
            state[g] = st * cd + upd
        return carry

    lax.fori_loop(0, nch, chunk_body, 0)

    for g in range(SSD_GROUPS):
        g0 = g * gw
        y = y_scr[:, g0:g0 + gw] + dsk_ref[:, g0:g0 + gw] * xs_scr[:, g0:g0 + gw]
        zz = z_ref[:, g0:g0 + gw].astype(F32)
        y = y * (zz * _sigmoid(zz))
        o_ref[:, g0:g0 + gw] = _rms(y, nw_ref[:, g0:g0 + gw]).astype(BF16)


def _ssd(proj, dt_raw, conv_w, conv_b, dt_bias, a_log, d_skip_b, norm_w, expand, batch, seq, tr=256):
    t = batch * seq
    nr = seq // tr
    kern = functools.partial(_ssd_kernel, tr=tr)
    const = lambda shape: pl.BlockSpec(shape, lambda b, i: (0, 0))
    return pl.pallas_call(
        kern,
        grid=(batch, nr),
        in_specs=[
            pl.BlockSpec((tr, SSD_CONV_DIM), lambda b, i: (b * nr + i, COL_XBC // SSD_CONV_DIM)),
            pl.BlockSpec((tr, SSD_D_INNER), lambda b, i: (b * nr + i, COL_Z // SSD_D_INNER)),
            pl.BlockSpec((tr, LANES), lambda b, i: (b * nr + i, 0)),
            const((SSD_CONV, SSD_CONV_DIM)),
            const((1, SSD_CONV_DIM)),
            const((1, LANES)),
            const((1, LANES)),
            const((1, SSD_D_INNER)),
            const((1, SSD_D_INNER)),
            const((LANES, SSD_D_INNER)),
        ],
        out_specs=pl.BlockSpec((tr, SSD_D_INNER), lambda b, i: (b * nr + i, 0)),
        out_shape=jax.ShapeDtypeStruct((t, SSD_D_INNER), BF16),
        scratch_shapes=[
            pltpu.VMEM((tr + 8, SSD_CONV_DIM), F32),
            pltpu.VMEM((tr, SSD_D_INNER), F32),
            pltpu.VMEM((tr, 2 * SSD_GROUPS * SSD_STATE), BF16),
            pltpu.VMEM((tr, LANES), F32),
            pltpu.VMEM((tr, SSD_D_INNER), F32),
            pltpu.VMEM((tr, SSD_D_INNER), F32),
            pltpu.VMEM((tr, SSD_D_INNER), BF16),
            pltpu.VMEM((tr, SSD_D_INNER), BF16),
            pltpu.VMEM((tr, SSD_D_INNER), F32),
            pltpu.VMEM((SSD_GROUPS, SSD_STATE, SSD_D_INNER // SSD_GROUPS), F32),
        ],
        compiler_params=_cparams(("parallel", "arbitrary")),
        name="ssd",
    )(proj, proj, dt_raw, conv_w, conv_b, dt_bias, a_log, d_skip_b, norm_w, expand)


def _merge_kernel(ya_ref, ys_ref, ga_ref, gs_ref, x_ref, wa_ref, wb_ref, wo_ref, o_ref):
    a = jnp.dot(ya_ref[...], wa_ref[...], preferred_element_type=F32)
    s = jnp.dot(ys_ref[...], wb_ref[...], preferred_element_type=F32)
    merged = _sigmoid(ga_ref[...].astype(F32)) * a + _sigmoid(gs_ref[...].astype(F32)) * s
    o_ref[...] = x_ref[...] + jnp.dot(merged.astype(BF16), wo_ref[...], preferred_element_type=F32)


def _merge(y_attn, y_ssd, proj, x2d, w_a, w_b, w_o, tm=512):
    t = x2d.shape[0]
    tm = min(tm, t)
    full = lambda r, c: pl.BlockSpec((r, c), lambda i: (0, 0))
    return pl.pallas_call(
        _merge_kernel,
        grid=(t // tm,),
        in_specs=[
            pl.BlockSpec((tm, D_MODEL), lambda i: (i, 0)),
            pl.BlockSpec((tm, SSD_D_INNER), lambda i: (i, 0)),
            pl.BlockSpec((tm, D_MODEL), lambda i: (i, COL_GA // D_MODEL)),
            pl.BlockSpec((tm, D_MODEL), lambda i: (i, COL_GS // D_MODEL)),
            pl.BlockSpec((tm, D_MODEL), lambda i: (i, 0)),
            full(D_MODEL, D_MODEL), full(SSD_D_INNER, D_MODEL), full(D_MODEL, D_MODEL),
        ],
        out_specs=pl.BlockSpec((tm, D_MODEL), lambda i: (i, 0)),
        out_shape=jax.ShapeDtypeStruct((t, D_MODEL), F32),
        compiler_params=_cparams(("parallel",)),
        name="merge",
    )(y_attn, y_ssd, proj, proj, x2d, w_a, w_b, w_o)


def _router_kernel(x_ref, nw_ref, rw_ref, rb_ref, h_ref, idx_ref, gate_ref, rank_ref, cnt_ref, carry_ref, *, tt):
    i = pl.program_id(0)

    @pl.when(i == 0)
    def _():
        carry_ref[...] = jnp.zeros_like(carry_ref)

    h = _rms(x_ref[...], nw_ref[...])
    h_ref[...] = h
    logits = jnp.dot(h, rw_ref[...], preferred_element_type=F32, precision=lax.Precision.HIGHEST) + rb_ref[...]
    lane = lax.broadcasted_iota(jnp.int32, (tt, LANES), 1)
    work = jnp.where(lane < N_EXPERTS, logits, -jnp.inf)

    vals, idxs, hots = [], [], []
    for _ in range(TOP_K):
        m = jnp.max(work, axis=-1, keepdims=True)
        idx = jnp.min(jnp.where(work == m, lane, LANES), axis=-1, keepdims=True)
        hot = lane == idx
        vals.append(m)
        idxs.append(idx)
        hots.append(hot)
        work = jnp.where(hot, -jnp.inf, work)

    exps = [jnp.exp(v - vals[0]) for v in vals]
    denom = exps[0] + exps[1] + exps[2] + exps[3]

    hot_sum = sum(jnp.where(hot, 1.0, 0.0) for hot in hots)
    r_i = lax.broadcasted_iota(jnp.int32, (tt, tt), 0)
    c_i = lax.broadcasted_iota(jnp.int32, (tt, tt), 1)
    strict = jnp.where(c_i < r_i, 1.0, 0.0).astype(BF16)
    prefix = jnp.dot(strict, hot_sum.astype(BF16), preferred_element_type=F32) + carry_ref[...]

    idx_out = jnp.zeros((tt, LANES), jnp.int32)
    rank_out = jnp.zeros((tt, LANES), jnp.int32)
    gate_out = jnp.zeros((tt, LANES), F32)
    for k in range(TOP_K):
        rank_k = jnp.sum(jnp.where(hots[k], prefix, 0.0), axis=-1, keepdims=True).astype(jnp.int32)
        sel = lane == k
        idx_out = jnp.where(sel, idxs[k], idx_out)
        rank_out = jnp.where(sel, rank_k, rank_out)
        gate_out = jnp.where(sel, exps[k] / denom, gate_out)
    idx_ref[...] = idx_out
    rank_ref[...] = rank_out
    gate_ref[...] = gate_out

    carry_ref[...] = carry_ref[...] + jnp.sum(hot_sum, axis=0, keepdims=True)
    cnt_ref[...] = carry_ref[...]


def _router(x1, norm_w, router_w_pad, router_b_pad, tt=512):
    t = x1.shape[0]
    tt = min(tt, t)
    kern = functools.partial(_router_kernel, tt=tt)
    row = pl.BlockSpec((tt, LANES), lambda i: (i, 0))
    return pl.pallas_call(
        kern,
        grid=(t // tt,),
        in_specs=[
            pl.BlockSpec((tt, D_MODEL), lambda i: (i, 0)),
            pl.BlockSpec((1, D_MODEL), lambda i: (0, 0)),
            pl.BlockSpec((D_MODEL, LANES), lambda i: (0, 0)),
            pl.BlockSpec((1, LANES), lambda i: (0, 0)),
        ],
        out_specs=[pl.BlockSpec((tt, D_MODEL), lambda i: (i, 0)), row, row, row,
                   pl.BlockSpec((1, LANES), lambda i: (0, 0))],
        out_shape=[
            jax.ShapeDtypeStruct((t, D_MODEL), F32),
            jax.ShapeDtypeStruct((t, LANES), jnp.int32),
            jax.ShapeDtypeStruct((t, LANES), F32),
            jax.ShapeDtypeStruct((t, LANES), jnp.int32),
            jax.ShapeDtypeStruct((1, LANES), F32),
        ],
        scratch_shapes=[pltpu.VMEM((1, LANES), F32)],
        compiler_params=_cparams(("arbitrary",)),
        name="router",
    )(x1, norm_w, router_w_pad, router_b_pad)


def _dispatch_kernel(pos_ref, h_ref, o_ref, sem, *, tt):
    def copy(t, k):
        return pltpu.make_async_copy(h_ref.at[pl.ds(t, 1), :], o_ref.at[pl.ds(pos_ref[t * TOP_K + k], 1), :], sem)

    def issue(t, c):
        for k in range(TOP_K):
            copy(t, k).start()
        return c

    lax.fori_loop(0, tt, issue, 0)

    def drain(t, c):
        for k in range(TOP_K):
            copy(t, k).wait()
        return c

    lax.fori_loop(0, tt, drain, 0)


def _dispatch(h2, pos_flat, tt=256):
    t = h2.shape[0]
    tt = min(tt, t)
    kern = functools.partial(_dispatch_kernel, tt=tt)
    return pl.pallas_call(
        kern,
        grid=(t // tt,),
        in_specs=[
            pl.BlockSpec((tt * TOP_K,), lambda i: (i,), memory_space=pltpu.SMEM),
            pl.BlockSpec((tt, D_MODEL), lambda i: (i, 0)),
        ],
        out_specs=pl.BlockSpec(memory_space=pl.ANY),
        out_shape=jax.ShapeDtypeStruct((t * TOP_K, D_MODEL), F32),
        scratch_shapes=[pltpu.SemaphoreType.DMA(())],
        compiler_params=_cparams(("arbitrary",)),
        name="dispatch",
    )(pos_flat, h2)


def _experts_kernel(tile_ref, exp_ref, first_ref, valid_ref, start_ref,
                    x_ref, wu_ref, bu_ref, wd_ref, bd_ref, o_ref, *, tm):
    w = pl.program_id(0)

    @pl.when(valid_ref[w] == 1)
    def _():
        e = exp_ref[w]
        gu = jnp.dot(x_ref[...].astype(BF16), wu_ref[0], preferred_element_type=F32) + bu_ref[0]
        g = jnp.minimum(gu[:, :D_EXPERT], SWIGLU_LIMIT)
        u = jnp.clip(gu[:, D_EXPERT:], -SWIGLU_LIMIT, SWIGLU_LIMIT)
        act = (u + 1.0) * (g * _sigmoid(SWIGLU_ALPHA * g))
        out = jnp.dot(act.astype(BF16), wd_ref[0], preferred_element_type=F32) + bd_ref[0]
        rows = tile_ref[w] * tm + lax.broadcasted_iota(jnp.int32, (tm, 1), 0)
        mine = (rows >= start_ref[e]) & (rows < start_ref[e + 1])

        @pl.when(first_ref[w] == 1)
        def _():
            o_ref[...] = jnp.where(mine, out, 0.0)

        @pl.when(first_ref[w] == 0)
        def _():
            o_ref[...] = jnp.where(mine, out, o_ref[...])


def _experts(x_sorted, item_tile, item_exp, item_first, item_valid, starts, w_up, b_up, w_down, b_down, tm):
    n = x_sorted.shape[0]
    n_items = item_tile.shape[0]
    kern = functools.partial(_experts_kernel, tm=tm)
    grid_spec = pltpu.PrefetchScalarGridSpec(
        num_scalar_prefetch=5,
        grid=(n_items,),
        in_specs=[
            pl.BlockSpec((tm, D_MODEL), lambda w, tl, ex, fi, va, st: (tl[w], 0)),
            pl.BlockSpec((1, D_MODEL, 2 * D_EXPERT), lambda w, tl, ex, fi, va, st: (ex[w], 0, 0)),
            pl.BlockSpec((1, 1, 2 * D_EXPERT), lambda w, tl, ex, fi, va, st: (ex[w], 0, 0)),
            pl.BlockSpec((1, D_EXPERT, D_MODEL), lambda w, tl, ex, fi, va, st: (ex[w], 0, 0)),
            pl.BlockSpec((1, 1, D_MODEL), lambda w, tl, ex, fi, va, st: (ex[w], 0, 0)),
        ],
        out_specs=pl.BlockSpec((tm, D_MODEL), lambda w, tl, ex, fi, va, st: (tl[w], 0)),
    )
    return pl.pallas_call(
        kern,
        grid_spec=grid_spec,
        out_shape=jax.ShapeDtypeStruct((n, D_MODEL), F32),
        compiler_params=_cparams(("arbitrary",)),
        name="experts",
    )(item_tile, item_exp, item_first, item_valid, starts, x_sorted, w_up, b_up, w_down, b_down)


def _combine_kernel(pos_ref, posn_ref, gate_ref, x_ref, p_ref, pnw_ref, wpg_ref, wpp_ref, fnw_ref, ys_ref, o_ref,
                    rows_ref, sem, *, tt):
    i = pl.program_id(0)
    n = pl.num_programs(0)
    slot = i % 2

    def gather(idx_ref, dst_slot):
        def issue(t, c):
            for k in range(TOP_K):
                pltpu.make_async_copy(ys_ref.at[pl.ds(idx_ref[t * TOP_K + k], 1), :],
                                      rows_ref.at[dst_slot, pl.ds(k * tt + t, 1), :], sem.at[dst_slot]).start()
            return c

        lax.fori_loop(0, tt, issue, 0)

    @pl.when(i == 0)
    def _():
        gather(pos_ref, 0)

    @pl.when(i + 1 < n)
    def _():
        gather(posn_ref, 1 - slot)

    pltpu.make_async_copy(ys_ref.at[pl.ds(0, TOP_K * tt), :], rows_ref.at[slot], sem.at[slot]).wait()

    gate = gate_ref[...]
    x = x_ref[...]
    for k in range(TOP_K):
        x = x + gate[:, k:k + 1] * rows_ref[slot, pl.ds(k * tt, tt), :]

    hg = _rms(x, pnw_ref[...]).astype(BF16)
    pg = _sigmoid(jnp.dot(hg, wpg_ref[...], preferred_element_type=F32))
    x = x + pg * jnp.dot(p_ref[...].astype(BF16), wpp_ref[...], preferred_element_type=F32)
    o_ref[...] = _rms(x, fnw_ref[...])


def _combine(pos_flat, gate, x1, p2d, ple_nw, w_pg, w_pp, final_nw, y_sorted, tt=256):
    t = x1.shape[0]
    tt = min(tt, t)
    kern = functools.partial(_combine_kernel, tt=tt)
    vec = pl.BlockSpec((1, D_MODEL), lambda i: (0, 0))
    nt = t // tt
    return pl.pallas_call(
        kern,
        grid=(nt,),
        in_specs=[
            pl.BlockSpec((tt * TOP_K,), lambda i: (i,), memory_space=pltpu.SMEM),
            pl.BlockSpec((tt * TOP_K,), lambda i: (jnp.minimum(i + 1, nt - 1),), memory_space=pltpu.SMEM),
            pl.BlockSpec((tt, LANES), lambda i: (i, 0)),
            pl.BlockSpec((tt, D_MODEL), lambda i: (i, 0)),
            pl.BlockSpec((tt, PLE_DIM), lambda i: (i, 0)),
            vec,
            pl.BlockSpec((D_MODEL, D_MODEL), lambda i: (0, 0)),
            pl.BlockSpec((PLE_DIM, D_MODEL), lambda i: (0, 0)),
            vec,
            pl.BlockSpec(memory_space=pl.ANY),
        ],
        out_specs=pl.BlockSpec((tt, D_MODEL), lambda i: (i, 0)),
        out_shape=jax.ShapeDtypeStruct((t, D_MODEL), F32),
        scratch_shapes=[pltpu.VMEM((2, TOP_K * tt, D_MODEL), F32), pltpu.SemaphoreType.DMA((2,))],
        compiler_params=_cparams(("arbitrary",)),
        name="combine",
    )(pos_flat, pos_flat, gate, x1, p2d, ple_nw, w_pg, w_pp, final_nw, y_sorted)


def _routing_plan(counts, idx, rank, tm, n_rows):
    counts = counts.astype(jnp.int32)
    ends = jnp.cumsum(counts)
    starts = ends - counts
    pos = (starts[idx] + rank).reshape(-1)
    n_tiles = n_rows // tm
    n_items = n_tiles + N_EXPERTS - 1
    first_tile = starts // tm
    last_tile = jnp.maximum(ends - 1, 0) // tm
    items_e = jnp.where(counts > 0, last_tile - first_tile + 1, 0)
    item_end = jnp.cumsum(items_e)
    item_start = item_end - items_e
    total = item_end[-1]
    w = jnp.arange(n_items, dtype=jnp.int32)
    e_w = jnp.minimum(jnp.sum(w[:, None] >= item_end[None, :], axis=1).astype(jnp.int32), N_EXPERTS - 1)
    tile_w = first_tile[e_w] + (w - item_start[e_w])
    valid = w < total
    last = jnp.maximum(total - 1, 0)
    e_w = jnp.where(valid, e_w, e_w[last])
    tile_w = jnp.where(valid, tile_w, tile_w[last])
    prev_tile = jnp.concatenate([jnp.full((1,), -1, jnp.int32), tile_w[:-1]])
    first = (tile_w != prev_tile) & valid
    starts_ext = jnp.concatenate([starts, ends[-1:]]).astype(jnp.int32)
    return pos.astype(jnp.int32), tile_w.astype(jnp.int32), e_w, first.astype(jnp.int32), valid.astype(jnp.int32), starts_ext


def kernel(x, p, mix_norm_w, w_in, lambda_q1, lambda_k1, lambda_q2, lambda_k2, da_head_norm_w, w_attn_branch, conv_w, conv_b, dt_bias, a_log, d_skip, ssd_norm_w, w_ssd_branch, w_out, moe_norm_w, router_w, router_b, w_up, b_up, w_down, b_down, ple_norm_w, w_ple_gate, w_ple_proj, final_norm_w):
    batch, seq, _ = x.shape
    t = batch * seq
    depth = w_in.shape[0]
    assert depth == 1, "the final RMSNorm is fused into the layer's last kernel"
    x2d = x.reshape(t, D_MODEL)

    log2e = math.log2(math.e)
    slopes = jnp.asarray([log2e * 2.0 ** (-8.0 * (h + 1) / DA_HEADS) for h in range(DA_HEADS)], F32)
    q_scale = log2e * DA_QK_DIM ** -0.5
    head_ids = jnp.arange(SSD_D_INNER, dtype=jnp.int32) // SSD_HEAD_DIM
    expand = (jnp.arange(LANES, dtype=jnp.int32)[:, None] == head_ids[None, :]).astype(BF16)

    def pad_lanes(v, fill=0.0):
        return jnp.pad(v.reshape(1, -1), ((0, 0), (0, LANES - v.shape[-1])), constant_values=fill)

    for i in range(depth):
        wi = w_in[i]
        w_main = jnp.concatenate([
            wi[:, _R_Z:_R_Z + SSD_D_INNER], wi[:, _R_Q:_R_Q + 1024] * q_scale, wi[:, _R_K:_R_K + 1024],
            wi[:, _R_GA:_R_GA + 1024], wi[:, _R_GS:_R_GS + 1024], wi[:, _R_XBC:_R_XBC + SSD_CONV_DIM]],
            axis=1).astype(BF16)
        w_dt = jnp.pad(wi[:, _R_DT:_R_DT + SSD_HEADS], ((0, 0), (0, LANES - SSD_HEADS))).astype(BF16)
        w_vt = wi[:, _R_V:_R_V + 1024].T.astype(BF16)

        proj, dt_raw, v_t = _in_proj(x2d, mix_norm_w[i].reshape(1, -1), w_main, w_dt, w_vt)

        lambda_init = 0.8 - 0.6 * math.exp(-0.3 * i)
        lam = (jnp.exp(jnp.sum(lambda_q1[i] * lambda_k1[i])) - jnp.exp(jnp.sum(lambda_q2[i] * lambda_k2[i]))
               + lambda_init).reshape(1).astype(F32)
        y_attn = _diff_attn(proj, v_t, slopes, lam, da_head_norm_w[i].reshape(1, -1), batch, seq, lambda_init)

        y_ssd = _ssd(proj, dt_raw, conv_w[i], conv_b[i].reshape(1, -1), pad_lanes(dt_bias[i]), pad_lanes(a_log[i]),
                     jnp.repeat(d_skip[i], SSD_HEAD_DIM).reshape(1, -1), ssd_norm_w[i].reshape(1, -1), expand,
                     batch, seq)

        x1 = _merge(y_attn, y_ssd, proj, x2d, w_attn_branch[i].astype(BF16), w_ssd_branch[i].astype(BF16),
                    w_out[i].astype(BF16))

        rw = jnp.pad(router_w[i], ((0, 0), (0, LANES - N_EXPERTS)))
        h2, idx, gate, rank, counts = _router(x1, moe_norm_w[i].reshape(1, -1), rw, pad_lanes(router_b[i]))

        tm = 512
        n_rows = t * TOP_K
        pos, item_tile, item_exp, item_first, item_valid, starts = _routing_plan(
            counts[0, :N_EXPERTS], idx[:, :TOP_K], rank[:, :TOP_K], tm, n_rows)

        x_sorted = _dispatch(h2, pos)
        y_sorted = _experts(x_sorted, item_tile, item_exp, item_first, item_valid, starts,
                            w_up[i].astype(BF16), b_up[i].reshape(N_EXPERTS, 1, -1),
                            w_down[i].astype(BF16), b_down[i].reshape(N_EXPERTS, 1, -1), tm)

        x2d = _combine(pos, gate, x1, p[i].reshape(t, PLE_DIM), ple_norm_w[i].reshape(1, -1),
                       w_ple_gate[i].astype(BF16), w_ple_proj[i].astype(BF16), final_norm_w.reshape(1, -1), y_sorted)
    return x2d.reshape(batch, seq, D_MODEL)
```

```python
import functools
import math

import jax
import jax.numpy as jnp
from jax import lax
from jax.experimental import pallas as pl
from jax.experimental.pallas import tpu as pltpu

F32 = jnp.float32
BF16 = jnp.bfloat16

D_MODEL = 1024
CHUNK = 64
PLE_DIM = 256
RMS_EPS = 1e-6

DA_HEADS = 8
DA_QK_DIM = 64
DA_V_DIM = 128

SSD_D_INNER = 2048
SSD_HEAD_DIM = 64
SSD_HEADS = 32
SSD_GROUPS = 4
SSD_STATE = 128
SSD_CONV = 4
SSD_CONV_DIM = 3072

N_EXPERTS = 32
TOP_K = 4
D_EXPERT = 1024
SWIGLU_LIMIT = 7.0
SWIGLU_ALPHA = 1.702

LANES = 128

COL_Z = 0
COL_Q = 2048
COL_K = 3072
COL_GA = 4096
COL_GS = 5120
COL_XBC = 6144
PROJ_W = 9216

ATTN_TILE = 256
ATTN_Q_TILE = 512

_R_Q, _R_K, _R_V, _R_Z, _R_XBC, _R_DT, _R_GA, _R_GS = 0, 1024, 2048, 3072, 5120, 8192, 8224, 9248

VMEM_LIMIT = 56 * 1024 * 1024


def _cparams(sem):
    return pltpu.CompilerParams(dimension_semantics=sem, vmem_limit_bytes=VMEM_LIMIT)


def _rms(x, w):
    return x * lax.rsqrt(jnp.mean(x * x, axis=-1, keepdims=True) + RMS_EPS) * w


def _sigmoid(x):
    return 1.0 / (1.0 + jnp.exp2(x * (-math.log2(math.e))))


def _inproj_kernel(x_ref, nw_ref, w_ref, wdt_ref, o_ref, dt_ref, vt_ref, h_ref, *, tm, n_proj):
    j = pl.program_id(1)

    @pl.when(j == 0)
    def _():
        hb = _rms(x_ref[...], nw_ref[...]).astype(BF16)
        h_ref[...] = hb
        dt_ref[...] = jnp.dot(hb, wdt_ref[...], preferred_element_type=F32)

    acc = jnp.dot(h_ref[...], w_ref[...], preferred_element_type=F32)

    @pl.when(j < n_proj)
    def _():
        o_ref[...] = acc.astype(BF16)

    @pl.when(j == n_proj)
    def _():
        vt = acc.T
        for c in range(tm // ATTN_TILE):
            vt_ref[c] = vt[:, c * ATTN_TILE:(c + 1) * ATTN_TILE].astype(BF16)


def _in_proj(x2d, norm_w, w_main, w_dt, tm=1024):
    t = x2d.shape[0]
    tm = min(tm, t)
    tn = DA_HEADS * DA_V_DIM
    n_proj = PROJ_W // tn
    return pl.pallas_call(
        functools.partial(_inproj_kernel, tm=tm, n_proj=n_proj),
        grid=(t // tm, n_proj + 1),
        in_specs=[
            pl.BlockSpec((tm, D_MODEL), lambda i, j: (i, 0)),
            pl.BlockSpec((1, D_MODEL), lambda i, j: (0, 0)),
            pl.BlockSpec((D_MODEL, tn), lambda i, j: (0, j)),
            pl.BlockSpec((D_MODEL, LANES), lambda i, j: (0, 0)),
        ],
        out_specs=[
            pl.BlockSpec((tm, tn), lambda i, j: (i, jnp.minimum(j, n_proj - 1))),
            pl.BlockSpec((tm, LANES), lambda i, j: (i, 0)),
            pl.BlockSpec((tm // ATTN_TILE, tn, ATTN_TILE), lambda i, j: (i, 0, 0)),
        ],
        out_shape=[
            jax.ShapeDtypeStruct((t, PROJ_W), BF16),
            jax.ShapeDtypeStruct((t, LANES), F32),
            jax.ShapeDtypeStruct((t // ATTN_TILE, tn, ATTN_TILE), BF16),
        ],
        scratch_shapes=[pltpu.VMEM((tm, D_MODEL), BF16)],
        compiler_params=_cparams(("parallel", "arbitrary")),
        name="in_proj",
    )(x2d, norm_w, w_main, w_dt)


def _attn_kernel(slopes_ref, lam_ref, q_ref, k_ref, vt_ref, nw_ref, o_ref,
                 bias_ref, qc_ref, s_ref, p_ref, al_ref, m_ref, l_ref, a_ref, *, tq, tk, out_scale):
    h = pl.program_id(1)
    i = pl.program_id(2)
    slope = slopes_ref[h]
    lam = lam_ref[0]
    r = tq // tk
    n_tiles = r * (i + 1)

    @pl.when(i == 0)
    def _():
        kk = lax.broadcasted_iota(jnp.int32, (tk, tq), 0)
        qq = lax.broadcasted_iota(jnp.int32, (tk, tq), 1)
        bias_ref[0] = -slope * (qq - kk).astype(F32)
        for d in range(r):
            kd = kk + d * tk
            bias_ref[1 + d] = jnp.where((kd // CHUNK) <= (qq // CHUNK), -slope * jnp.abs(qq - kd).astype(F32), -jnp.inf)

    qt = q_ref[...].astype(F32).T.astype(BF16)
    dim = lax.broadcasted_iota(jnp.int32, qt.shape, 0)
    zero = jnp.zeros_like(qt)
    qc_ref[...] = jnp.concatenate([jnp.where(dim < DA_QK_DIM, qt, zero), jnp.where(dim >= DA_QK_DIM, qt, zero)], axis=1)

    m_ref[...] = jnp.full(m_ref.shape, -jnp.inf, F32)
    l_ref[...] = jnp.zeros(l_ref.shape, F32)
    a_ref[...] = jnp.zeros(a_ref.shape, F32)
    al_ref[...] = jnp.zeros(al_ref.shape, F32)
    p_ref[...] = jnp.zeros(p_ref.shape, BF16)

    def scores(j):
        k = k_ref[pl.ds(pl.multiple_of(j * tk, tk), tk), :]
        return jnp.dot(k, qc_ref[...], preferred_element_type=F32)

    def softmax_step(sel, cj):
        for g in range(2 * tq // LANES):
            cols = slice(g * LANES, (g + 1) * LANES)
            bcols = slice((g * LANES) % tq, (g * LANES) % tq + LANES)
            s = s_ref[:, cols] + bias_ref[sel, :, bcols]
            m_prev = m_ref[:, cols]
            m_cur = jnp.maximum(m_prev, jnp.max(s, axis=0, keepdims=True) + cj)
            alpha = jnp.exp2(m_prev - m_cur)
            p = jnp.exp2(s + (cj - m_cur))
            m_ref[:, cols] = m_cur
            l_ref[:, cols] = alpha * l_ref[:, cols] + jnp.sum(p, axis=0, keepdims=True)
            al_ref[:, cols] = alpha
            p_ref[:, cols] = p.astype(BF16)

    def pv_step(j):
        vt = vt_ref[j]
        for g in range(2 * tq // 256):
            cols = slice(g * 256, (g + 1) * 256)
            a_ref[:, cols] = al_ref[:, cols] * a_ref[:, cols] + jnp.dot(vt, p_ref[:, cols], preferred_element_type=F32)

    s_ref[...] = scores(0)

    def body(j, carry):
        s_next = scores(jnp.minimum(j + 1, n_tiles - 1))
        pv_step(jnp.maximum(j - 1, 0))
        past = j < r * i
        cj = jnp.where(past, -slope * (i * tq - j * tk).astype(F32), 0.0)
        softmax_step(jnp.maximum(j - r * i + 1, 0), cj)
        s_ref[...] = s_next
        return carry

    lax.fori_loop(0, n_tiles, body, 0)
    pv_step(n_tiles - 1)

    on = a_ref[...] / l_ref[...]
    ot = on[:, :tq] - lam * on[:, tq:]
    ot = ot * lax.rsqrt(jnp.mean(ot * ot, axis=0, keepdims=True) + RMS_EPS)
    o_ref[...] = (ot.T * nw_ref[...] * out_scale).astype(BF16)


def _diff_attn(proj, v_t, slopes, lam, head_norm_w, batch, seq, lambda_init):
    tk = ATTN_TILE
    tq = min(ATTN_Q_TILE, seq)
    nq = seq // tq
    nk = seq // tk
    t = batch * seq
    qb, kb = COL_Q // LANES, COL_K // LANES
    assert tq % tk == 0
    kern = functools.partial(_attn_kernel, tq=tq, tk=tk, out_scale=1.0 - lambda_init)
    smem = pl.BlockSpec(memory_space=pltpu.SMEM)
    return pl.pallas_call(
        kern,
        grid=(batch, DA_HEADS, nq),
        in_specs=[
            smem, smem,
            pl.BlockSpec((tq, LANES), lambda b, h, i: (b * nq + i, qb + h)),
            pl.BlockSpec((seq, LANES), lambda b, h, i: (b, kb + h)),
            pl.BlockSpec((nk, DA_V_DIM, tk), lambda b, h, i: (b, h, 0)),
            pl.BlockSpec((1, LANES), lambda b, h, i: (0, 0)),
        ],
        out_specs=pl.BlockSpec((tq, LANES), lambda b, h, i: (b * nq + i, h)),
        out_shape=jax.ShapeDtypeStruct((t, DA_HEADS * DA_V_DIM), BF16),
        scratch_shapes=[
            pltpu.VMEM((1 + tq // tk, tk, tq), F32),
            pltpu.VMEM((LANES, 2 * tq), BF16),
            pltpu.VMEM((tk, 2 * tq), F32),
            pltpu.VMEM((tk, 2 * tq), BF16),
            pltpu.VMEM((1, 2 * tq), F32),
            pltpu.VMEM((1, 2 * tq), F32),
            pltpu.VMEM((1, 2 * tq), F32),
            pltpu.VMEM((DA_V_DIM, 2 * tq), F32),
        ],
        compiler_params=_cparams(("parallel", "parallel", "arbitrary")),
        name="diff_attn",
    )(slopes, lam, proj, proj, v_t, head_norm_w)


def _split2(x):
    hi = x.astype(BF16)
    lo = (x - hi.astype(F32)).astype(BF16)
    return hi, lo


def _split3(x):
    hi = x.astype(BF16)
    r = x - hi.astype(F32)
    mid = r.astype(BF16)
    lo = (r - mid.astype(F32)).astype(BF16)
    return hi, mid, lo


def _ssd_kernel(xbc_ref, z_ref, dt_ref, cw_ref, cb_ref, dtb_ref, alog_ref, dsk_ref, nw_ref, e_ref,
                o_ref, ubuf, shift_ref, xs_scr, bc_scr, acum_scr, acb, dfs_scr, xdt_scr, xw_scr, y_scr, state, *, tr):
    i = pl.program_id(1)
    nch = tr // CHUNK
    gw = SSD_D_INNER // SSD_GROUPS
    pad = 8

    head = 16

    @pl.when(i == 0)
    def _():
        ubuf[0:pad, :] = jnp.zeros((pad, SSD_CONV_DIM), F32)
        state[...] = jnp.zeros_like(state)
        r_s = lax.broadcasted_iota(jnp.int32, (tr, tr), 0)
        c_s = lax.broadcasted_iota(jnp.int32, (tr, tr), 1)
        for d in range(1, SSD_CONV):
            shift_ref[(d - 1) * tr:d * tr, :] = jnp.where(c_s == r_s - d, 1.0, 0.0).astype(BF16)

    ubuf[pad:pad + head, :] = xbc_ref[0:head, :].astype(F32)
    cs = 512
    for c0 in range(0, SSD_CONV_DIM, cs):
        ub = xbc_ref[:, c0:c0 + cs]
        sh = jnp.dot(shift_ref[...], ub, preferred_element_type=F32)
        acc = cb_ref[:, c0:c0 + cs] + cw_ref[SSD_CONV - 1:SSD_CONV, c0:c0 + cs] * ub.astype(F32)
        for d in range(1, SSD_CONV):
            acc = acc + cw_ref[SSD_CONV - 1 - d:SSD_CONV - d, c0:c0 + cs] * sh[(d - 1) * tr:d * tr, :]
        u = acc * _sigmoid(acc)
        acc_h = jnp.broadcast_to(cb_ref[:, c0:c0 + cs], (head, cs))
        for k in range(SSD_CONV):
            off = pad - (SSD_CONV - 1) + k
            acc_h = acc_h + cw_ref[k:k + 1, c0:c0 + cs] * ubuf[off:off + head, c0:c0 + cs]
        u_h = acc_h * _sigmoid(acc_h)
        if c0 < SSD_D_INNER:
            xs_scr[:, c0:c0 + cs] = u
            xs_scr[0:head, c0:c0 + cs] = u_h
        else:
            bc_scr[:, c0 - SSD_D_INNER:c0 - SSD_D_INNER + cs] = u.astype(BF16)
            bc_scr[0:head, c0 - SSD_D_INNER:c0 - SSD_D_INNER + cs] = u_h.astype(BF16)
    ubuf[0:pad, :] = xbc_ref[tr - 2 * pad:tr, :].astype(F32)[pad:, :]

    xdt_in = dt_ref[...] + dtb_ref[...]
    dtv = jnp.maximum(xdt_in, 0.0) + jnp.log1p(jnp.exp(-jnp.abs(xdt_in)))
    da = dtv * (-jnp.exp(alog_ref[...]))
    r_i = lax.broadcasted_iota(jnp.int32, (tr, tr), 0)
    c_i = lax.broadcasted_iota(jnp.int32, (tr, tr), 1)
    tril = jnp.where((c_i <= r_i) & ((c_i // CHUNK) == (r_i // CHUNK)), 1.0, 0.0).astype(BF16)
    acum = sum(jnp.dot(tril, piece, preferred_element_type=F32) for piece in _split3(da))
    acum_scr[...] = acum

    a_hi, a_lo = _split2(acum)
    d_hi, d_lo = _split2(dtv)
    for c0 in range(0, SSD_D_INNER, cs):
        e = e_ref[:, c0:c0 + cs]
        ab = jnp.dot(a_hi, e, preferred_element_type=F32) + jnp.dot(a_lo, e, preferred_element_type=F32)
        db = jnp.dot(d_hi, e, preferred_element_type=F32) + jnp.dot(d_lo, e, preferred_element_type=F32)
        acb[:, c0:c0 + cs] = ab
        dfs_scr[:, c0:c0 + cs] = jnp.exp(ab)
        xdt = xs_scr[:, c0:c0 + cs] * db
        xdt_scr[:, c0:c0 + cs] = xdt.astype(BF16)
        for c in range(nch):
            r0 = c * CHUNK
            last = ab[r0 + CHUNK - 1:r0 + CHUNK, :]
            xw_scr[r0:r0 + CHUNK, c0:c0 + cs] = (xdt[r0:r0 + CHUNK, :] * jnp.exp(last - ab[r0:r0 + CHUNK, :])).astype(BF16)

    lane2 = lax.broadcasted_iota(jnp.int32, (CHUNK, LANES), 1)
    row2 = lax.broadcasted_iota(jnp.int32, (CHUNK, LANES), 0)
    causal2 = row2 >= (lane2 % SSD_HEAD_DIM)
    lo_half = lane2 < SSD_HEAD_DIM
    zpad = jnp.zeros((LANES - CHUNK, LANES), F32)

    def chunk_body(c, carry):
        r0 = pl.multiple_of(c * CHUNK, CHUNK)
        rows = pl.ds(r0, CHUNK)
        a_c = acum_scr[rows, :]
        a_t = jnp.concatenate([a_c, zpad], axis=0).T
        a_t_r = pltpu.roll(a_t, SSD_HEAD_DIM, 1)
        last_row = acb[pl.ds(r0 + CHUNK - 1, 1), :]
        for g in range(SSD_GROUPS):
            bg = bc_scr[rows, g * SSD_STATE:(g + 1) * SSD_STATE]
            cg = bc_scr[rows, SSD_GROUPS * SSD_STATE + g * SSD_STATE:SSD_GROUPS * SSD_STATE + (g + 1) * SSD_STATE]
            cbt = lax.dot_general(cg, jnp.concatenate([bg, bg], axis=0), (((1,), (1,)), ((), ())),
                                  preferred_element_type=F32)
            g0 = g * gw
            st = state[g]
            yoff = jnp.dot(cg, st.astype(BF16), preferred_element_type=F32) * dfs_scr[rows, g0:g0 + gw]
            for jj in range(gw // LANES):
                pidx = g * (gw // LANES) + jj
                l0 = pidx * LANES
                arow = a_t[2 * pidx:2 * pidx + 1, :] + a_t_r[2 * pidx + 1:2 * pidx + 2, :]
                seg = acb[rows, l0:l0 + LANES] - arow
                decay = jnp.exp(jnp.where(causal2, seg, -jnp.inf))
                mp = (cbt * decay).astype(BF16)
                xd = xdt_scr[rows, l0:l0 + LANES]
                zb = jnp.zeros_like(xd)
                bd = jnp.concatenate([jnp.where(lo_half, xd, zb), jnp.where(lo_half, zb, xd)], axis=0)
                yd = jnp.dot(mp, bd, preferred_element_type=F32)
                y_scr[rows, l0:l0 + LANES] = yd + yoff[:, jj * LANES:(jj + 1) * LANES]
            cd = jnp.exp(last_row[:, g0:g0 + gw])
            upd = lax.dot_general(bg, xw_scr[rows, g0:g0 + gw], (((0,), (0,)), ((), ())),
                                  preferred_element_type=F32)
            state[g] = st * cd + upd
        return carry

    lax.fori_loop(0, nch, chunk_body, 0)

    for g in range(SSD_GROUPS):
        g0 = g * gw
        y = y_scr[:, g0:g0 + gw] + dsk_ref[:, g0:g0 + gw] * xs_scr[:, g0:g0 + gw]
        zz = z_ref[:, g0:g0 + gw].astype(F32)
        y = y * (zz * _sigmoid(zz))
        o_ref[:, g0:g0 + gw] = _rms(y, nw_ref[:, g0:g0 + gw]).astype(BF16)


def _ssd(proj, dt_raw, conv_w, conv_b, dt_bias, a_log, d_skip_b, norm_w, expand, batch, seq, tr=256):
    t = batch * seq
    nr = seq // tr
    kern = functools.partial(_ssd_kernel, tr=tr)
    const = lambda shape: pl.BlockSpec(shape, lambda b, i: (0, 0))
    return pl.pallas_call(
        kern,
        grid=(batch, nr),
        in_specs=[
            pl.BlockSpec((tr, SSD_CONV_DIM), lambda b, i: (b * nr + i, COL_XBC // SSD_CONV_DIM)),
            pl.BlockSpec((tr, SSD_D_INNER), lambda b, i: (b * nr + i, COL_Z // SSD_D_INNER)),
            pl.BlockSpec((tr, LANES), lambda b, i: (b * nr + i, 0)),
            const((SSD_CONV, SSD_CONV_DIM)),
            const((1, SSD_CONV_DIM)),
            const((1, LANES)),
            const((1, LANES)),
            const((1, SSD_D_INNER)),
            const((1, SSD_D_INNER)),
            const((LANES, SSD_D_INNER)),
        ],
        out_specs=pl.BlockSpec((tr, SSD_D_INNER), lambda b, i: (b * nr + i, 0)),
        out_shape=jax.ShapeDtypeStruct((t, SSD_D_INNER), BF16),
        scratch_shapes=[
            pltpu.VMEM((8 + 16, SSD_CONV_DIM), F32),
            pltpu.VMEM(((SSD_CONV - 1) * tr, tr), BF16),
            pltpu.VMEM((tr, SSD_D_INNER), F32),
            pltpu.VMEM((tr, 2 * SSD_GROUPS * SSD_STATE), BF16),
            pltpu.VMEM((tr, LANES), F32),
            pltpu.VMEM((tr, SSD_D_INNER), F32),
            pltpu.VMEM((tr, SSD_D_INNER), F32),
            pltpu.VMEM((tr, SSD_D_INNER), BF16),
            pltpu.VMEM((tr, SSD_D_INNER), BF16),
            pltpu.VMEM((tr, SSD_D_INNER), F32),
            pltpu.VMEM((SSD_GROUPS, SSD_STATE, SSD_D_INNER // SSD_GROUPS), F32),
        ],
        compiler_params=_cparams(("parallel", "arbitrary")),
        name="ssd",
    )(proj, proj, dt_raw, conv_w, conv_b, dt_bias, a_log, d_skip_b, norm_w, expand)


def _merge_kernel(ya_ref, ys_ref, ga_ref, gs_ref, x_ref, wa_ref, wb_ref, wo_ref, o_ref):
    a = jnp.dot(ya_ref[...], wa_ref[...], preferred_element_type=F32)
    s = jnp.dot(ys_ref[...], wb_ref[...], preferred_element_type=F32)
    merged = _sigmoid(ga_ref[...].astype(F32)) * a + _sigmoid(gs_ref[...].astype(F32)) * s
    o_ref[...] = x_ref[...] + jnp.dot(merged.astype(BF16), wo_ref[...], preferred_element_type=F32)


def _merge(y_attn, y_ssd, proj, x2d, w_a, w_b, w_o, tm=512):
    t = x2d.shape[0]
    tm = min(tm, t)
    full = lambda r, c: pl.BlockSpec((r, c), lambda i: (0, 0))
    return pl.pallas_call(
        _merge_kernel,
        grid=(t // tm,),
        in_specs=[
            pl.BlockSpec((tm, D_MODEL), lambda i: (i, 0)),
            pl.BlockSpec((tm, SSD_D_INNER), lambda i: (i, 0)),
            pl.BlockSpec((tm, D_MODEL), lambda i: (i, COL_GA // D_MODEL)),
            pl.BlockSpec((tm, D_MODEL), lambda i: (i, COL_GS // D_MODEL)),
            pl.BlockSpec((tm, D_MODEL), lambda i: (i, 0)),
            full(D_MODEL, D_MODEL), full(SSD_D_INNER, D_MODEL), full(D_MODEL, D_MODEL),
        ],
        out_specs=pl.BlockSpec((tm, D_MODEL), lambda i: (i, 0)),
        out_shape=jax.ShapeDtypeStruct((t, D_MODEL), F32),
        compiler_params=_cparams(("parallel",)),
        name="merge",
    )(y_attn, y_ssd, proj, proj, x2d, w_a, w_b, w_o)


def _router_kernel(x_ref, nw_ref, rw_ref, rb_ref, h_ref, idx_ref, gate_ref, rank_ref, cnt_ref, carry_ref, *, tt):
    i = pl.program_id(0)

    @pl.when(i == 0)
    def _():
        carry_ref[...] = jnp.zeros_like(carry_ref)

    h = _rms(x_ref[...], nw_ref[...])
    h_ref[...] = h
    logits = jnp.dot(h, rw_ref[...], preferred_element_type=F32, precision=lax.Precision.HIGHEST) + rb_ref[...]
    lane = lax.broadcasted_iota(jnp.int32, (tt, LANES), 1)
    work = jnp.where(lane < N_EXPERTS, logits, -jnp.inf)

    vals, idxs, hots = [], [], []
    for _ in range(TOP_K):
        m = jnp.max(work, axis=-1, keepdims=True)
        idx = jnp.min(jnp.where(work == m, lane, LANES), axis=-1, keepdims=True)
        hot = lane == idx
        vals.append(m)
        idxs.append(idx)
        hots.append(hot)
        work = jnp.where(hot, -jnp.inf, work)

    exps = [jnp.exp(v - vals[0]) for v in vals]
    denom = exps[0] + exps[1] + exps[2] + exps[3]

    hot_sum = sum(jnp.where(hot, 1.0, 0.0) for hot in hots)
    r_i = lax.broadcasted_iota(jnp.int32, (tt, tt), 0)
    c_i = lax.broadcasted_iota(jnp.int32, (tt, tt), 1)
    strict = jnp.where(c_i < r_i, 1.0, 0.0).astype(BF16)
    prefix = jnp.dot(strict, hot_sum.astype(BF16), preferred_element_type=F32) + carry_ref[...]

    idx_out = jnp.zeros((tt, LANES), jnp.int32)
    rank_out = jnp.zeros((tt, LANES), jnp.int32)
    gate_out = jnp.zeros((tt, LANES), F32)
    for k in range(TOP_K):
        rank_k = jnp.sum(jnp.where(hots[k], prefix, 0.0), axis=-1, keepdims=True).astype(jnp.int32)
        sel = lane == k
        idx_out = jnp.where(sel, idxs[k], idx_out)
        rank_out = jnp.where(sel, rank_k, rank_out)
        gate_out = jnp.where(sel, exps[k] / denom, gate_out)
    idx_ref[...] = idx_out
    rank_ref[...] = rank_out
    gate_ref[...] = gate_out

    carry_ref[...] = carry_ref[...] + jnp.sum(hot_sum, axis=0, keepdims=True)
    cnt_ref[...] = carry_ref[...]


def _router(x1, norm_w, router_w_pad, router_b_pad, tt=512):
    t = x1.shape[0]
    tt = min(tt, t)
    kern = functools.partial(_router_kernel, tt=tt)
    row = pl.BlockSpec((tt, LANES), lambda i: (i, 0))
    return pl.pallas_call(
        kern,
        grid=(t // tt,),
        in_specs=[
            pl.BlockSpec((tt, D_MODEL), lambda i: (i, 0)),
            pl.BlockSpec((1, D_MODEL), lambda i: (0, 0)),
            pl.BlockSpec((D_MODEL, LANES), lambda i: (0, 0)),
            pl.BlockSpec((1, LANES), lambda i: (0, 0)),
        ],
        out_specs=[pl.BlockSpec((tt, D_MODEL), lambda i: (i, 0)), row, row, row,
                   pl.BlockSpec((1, LANES), lambda i: (0, 0))],
        out_shape=[
            jax.ShapeDtypeStruct((t, D_MODEL), F32),
            jax.ShapeDtypeStruct((t, LANES), jnp.int32),
            jax.ShapeDtypeStruct((t, LANES), F32),
            jax.ShapeDtypeStruct((t, LANES), jnp.int32),
            jax.ShapeDtypeStruct((1, LANES), F32),
        ],
        scratch_shapes=[pltpu.VMEM((1, LANES), F32)],
        compiler_params=_cparams(("arbitrary",)),
        name="router",
    )(x1, norm_w, router_w_pad, router_b_pad)


def _dispatch_kernel(pos_ref, h_ref, o_ref, sem, *, tt):
    def copy(t, k):
        return pltpu.make_async_copy(h_ref.at[pl.ds(t, 1), :], o_ref.at[pl.ds(pos_ref[t * TOP_K + k], 1), :], sem)

    def issue(t, c):
        for k in range(TOP_K):
            copy(t, k).start()
        return c

    lax.fori_loop(0, tt, issue, 0)

    for _ in range(TOP_K):
        pltpu.make_async_copy(h_ref, o_ref.at[pl.ds(0, tt), :], sem).wait()


def _dispatch(h2, pos_flat, tt=256):
    t = h2.shape[0]
    tt = min(tt, t)
    kern = functools.partial(_dispatch_kernel, tt=tt)
    return pl.pallas_call(
        kern,
        grid=(t // tt,),
        in_specs=[
            pl.BlockSpec((tt * TOP_K,), lambda i: (i,), memory_space=pltpu.SMEM),
            pl.BlockSpec((tt, D_MODEL), lambda i: (i, 0)),
        ],
        out_specs=pl.BlockSpec(memory_space=pl.ANY),
        out_shape=jax.ShapeDtypeStruct((t * TOP_K, D_MODEL), F32),
        scratch_shapes=[pltpu.SemaphoreType.DMA(())],
        compiler_params=_cparams(("arbitrary",)),
        name="dispatch",
    )(pos_flat, h2)


def _experts_kernel(tile_ref, exp_ref, first_ref, efirst_ref, valid_ref, start_ref,
                    x_ref, wu_ref, bu_ref, wd_ref, bd_ref, o_ref, wub_ref, wdb_ref, *, tm):
    w = pl.program_id(0)

    @pl.when(efirst_ref[w] == 1)
    def _():
        for c0 in range(0, 2 * D_EXPERT, 512):
            wub_ref[:, c0:c0 + 512] = wu_ref[0, :, c0:c0 + 512].astype(BF16)
        for c0 in range(0, D_MODEL, 512):
            wdb_ref[:, c0:c0 + 512] = wd_ref[0, :, c0:c0 + 512].astype(BF16)

    @pl.when(valid_ref[w] == 1)
    def _():
        e = exp_ref[w]
        gu = jnp.dot(x_ref[...].astype(BF16), wub_ref[...], preferred_element_type=F32) + bu_ref[0]
        g = jnp.minimum(gu[:, :D_EXPERT], SWIGLU_LIMIT)
        u = jnp.clip(gu[:, D_EXPERT:], -SWIGLU_LIMIT, SWIGLU_LIMIT)
        act = (u + 1.0) * (g * _sigmoid(SWIGLU_ALPHA * g))
        out = jnp.dot(act.astype(BF16), wdb_ref[...], preferred_element_type=F32) + bd_ref[0]
        rows = tile_ref[w] * tm + lax.broadcasted_iota(jnp.int32, (tm, 1), 0)
        mine = (rows >= start_ref[e]) & (rows < start_ref[e + 1])

        @pl.when(first_ref[w] == 1)
        def _():
            o_ref[...] = jnp.where(mine, out, 0.0)

        @pl.when(first_ref[w] == 0)
        def _():
            o_ref[...] = jnp.where(mine, out, o_ref[...])


def _experts(x_sorted, item_tile, item_exp, item_first, item_efirst, item_valid, starts, w_up, b_up, w_down, b_down, tm):
    n = x_sorted.shape[0]
    n_items = item_tile.shape[0]
    kern = functools.partial(_experts_kernel, tm=tm)
    grid_spec = pltpu.PrefetchScalarGridSpec(
        num_scalar_prefetch=6,
        grid=(n_items,),
        in_specs=[
            pl.BlockSpec((tm, D_MODEL), lambda w, tl, ex, *_: (tl[w], 0)),
            pl.BlockSpec((1, D_MODEL, 2 * D_EXPERT), lambda w, tl, ex, *_: (ex[w], 0, 0)),
            pl.BlockSpec((1, 1, 2 * D_EXPERT), lambda w, tl, ex, *_: (ex[w], 0, 0)),
            pl.BlockSpec((1, D_EXPERT, D_MODEL), lambda w, tl, ex, *_: (ex[w], 0, 0)),
            pl.BlockSpec((1, 1, D_MODEL), lambda w, tl, ex, *_: (ex[w], 0, 0)),
        ],
        out_specs=pl.BlockSpec((tm, D_MODEL), lambda w, tl, ex, *_: (tl[w], 0)),
        scratch_shapes=[pltpu.VMEM((D_MODEL, 2 * D_EXPERT), BF16), pltpu.VMEM((D_EXPERT, D_MODEL), BF16)],
    )
    return pl.pallas_call(
        kern,
        grid_spec=grid_spec,
        out_shape=jax.ShapeDtypeStruct((n, D_MODEL), F32),
        compiler_params=_cparams(("arbitrary",)),
        name="experts",
    )(item_tile, item_exp, item_first, item_efirst, item_valid, starts, x_sorted, w_up, b_up, w_down, b_down)


def _combine_kernel(pos_ref, posn_ref, gate_ref, x_ref, p_ref, pnw_ref, wpg_ref, wpp_ref, fnw_ref, ys_ref, o_ref,
                    rows_ref, sem, *, tt):
    i = pl.program_id(0)
    n = pl.num_programs(0)
    slot = i % 2

    def gather(idx_ref, dst_slot):
        def issue(t, c):
            for k in range(TOP_K):
                pltpu.make_async_copy(ys_ref.at[pl.ds(idx_ref[t * TOP_K + k], 1), :],
                                      rows_ref.at[dst_slot, pl.ds(k * tt + t, 1), :], sem.at[dst_slot]).start()
            return c

        lax.fori_loop(0, tt, issue, 0)

    @pl.when(i == 0)
    def _():
        gather(pos_ref, 0)

    @pl.when(i + 1 < n)
    def _():
        gather(posn_ref, 1 - slot)

    pltpu.make_async_copy(ys_ref.at[pl.ds(0, TOP_K * tt), :], rows_ref.at[slot], sem.at[slot]).wait()

    gate = gate_ref[...]
    x = x_ref[...]
    for k in range(TOP_K):
        x = x + gate[:, k:k + 1] * rows_ref[slot, pl.ds(k * tt, tt), :]

    hg = _rms(x, pnw_ref[...]).astype(BF16)
    pg = _sigmoid(jnp.dot(hg, wpg_ref[...], preferred_element_type=F32))
    x = x + pg * jnp.dot(p_ref[...].astype(BF16), wpp_ref[...], preferred_element_type=F32)
    o_ref[...] = _rms(x, fnw_ref[...])


def _combine(pos_flat, gate, x1, p2d, ple_nw, w_pg, w_pp, final_nw, y_sorted, tt=256):
    t = x1.shape[0]
    tt = min(tt, t)
    kern = functools.partial(_combine_kernel, tt=tt)
    vec = pl.BlockSpec((1, D_MODEL), lambda i: (0, 0))
    nt = t // tt
    return pl.pallas_call(
        kern,
        grid=(nt,),
        in_specs=[
            pl.BlockSpec((tt * TOP_K,), lambda i: (i,), memory_space=pltpu.SMEM),
            pl.BlockSpec((tt * TOP_K,), lambda i: (jnp.minimum(i + 1, nt - 1),), memory_space=pltpu.SMEM),
            pl.BlockSpec((tt, LANES), lambda i: (i, 0)),
            pl.BlockSpec((tt, D_MODEL), lambda i: (i, 0)),
            pl.BlockSpec((tt, PLE_DIM), lambda i: (i, 0)),
            vec,
            pl.BlockSpec((D_MODEL, D_MODEL), lambda i: (0, 0)),
            pl.BlockSpec((PLE_DIM, D_MODEL), lambda i: (0, 0)),
            vec,
            pl.BlockSpec(memory_space=pl.ANY),
        ],
        out_specs=pl.BlockSpec((tt, D_MODEL), lambda i: (i, 0)),
        out_shape=jax.ShapeDtypeStruct((t, D_MODEL), F32),
        scratch_shapes=[pltpu.VMEM((2, TOP_K * tt, D_MODEL), F32), pltpu.SemaphoreType.DMA((2,))],
        compiler_params=_cparams(("arbitrary",)),
        name="combine",
    )(pos_flat, pos_flat, gate, x1, p2d, ple_nw, w_pg, w_pp, final_nw, y_sorted)


def _routing_plan(counts, idx, rank, tm, n_rows):
    counts = counts.astype(jnp.int32)
    ends = jnp.cumsum(counts)
    starts = ends - counts
    pos = (starts[idx] + rank).reshape(-1)
    n_tiles = n_rows // tm
    n_items = n_tiles + N_EXPERTS - 1
    first_tile = starts // tm
    last_tile = jnp.maximum(ends - 1, 0) // tm
    items_e = jnp.where(counts > 0, last_tile - first_tile + 1, 0)
    item_end = jnp.cumsum(items_e)
    item_start = item_end - items_e
    total = item_end[-1]
    w = jnp.arange(n_items, dtype=jnp.int32)
    e_w = jnp.minimum(jnp.sum(w[:, None] >= item_end[None, :], axis=1).astype(jnp.int32), N_EXPERTS - 1)
    tile_w = first_tile[e_w] + (w - item_start[e_w])
    valid = w < total
    last = jnp.maximum(total - 1, 0)
    e_w = jnp.where(valid, e_w, e_w[last])
    tile_w = jnp.where(valid, tile_w, tile_w[last])
    prev_tile = jnp.concatenate([jnp.full((1,), -1, jnp.int32), tile_w[:-1]])
    first = (tile_w != prev_tile) & valid
    prev_e = jnp.concatenate([jnp.full((1,), -1, jnp.int32), e_w[:-1]])
    efirst = (e_w != prev_e) & valid
    starts_ext = jnp.concatenate([starts, ends[-1:]]).astype(jnp.int32)
    return (pos.astype(jnp.int32), tile_w.astype(jnp.int32), e_w, first.astype(jnp.int32), efirst.astype(jnp.int32),
            valid.astype(jnp.int32), starts_ext)


def kernel(x, p, mix_norm_w, w_in, lambda_q1, lambda_k1, lambda_q2, lambda_k2, da_head_norm_w, w_attn_branch, conv_w, conv_b, dt_bias, a_log, d_skip, ssd_norm_w, w_ssd_branch, w_out, moe_norm_w, router_w, router_b, w_up, b_up, w_down, b_down, ple_norm_w, w_ple_gate, w_ple_proj, final_norm_w):
    batch, seq, _ = x.shape
    t = batch * seq
    depth = w_in.shape[0]
    assert depth == 1, "the final RMSNorm is fused into the layer's last kernel"
    x2d = x.reshape(t, D_MODEL)

    log2e = math.log2(math.e)
    slopes = jnp.asarray([log2e * 2.0 ** (-8.0 * (h + 1) / DA_HEADS) for h in range(DA_HEADS)], F32)
    q_scale = log2e * DA_QK_DIM ** -0.5
    head_ids = jnp.arange(SSD_D_INNER, dtype=jnp.int32) // SSD_HEAD_DIM
    expand = (jnp.arange(LANES, dtype=jnp.int32)[:, None] == head_ids[None, :]).astype(BF16)

    def pad_lanes(v, fill=0.0):
        return jnp.pad(v.reshape(1, -1), ((0, 0), (0, LANES - v.shape[-1])), constant_values=fill)

    for i in range(depth):
        wi = w_in[i]
        w_main = jnp.concatenate([
            wi[:, _R_Z:_R_Z + SSD_D_INNER], wi[:, _R_Q:_R_Q + 1024] * q_scale, wi[:, _R_K:_R_K + 1024],
            wi[:, _R_GA:_R_GA + 1024], wi[:, _R_GS:_R_GS + 1024], wi[:, _R_XBC:_R_XBC + SSD_CONV_DIM],
            wi[:, _R_V:_R_V + 1024]], axis=1).astype(BF16)
        w_dt = jnp.pad(wi[:, _R_DT:_R_DT + SSD_HEADS], ((0, 0), (0, LANES - SSD_HEADS))).astype(BF16)

        proj, dt_raw, v_t = _in_proj(x2d, mix_norm_w[i].reshape(1, -1), w_main, w_dt)

        lambda_init = 0.8 - 0.6 * math.exp(-0.3 * i)
        lam = (jnp.exp(jnp.sum(lambda_q1[i] * lambda_k1[i])) - jnp.exp(jnp.sum(lambda_q2[i] * lambda_k2[i]))
               + lambda_init).reshape(1).astype(F32)
        y_attn = _diff_attn(proj, v_t, slopes, lam, da_head_norm_w[i].reshape(1, -1), batch, seq, lambda_init)

        y_ssd = _ssd(proj, dt_raw, conv_w[i], conv_b[i].reshape(1, -1), pad_lanes(dt_bias[i]), pad_lanes(a_log[i]),
                     jnp.repeat(d_skip[i], SSD_HEAD_DIM).reshape(1, -1), ssd_norm_w[i].reshape(1, -1), expand,
                     batch, seq)

        x1 = _merge(y_attn, y_ssd, proj, x2d, w_attn_branch[i].astype(BF16), w_ssd_branch[i].astype(BF16),
                    w_out[i].astype(BF16))

        rw = jnp.pad(router_w[i], ((0, 0), (0, LANES - N_EXPERTS)))
        h2, idx, gate, rank, counts = _router(x1, moe_norm_w[i].reshape(1, -1), rw, pad_lanes(router_b[i]))

        tm = 512
        n_rows = t * TOP_K
        pos, item_tile, item_exp, item_first, item_efirst, item_valid, starts = _routing_plan(
            counts[0, :N_EXPERTS], idx[:, :TOP_K], rank[:, :TOP_K], tm, n_rows)

        x_sorted = _dispatch(h2, pos)
        y_sorted = _experts(x_sorted, item_tile, item_exp, item_first, item_efirst, item_valid, starts,
                            w_up[i], b_up[i].reshape(N_EXPERTS, 1, -1),
                            w_down[i], b_down[i].reshape(N_EXPERTS, 1, -1), tm)

        x2d = _combine(pos, gate, x1, p[i].reshape(t, PLE_DIM), ple_norm_w[i].reshape(1, -1),
                       w_ple_gate[i].astype(BF16), w_ple_proj[i].astype(BF16), final_norm_w.reshape(1, -1), y_sorted)
    return x2d.reshape(batch, seq, D_MODEL)
```

```python
import functools
import math

import jax
import jax.numpy as jnp
from jax import lax
from jax.experimental import pallas as pl
from jax.experimental.pallas import tpu as pltpu

F32 = jnp.float32
BF16 = jnp.bfloat16

D_MODEL = 1024
CHUNK = 64
PLE_DIM = 256
RMS_EPS = 1e-6

DA_HEADS = 8
DA_QK_DIM = 64
DA_V_DIM = 128

SSD_D_INNER = 2048
SSD_HEAD_DIM = 64
SSD_HEADS = 32
SSD_GROUPS = 4
SSD_STATE = 128
SSD_CONV = 4
SSD_CONV_DIM = 3072

N_EXPERTS = 32
TOP_K = 4
D_EXPERT = 1024
SWIGLU_LIMIT = 7.0
SWIGLU_ALPHA = 1.702

LANES = 128

COL_Z = 0
COL_Q = 2048
COL_K = 3072
COL_GA = 4096
COL_GS = 5120
COL_XBC = 6144
PROJ_W = 9216

ATTN_TILE = 256
ATTN_Q_TILE = 512
SUM_ROWS = 16

_R_Q, _R_K, _R_V, _R_Z, _R_XBC, _R_DT, _R_GA, _R_GS = 0, 1024, 2048, 3072, 5120, 8192, 8224, 9248

VMEM_LIMIT = 56 * 1024 * 1024


def _cparams(sem):
    return pltpu.CompilerParams(dimension_semantics=sem, vmem_limit_bytes=VMEM_LIMIT)


def _rms(x, w):
    return x * lax.rsqrt(jnp.mean(x * x, axis=-1, keepdims=True) + RMS_EPS) * w


def _sigmoid(x):
    return 1.0 / (1.0 + jnp.exp2(x * (-math.log2(math.e))))


def _inproj_kernel(x_ref, nw_ref, w_ref, wdt_ref, o_ref, dt_ref, vt_ref, h_ref, *, tm, n_proj):
    j = pl.program_id(1)

    @pl.when(j == 0)
    def _():
        hb = _rms(x_ref[...], nw_ref[...]).astype(BF16)
        h_ref[...] = hb
        dt_ref[...] = jnp.dot(hb, wdt_ref[...], preferred_element_type=F32)

    acc = jnp.dot(h_ref[...], w_ref[...], preferred_element_type=F32)

    @pl.when(j < n_proj)
    def _():
        o_ref[...] = acc.astype(BF16)

    @pl.when(j == n_proj)
    def _():
        vt = acc.T
        for c in range(tm // ATTN_TILE):
            vt_ref[c] = vt[:, c * ATTN_TILE:(c + 1) * ATTN_TILE].astype(BF16)


def _in_proj(x2d, norm_w, w_main, w_dt, tm=2048):
    t = x2d.shape[0]
    tm = min(tm, t)
    tn = DA_HEADS * DA_V_DIM
    n_proj = PROJ_W // tn
    return pl.pallas_call(
        functools.partial(_inproj_kernel, tm=tm, n_proj=n_proj),
        grid=(t // tm, n_proj + 1),
        in_specs=[
            pl.BlockSpec((tm, D_MODEL), lambda i, j: (i, 0)),
            pl.BlockSpec((1, D_MODEL), lambda i, j: (0, 0)),
            pl.BlockSpec((D_MODEL, tn), lambda i, j: (0, j)),
            pl.BlockSpec((D_MODEL, LANES), lambda i, j: (0, 0)),
        ],
        out_specs=[
            pl.BlockSpec((tm, tn), lambda i, j: (i, jnp.minimum(j, n_proj - 1))),
            pl.BlockSpec((tm, LANES), lambda i, j: (i, 0)),
            pl.BlockSpec((tm // ATTN_TILE, tn, ATTN_TILE), lambda i, j: (i, 0, 0)),
        ],
        out_shape=[
            jax.ShapeDtypeStruct((t, PROJ_W), BF16),
            jax.ShapeDtypeStruct((t, LANES), F32),
            jax.ShapeDtypeStruct((t // ATTN_TILE, tn, ATTN_TILE), BF16),
        ],
        scratch_shapes=[pltpu.VMEM((tm, D_MODEL), BF16)],
        compiler_params=_cparams(("parallel", "arbitrary")),
        name="in_proj",
    )(x2d, norm_w, w_main, w_dt)


def _attn_kernel(slopes_ref, lam_ref, q_ref, k_ref, vt_ref, nw_ref, o_ref,
                 bias_ref, qc_ref, s_ref, p_ref, al_ref, m_ref, a_ref, *, tq, tk, out_scale):
    h = pl.program_id(1)
    i = pl.program_id(2)
    slope = slopes_ref[h]
    lam = lam_ref[0]
    r = tq // tk
    n_tiles = r * (i + 1)

    @pl.when(i == 0)
    def _():
        kk = lax.broadcasted_iota(jnp.int32, (tk, tq), 0)
        qq = lax.broadcasted_iota(jnp.int32, (tk, tq), 1)
        bias_ref[0] = -slope * (qq - kk).astype(F32)
        for d in range(r):
            kd = kk + d * tk
            bias_ref[1 + d] = jnp.where((kd // CHUNK) <= (qq // CHUNK), -slope * jnp.abs(qq - kd).astype(F32), -jnp.inf)

    qt = q_ref[...].astype(F32).T.astype(BF16)
    dim = lax.broadcasted_iota(jnp.int32, qt.shape, 0)
    zero = jnp.zeros_like(qt)
    qc_ref[...] = jnp.concatenate([jnp.where(dim < DA_QK_DIM, qt, zero), jnp.where(dim >= DA_QK_DIM, qt, zero)], axis=1)

    m_ref[...] = jnp.full(m_ref.shape, -jnp.inf, F32)
    a_ref[...] = jnp.zeros(a_ref.shape, F32)
    al_ref[...] = jnp.zeros(al_ref.shape, F32)
    p_ref[...] = jnp.zeros(p_ref.shape, BF16)

    def scores(j):
        k = k_ref[pl.ds(pl.multiple_of(j * tk, tk), tk), :]
        return jnp.dot(k, qc_ref[...], preferred_element_type=F32)

    def softmax_step(sel, cj):
        for g in range(2 * tq // LANES):
            cols = slice(g * LANES, (g + 1) * LANES)
            bcols = slice((g * LANES) % tq, (g * LANES) % tq + LANES)
            s = s_ref[:, cols] + bias_ref[sel, :, bcols]
            m_prev = m_ref[:, cols]
            m_cur = jnp.maximum(m_prev, jnp.max(s, axis=0, keepdims=True) + cj)
            alpha = jnp.exp2(m_prev - m_cur)
            p = jnp.exp2(s + (cj - m_cur))
            m_ref[:, cols] = m_cur
            al_ref[:, cols] = alpha
            p_ref[:, cols] = p.astype(BF16)

    def pv_step(j):
        vt = jnp.concatenate([vt_ref[j], jnp.ones((SUM_ROWS, tk), BF16)], axis=0)
        for g in range(2 * tq // 256):
            cols = slice(g * 256, (g + 1) * 256)
            a_ref[:, cols] = al_ref[:, cols] * a_ref[:, cols] + jnp.dot(vt, p_ref[:, cols], preferred_element_type=F32)

    s_ref[...] = scores(0)

    def body(j, carry):
        s_next = scores(jnp.minimum(j + 1, n_tiles - 1))
        pv_step(jnp.maximum(j - 1, 0))
        past = j < r * i
        cj = jnp.where(past, -slope * (i * tq - j * tk).astype(F32), 0.0)
        softmax_step(jnp.maximum(j - r * i + 1, 0), cj)
        s_ref[...] = s_next
        return carry

    lax.fori_loop(0, n_tiles, body, 0)
    pv_step(n_tiles - 1)

    on = a_ref[0:DA_V_DIM, :] / a_ref[DA_V_DIM:DA_V_DIM + 1, :]
    ot = on[:, :tq] - lam * on[:, tq:]
    ot = ot * lax.rsqrt(jnp.mean(ot * ot, axis=0, keepdims=True) + RMS_EPS)
    o_ref[...] = (ot.T * nw_ref[...] * out_scale).astype(BF16)


def _diff_attn(proj, v_t, slopes, lam, head_norm_w, batch, seq, lambda_init):
    tk = ATTN_TILE
    tq = min(ATTN_Q_TILE, seq)
    nq = seq // tq
    nk = seq // tk
    t = batch * seq
    qb, kb = COL_Q // LANES, COL_K // LANES
    assert tq % tk == 0
    kern = functools.partial(_attn_kernel, tq=tq, tk=tk, out_scale=1.0 - lambda_init)
    smem = pl.BlockSpec(memory_space=pltpu.SMEM)
    return pl.pallas_call(
        kern,
        grid=(batch, DA_HEADS, nq),
        in_specs=[
            smem, smem,
            pl.BlockSpec((tq, LANES), lambda b, h, i: (b * nq + i, qb + h)),
            pl.BlockSpec((seq, LANES), lambda b, h, i: (b, kb + h)),
            pl.BlockSpec((nk, DA_V_DIM, tk), lambda b, h, i: (b, h, 0)),
            pl.BlockSpec((1, LANES), lambda b, h, i: (0, 0)),
        ],
        out_specs=pl.BlockSpec((tq, LANES), lambda b, h, i: (b * nq + i, h)),
        out_shape=jax.ShapeDtypeStruct((t, DA_HEADS * DA_V_DIM), BF16),
        scratch_shapes=[
            pltpu.VMEM((1 + tq // tk, tk, tq), F32),
            pltpu.VMEM((LANES, 2 * tq), BF16),
            pltpu.VMEM((tk, 2 * tq), F32),
            pltpu.VMEM((tk, 2 * tq), BF16),
            pltpu.VMEM((1, 2 * tq), F32),
            pltpu.VMEM((1, 2 * tq), F32),
            pltpu.VMEM((DA_V_DIM + SUM_ROWS, 2 * tq), F32),
        ],
        compiler_params=_cparams(("parallel", "parallel", "arbitrary")),
        name="diff_attn",
    )(slopes, lam, proj, proj, v_t, head_norm_w)


def _split2(x):
    hi = x.astype(BF16)
    lo = (x - hi.astype(F32)).astype(BF16)
    return hi, lo


def _split3(x):
    hi = x.astype(BF16)
    r = x - hi.astype(F32)
    mid = r.astype(BF16)
    lo = (r - mid.astype(F32)).astype(BF16)
    return hi, mid, lo


def _ssd_kernel(xbc_ref, z_ref, dt_ref, cw_ref, cb_ref, dtb_ref, alog_ref, dsk_ref, nw_ref, e_ref,
                o_ref, ubuf, shift_ref, xs_scr, bc_scr, acum_scr, acb, dfs_scr, xdt_scr, xw_scr, y_scr, state, *, tr):
    i = pl.program_id(1)
    nch = tr // CHUNK
    gw = SSD_D_INNER // SSD_GROUPS
    pad = 8

    head = 16

    @pl.when(i == 0)
    def _():
        ubuf[0:pad, :] = jnp.zeros((pad, SSD_CONV_DIM), F32)
        state[...] = jnp.zeros_like(state)
        r_s = lax.broadcasted_iota(jnp.int32, (tr, tr), 0)
        c_s = lax.broadcasted_iota(jnp.int32, (tr, tr), 1)
        for d in range(1, SSD_CONV):
            shift_ref[(d - 1) * tr:d * tr, :] = jnp.where(c_s == r_s - d, 1.0, 0.0).astype(BF16)

    ubuf[pad:pad + head, :] = xbc_ref[0:head, :].astype(F32)
    cs = 512
    for c0 in range(0, SSD_CONV_DIM, cs):
        ub = xbc_ref[:, c0:c0 + cs]
        sh = jnp.dot(shift_ref[...], ub, preferred_element_type=F32)
        acc = cb_ref[:, c0:c0 + cs] + cw_ref[SSD_CONV - 1:SSD_CONV, c0:c0 + cs] * ub.astype(F32)
        for d in range(1, SSD_CONV):
            acc = acc + cw_ref[SSD_CONV - 1 - d:SSD_CONV - d, c0:c0 + cs] * sh[(d - 1) * tr:d * tr, :]
        u = acc * _sigmoid(acc)
        acc_h = jnp.broadcast_to(cb_ref[:, c0:c0 + cs], (head, cs))
        for k in range(SSD_CONV):
            off = pad - (SSD_CONV - 1) + k
            acc_h = acc_h + cw_ref[k:k + 1, c0:c0 + cs] * ubuf[off:off + head, c0:c0 + cs]
        u_h = acc_h * _sigmoid(acc_h)
        if c0 < SSD_D_INNER:
            xs_scr[:, c0:c0 + cs] = u
            xs_scr[0:head, c0:c0 + cs] = u_h
        else:
            bc_scr[:, c0 - SSD_D_INNER:c0 - SSD_D_INNER + cs] = u.astype(BF16)
            bc_scr[0:head, c0 - SSD_D_INNER:c0 - SSD_D_INNER + cs] = u_h.astype(BF16)
    ubuf[0:pad, :] = xbc_ref[tr - 2 * pad:tr, :].astype(F32)[pad:, :]

    xdt_in = dt_ref[...] + dtb_ref[...]
    dtv = jnp.maximum(xdt_in, 0.0) + jnp.log1p(jnp.exp(-jnp.abs(xdt_in)))
    da = dtv * (-jnp.exp(alog_ref[...]))
    r_i = lax.broadcasted_iota(jnp.int32, (tr, tr), 0)
    c_i = lax.broadcasted_iota(jnp.int32, (tr, tr), 1)
    tril = jnp.where((c_i <= r_i) & ((c_i // CHUNK) == (r_i // CHUNK)), 1.0, 0.0).astype(BF16)
    acum = sum(jnp.dot(tril, piece, preferred_element_type=F32) for piece in _split3(da))
    acum_scr[...] = acum

    a_hi, a_lo = _split2(acum)
    d_hi, d_lo = _split2(dtv)
    for c0 in range(0, SSD_D_INNER, cs):
        e = e_ref[:, c0:c0 + cs]
        ab = jnp.dot(a_hi, e, preferred_element_type=F32) + jnp.dot(a_lo, e, preferred_element_type=F32)
        db = jnp.dot(d_hi, e, preferred_element_type=F32) + jnp.dot(d_lo, e, preferred_element_type=F32)
        acb[:, c0:c0 + cs] = ab
        dfs_scr[:, c0:c0 + cs] = jnp.exp(ab)
        xdt = xs_scr[:, c0:c0 + cs] * db
        xdt_scr[:, c0:c0 + cs] = xdt.astype(BF16)
        for c in range(nch):
            r0 = c * CHUNK
            last = ab[r0 + CHUNK - 1:r0 + CHUNK, :]
            xw_scr[r0:r0 + CHUNK, c0:c0 + cs] = (xdt[r0:r0 + CHUNK, :] * jnp.exp(last - ab[r0:r0 + CHUNK, :])).astype(BF16)

    lane2 = lax.broadcasted_iota(jnp.int32, (CHUNK, LANES), 1)
    row2 = lax.broadcasted_iota(jnp.int32, (CHUNK, LANES), 0)
    causal2 = row2 >= (lane2 % SSD_HEAD_DIM)
    lo_half = lane2 < SSD_HEAD_DIM
    zpad = jnp.zeros((LANES - CHUNK, LANES), F32)

    def chunk_body(c, carry):
        r0 = pl.multiple_of(c * CHUNK, CHUNK)
        rows = pl.ds(r0, CHUNK)
        a_c = acum_scr[rows, :]
        a_t = jnp.concatenate([a_c, zpad], axis=0).T
        a_t_r = pltpu.roll(a_t, SSD_HEAD_DIM, 1)
        last_row = acb[pl.ds(r0 + CHUNK - 1, 1), :]
        for g in range(SSD_GROUPS):
            bg = bc_scr[rows, g * SSD_STATE:(g + 1) * SSD_STATE]
            cg = bc_scr[rows, SSD_GROUPS * SSD_STATE + g * SSD_STATE:SSD_GROUPS * SSD_STATE + (g + 1) * SSD_STATE]
            cbt = lax.dot_general(cg, jnp.concatenate([bg, bg], axis=0), (((1,), (1,)), ((), ())),
                                  preferred_element_type=F32)
            g0 = g * gw
            st = state[g]
            yoff = jnp.dot(cg, st.astype(BF16), preferred_element_type=F32) * dfs_scr[rows, g0:g0 + gw]
            for jj in range(gw // LANES):
                pidx = g * (gw // LANES) + jj
                l0 = pidx * LANES
                arow = a_t[2 * pidx:2 * pidx + 1, :] + a_t_r[2 * pidx + 1:2 * pidx + 2, :]
                seg = acb[rows, l0:l0 + LANES] - arow
                decay = jnp.exp(jnp.where(causal2, seg, -jnp.inf))
                mp = (cbt * decay).astype(BF16)
                xd = xdt_scr[rows, l0:l0 + LANES]
                zb = jnp.zeros_like(xd)
                bd = jnp.concatenate([jnp.where(lo_half, xd, zb), jnp.where(lo_half, zb, xd)], axis=0)
                yd = jnp.dot(mp, bd, preferred_element_type=F32)
                y_scr[rows, l0:l0 + LANES] = yd + yoff[:, jj * LANES:(jj + 1) * LANES]
            cd = jnp.exp(last_row[:, g0:g0 + gw])
            upd = lax.dot_general(bg, xw_scr[rows, g0:g0 + gw], (((0,), (0,)), ((), ())),
                                  preferred_element_type=F32)
            state[g] = st * cd + upd
        return carry

    lax.fori_loop(0, nch, chunk_body, 0)

    for g in range(SSD_GROUPS):
        g0 = g * gw
        y = y_scr[:, g0:g0 + gw] + dsk_ref[:, g0:g0 + gw] * xs_scr[:, g0:g0 + gw]
        zz = z_ref[:, g0:g0 + gw].astype(F32)
        y = y * (zz * _sigmoid(zz))
        o_ref[:, g0:g0 + gw] = _rms(y, nw_ref[:, g0:g0 + gw]).astype(BF16)


def _ssd(proj, dt_raw, conv_w, conv_b, dt_bias, a_log, d_skip_b, norm_w, expand, batch, seq, tr=256):
    t = batch * seq
    nr = seq // tr
    kern = functools.partial(_ssd_kernel, tr=tr)
    const = lambda shape: pl.BlockSpec(shape, lambda b, i: (0, 0))
    return pl.pallas_call(
        kern,
        grid=(batch, nr),
        in_specs=[
            pl.BlockSpec((tr, SSD_CONV_DIM), lambda b, i: (b * nr + i, COL_XBC // SSD_CONV_DIM)),
            pl.BlockSpec((tr, SSD_D_INNER), lambda b, i: (b * nr + i, COL_Z // SSD_D_INNER)),
            pl.BlockSpec((tr, LANES), lambda b, i: (b * nr + i, 0)),
            const((SSD_CONV, SSD_CONV_DIM)),
            const((1, SSD_CONV_DIM)),
            const((1, LANES)),
            const((1, LANES)),
            const((1, SSD_D_INNER)),
            const((1, SSD_D_INNER)),
            const((LANES, SSD_D_INNER)),
        ],
        out_specs=pl.BlockSpec((tr, SSD_D_INNER), lambda b, i: (b * nr + i, 0)),
        out_shape=jax.ShapeDtypeStruct((t, SSD_D_INNER), BF16),
        scratch_shapes=[
            pltpu.VMEM((8 + 16, SSD_CONV_DIM), F32),
            pltpu.VMEM(((SSD_CONV - 1) * tr, tr), BF16),
            pltpu.VMEM((tr, SSD_D_INNER), F32),
            pltpu.VMEM((tr, 2 * SSD_GROUPS * SSD_STATE), BF16),
            pltpu.VMEM((tr, LANES), F32),
            pltpu.VMEM((tr, SSD_D_INNER), F32),
            pltpu.VMEM((tr, SSD_D_INNER), F32),
            pltpu.VMEM((tr, SSD_D_INNER), BF16),
            pltpu.VMEM((tr, SSD_D_INNER), BF16),
            pltpu.VMEM((tr, SSD_D_INNER), F32),
            pltpu.VMEM((SSD_GROUPS, SSD_STATE, SSD_D_INNER // SSD_GROUPS), F32),
        ],
        compiler_params=_cparams(("parallel", "arbitrary")),
        name="ssd",
    )(proj, proj, dt_raw, conv_w, conv_b, dt_bias, a_log, d_skip_b, norm_w, expand)


def _merge_kernel(ya_ref, ys_ref, ga_ref, gs_ref, x_ref, wa_ref, wb_ref, wo_ref, o_ref):
    a = jnp.dot(ya_ref[...], wa_ref[...], preferred_element_type=F32)
    s = jnp.dot(ys_ref[...], wb_ref[...], preferred_element_type=F32)
    merged = _sigmoid(ga_ref[...].astype(F32)) * a + _sigmoid(gs_ref[...].astype(F32)) * s
    o_ref[...] = x_ref[...] + jnp.dot(merged.astype(BF16), wo_ref[...], preferred_element_type=F32)


def _merge(y_attn, y_ssd, proj, x2d, w_a, w_b, w_o, tm=512):
    t = x2d.shape[0]
    tm = min(tm, t)
    full = lambda r, c: pl.BlockSpec((r, c), lambda i: (0, 0))
    return pl.pallas_call(
        _merge_kernel,
        grid=(t // tm,),
        in_specs=[
            pl.BlockSpec((tm, D_MODEL), lambda i: (i, 0)),
            pl.BlockSpec((tm, SSD_D_INNER), lambda i: (i, 0)),
            pl.BlockSpec((tm, D_MODEL), lambda i: (i, COL_GA // D_MODEL)),
            pl.BlockSpec((tm, D_MODEL), lambda i: (i, COL_GS // D_MODEL)),
            pl.BlockSpec((tm, D_MODEL), lambda i: (i, 0)),
            full(D_MODEL, D_MODEL), full(SSD_D_INNER, D_MODEL), full(D_MODEL, D_MODEL),
        ],
        out_specs=pl.BlockSpec((tm, D_MODEL), lambda i: (i, 0)),
        out_shape=jax.ShapeDtypeStruct((t, D_MODEL), F32),
        compiler_params=_cparams(("parallel",)),
        name="merge",
    )(y_attn, y_ssd, proj, proj, x2d, w_a, w_b, w_o)


def _router_kernel(x_ref, nw_ref, rw_ref, rb_ref, h_ref, idx_ref, gate_ref, rank_ref, cnt_ref, carry_ref, *, tt):
    i = pl.program_id(0)

    @pl.when(i == 0)
    def _():
        carry_ref[...] = jnp.zeros_like(carry_ref)

    h = _rms(x_ref[...], nw_ref[...])
    h_ref[...] = h
    logits = jnp.dot(h, rw_ref[...], preferred_element_type=F32, precision=lax.Precision.HIGHEST) + rb_ref[...]
    lane = lax.broadcasted_iota(jnp.int32, (tt, LANES), 1)
    work = jnp.where(lane < N_EXPERTS, logits, -jnp.inf)

    vals, idxs, hots = [], [], []
    for _ in range(TOP_K):
        m = jnp.max(work, axis=-1, keepdims=True)
        idx = jnp.min(jnp.where(work == m, lane, LANES), axis=-1, keepdims=True)
        hot = lane == idx
        vals.append(m)
        idxs.append(idx)
        hots.append(hot)
        work = jnp.where(hot, -jnp.inf, work)

    exps = [jnp.exp(v - vals[0]) for v in vals]
    denom = exps[0] + exps[1] + exps[2] + exps[3]

    hot_sum = sum(jnp.where(hot, 1.0, 0.0) for hot in hots)
    r_i = lax.broadcasted_iota(jnp.int32, (tt, tt), 0)
    c_i = lax.broadcasted_iota(jnp.int32, (tt, tt), 1)
    strict = jnp.where(c_i < r_i, 1.0, 0.0).astype(BF16)
    prefix = jnp.dot(strict, hot_sum.astype(BF16), preferred_element_type=F32) + carry_ref[...]

    idx_out = jnp.zeros((tt, LANES), jnp.int32)
    rank_out = jnp.zeros((tt, LANES), jnp.int32)
    gate_out = jnp.zeros((tt, LANES), F32)
    for k in range(TOP_K):
        rank_k = jnp.sum(jnp.where(hots[k], prefix, 0.0), axis=-1, keepdims=True).astype(jnp.int32)
        sel = lane == k
        idx_out = jnp.where(sel, idxs[k], idx_out)
        rank_out = jnp.where(sel, rank_k, rank_out)
        gate_out = jnp.where(sel, exps[k] / denom, gate_out)
    idx_ref[...] = idx_out
    rank_ref[...] = rank_out
    gate_ref[...] = gate_out

    carry_ref[...] = carry_ref[...] + jnp.sum(hot_sum, axis=0, keepdims=True)
    cnt_ref[...] = carry_ref[...]


def _router(x1, norm_w, router_w_pad, router_b_pad, tt=512):
    t = x1.shape[0]
    tt = min(tt, t)
    kern = functools.partial(_router_kernel, tt=tt)
    row = pl.BlockSpec((tt, LANES), lambda i: (i, 0))
    return pl.pallas_call(
        kern,
        grid=(t // tt,),
        in_specs=[
            pl.BlockSpec((tt, D_MODEL), lambda i: (i, 0)),
            pl.BlockSpec((1, D_MODEL), lambda i: (0, 0)),
            pl.BlockSpec((D_MODEL, LANES), lambda i: (0, 0)),
            pl.BlockSpec((1, LANES), lambda i: (0, 0)),
        ],
        out_specs=[pl.BlockSpec((tt, D_MODEL), lambda i: (i, 0)), row, row, row,
                   pl.BlockSpec((1, LANES), lambda i: (0, 0))],
        out_shape=[
            jax.ShapeDtypeStruct((t, D_MODEL), F32),
            jax.ShapeDtypeStruct((t, LANES), jnp.int32),
            jax.ShapeDtypeStruct((t, LANES), F32),
            jax.ShapeDtypeStruct((t, LANES), jnp.int32),
            jax.ShapeDtypeStruct((1, LANES), F32),
        ],
        scratch_shapes=[pltpu.VMEM((1, LANES), F32)],
        compiler_params=_cparams(("arbitrary",)),
        name="router",
    )(x1, norm_w, router_w_pad, router_b_pad)


def _dispatch_kernel(pos_ref, h_ref, o_ref, sem, *, tt):
    def copy(t, k):
        return pltpu.make_async_copy(h_ref.at[pl.ds(t, 1), :], o_ref.at[pl.ds(pos_ref[t * TOP_K + k], 1), :], sem)

    def issue(t, c):
        for k in range(TOP_K):
            copy(t, k).start()
        return c

    lax.fori_loop(0, tt, issue, 0)

    for _ in range(TOP_K):
        pltpu.make_async_copy(h_ref, o_ref.at[pl.ds(0, tt), :], sem).wait()


def _dispatch(h2, pos_flat, tt=256):
    t = h2.shape[0]
    tt = min(tt, t)
    kern = functools.partial(_dispatch_kernel, tt=tt)
    return pl.pallas_call(
        kern,
        grid=(t // tt,),
        in_specs=[
            pl.BlockSpec((tt * TOP_K,), lambda i: (i,), memory_space=pltpu.SMEM),
            pl.BlockSpec((tt, D_MODEL), lambda i: (i, 0)),
        ],
        out_specs=pl.BlockSpec(memory_space=pl.ANY),
        out_shape=jax.ShapeDtypeStruct((t * TOP_K, D_MODEL), F32),
        scratch_shapes=[pltpu.SemaphoreType.DMA(())],
        compiler_params=_cparams(("arbitrary",)),
        name="dispatch",
    )(pos_flat, h2)


def _experts_kernel(tile_ref, exp_ref, first_ref, efirst_ref, valid_ref, start_ref,
                    x_ref, wu_ref, bu_ref, wd_ref, bd_ref, o_ref, wub_ref, wdb_ref, *, tm):
    w = pl.program_id(0)

    @pl.when(efirst_ref[w] == 1)
    def _():
        for c0 in range(0, 2 * D_EXPERT, 512):
            wub_ref[:, c0:c0 + 512] = wu_ref[0, :, c0:c0 + 512].astype(BF16)
        for c0 in range(0, D_MODEL, 512):
            wdb_ref[:, c0:c0 + 512] = wd_ref[0, :, c0:c0 + 512].astype(BF16)

    @pl.when(valid_ref[w] == 1)
    def _():
        e = exp_ref[w]
        gu = jnp.dot(x_ref[...].astype(BF16), wub_ref[...], preferred_element_type=F32) + bu_ref[0]
        g = jnp.minimum(gu[:, :D_EXPERT], SWIGLU_LIMIT)
        u = jnp.clip(gu[:, D_EXPERT:], -SWIGLU_LIMIT, SWIGLU_LIMIT)
        act = (u + 1.0) * (g * _sigmoid(SWIGLU_ALPHA * g))
        out = jnp.dot(act.astype(BF16), wdb_ref[...], preferred_element_type=F32) + bd_ref[0]
        rows = tile_ref[w] * tm + lax.broadcasted_iota(jnp.int32, (tm, 1), 0)
        mine = (rows >= start_ref[e]) & (rows < start_ref[e + 1])

        @pl.when(first_ref[w] == 1)
        def _():
            o_ref[...] = jnp.where(mine, out, 0.0)

        @pl.when(first_ref[w] == 0)
        def _():
            o_ref[...] = jnp.where(mine, out, o_ref[...])


def _experts(x_sorted, item_tile, item_exp, item_first, item_efirst, item_valid, starts, w_up, b_up, w_down, b_down, tm):
    n = x_sorted.shape[0]
    n_items = item_tile.shape[0]
    kern = functools.partial(_experts_kernel, tm=tm)
    grid_spec = pltpu.PrefetchScalarGridSpec(
        num_scalar_prefetch=6,
        grid=(n_items,),
        in_specs=[
            pl.BlockSpec((tm, D_MODEL), lambda w, tl, ex, *_: (tl[w], 0)),
            pl.BlockSpec((1, D_MODEL, 2 * D_EXPERT), lambda w, tl, ex, *_: (ex[w], 0, 0)),
            pl.BlockSpec((1, 1, 2 * D_EXPERT), lambda w, tl, ex, *_: (ex[w], 0, 0)),
            pl.BlockSpec((1, D_EXPERT, D_MODEL), lambda w, tl, ex, *_: (ex[w], 0, 0)),
            pl.BlockSpec((1, 1, D_MODEL), lambda w, tl, ex, *_: (ex[w], 0, 0)),
        ],
        out_specs=pl.BlockSpec((tm, D_MODEL), lambda w, tl, ex, *_: (tl[w], 0)),
        scratch_shapes=[pltpu.VMEM((D_MODEL, 2 * D_EXPERT), BF16), pltpu.VMEM((D_EXPERT, D_MODEL), BF16)],
    )
    return pl.pallas_call(
        kern,
        grid_spec=grid_spec,
        out_shape=jax.ShapeDtypeStruct((n, D_MODEL), F32),
        compiler_params=_cparams(("arbitrary",)),
        name="experts",
    )(item_tile, item_exp, item_first, item_efirst, item_valid, starts, x_sorted, w_up, b_up, w_down, b_down)


def _combine_kernel(pos_ref, posn_ref, gate_ref, x_ref, p_ref, pnw_ref, wpg_ref, wpp_ref, fnw_ref, ys_ref, o_ref,
                    rows_ref, sem, *, tt):
    i = pl.program_id(0)
    n = pl.num_programs(0)
    slot = i % 2

    def gather(idx_ref, dst_slot):
        def issue(t, c):
            for k in range(TOP_K):
                pltpu.make_async_copy(ys_ref.at[pl.ds(idx_ref[t * TOP_K + k], 1), :],
                                      rows_ref.at[dst_slot, pl.ds(k * tt + t, 1), :], sem.at[dst_slot]).start()
            return c

        lax.fori_loop(0, tt, issue, 0)

    @pl.when(i == 0)
    def _():
        gather(pos_ref, 0)

    @pl.when(i + 1 < n)
    def _():
        gather(posn_ref, 1 - slot)

    pltpu.make_async_copy(ys_ref.at[pl.ds(0, TOP_K * tt), :], rows_ref.at[slot], sem.at[slot]).wait()

    gate = gate_ref[...]
    x = x_ref[...]
    for k in range(TOP_K):
        x = x + gate[:, k:k + 1] * rows_ref[slot, pl.ds(k * tt, tt), :]

    hg = _rms(x, pnw_ref[...]).astype(BF16)
    pg = _sigmoid(jnp.dot(hg, wpg_ref[...], preferred_element_type=F32))
    x = x + pg * jnp.dot(p_ref[...].astype(BF16), wpp_ref[...], preferred_element_type=F32)
    o_ref[...] = _rms(x, fnw_ref[...])


def _combine(pos_flat, gate, x1, p2d, ple_nw, w_pg, w_pp, final_nw, y_sorted, tt=256):
    t = x1.shape[0]
    tt = min(tt, t)
    kern = functools.partial(_combine_kernel, tt=tt)
    vec = pl.BlockSpec((1, D_MODEL), lambda i: (0, 0))
    nt = t // tt
    return pl.pallas_call(
        kern,
        grid=(nt,),
        in_specs=[
            pl.BlockSpec((tt * TOP_K,), lambda i: (i,), memory_space=pltpu.SMEM),
            pl.BlockSpec((tt * TOP_K,), lambda i: (jnp.minimum(i + 1, nt - 1),), memory_space=pltpu.SMEM),
            pl.BlockSpec((tt, LANES), lambda i: (i, 0)),
            pl.BlockSpec((tt, D_MODEL), lambda i: (i, 0)),
            pl.BlockSpec((tt, PLE_DIM), lambda i: (i, 0)),
            vec,
            pl.BlockSpec((D_MODEL, D_MODEL), lambda i: (0, 0)),
            pl.BlockSpec((PLE_DIM, D_MODEL), lambda i: (0, 0)),
            vec,
            pl.BlockSpec(memory_space=pl.ANY),
        ],
        out_specs=pl.BlockSpec((tt, D_MODEL), lambda i: (i, 0)),
        out_shape=jax.ShapeDtypeStruct((t, D_MODEL), F32),
        scratch_shapes=[pltpu.VMEM((2, TOP_K * tt, D_MODEL), F32), pltpu.SemaphoreType.DMA((2,))],
        compiler_params=_cparams(("arbitrary",)),
        name="combine",
    )(pos_flat, pos_flat, gate, x1, p2d, ple_nw, w_pg, w_pp, final_nw, y_sorted)


def _routing_plan(counts, idx, rank, tm, n_rows):
    counts = counts.astype(jnp.int32)
    ends = jnp.cumsum(counts)
    starts = ends - counts
    pos = (starts[idx] + rank).reshape(-1)
    n_tiles = n_rows // tm
    n_items = n_tiles + N_EXPERTS - 1
    first_tile = starts // tm
    last_tile = jnp.maximum(ends - 1, 0) // tm
    items_e = jnp.where(counts > 0, last_tile - first_tile + 1, 0)
    item_end = jnp.cumsum(items_e)
    item_start = item_end - items_e
    total = item_end[-1]
    w = jnp.arange(n_items, dtype=jnp.int32)
    e_w = jnp.minimum(jnp.sum(w[:, None] >= item_end[None, :], axis=1).astype(jnp.int32), N_EXPERTS - 1)
    tile_w = first_tile[e_w] + (w - item_start[e_w])
    valid = w < total
    last = jnp.maximum(total - 1, 0)
    e_w = jnp.where(valid, e_w, e_w[last])
    tile_w = jnp.where(valid, tile_w, tile_w[last])
    prev_tile = jnp.concatenate([jnp.full((1,), -1, jnp.int32), tile_w[:-1]])
    first = (tile_w != prev_tile) & valid
    prev_e = jnp.concatenate([jnp.full((1,), -1, jnp.int32), e_w[:-1]])
    efirst = (e_w != prev_e) & valid
    starts_ext = jnp.concatenate([starts, ends[-1:]]).astype(jnp.int32)
    return (pos.astype(jnp.int32), tile_w.astype(jnp.int32), e_w, first.astype(jnp.int32), efirst.astype(jnp.int32),
            valid.astype(jnp.int32), starts_ext)


def kernel(x, p, mix_norm_w, w_in, lambda_q1, lambda_k1, lambda_q2, lambda_k2, da_head_norm_w, w_attn_branch, conv_w, conv_b, dt_bias, a_log, d_skip, ssd_norm_w, w_ssd_branch, w_out, moe_norm_w, router_w, router_b, w_up, b_up, w_down, b_down, ple_norm_w, w_ple_gate, w_ple_proj, final_norm_w):
    batch, seq, _ = x.shape
    t = batch * seq
    depth = w_in.shape[0]
    assert depth == 1, "the final RMSNorm is fused into the layer's last kernel"
    x2d = x.reshape(t, D_MODEL)

    log2e = math.log2(math.e)
    slopes = jnp.asarray([log2e * 2.0 ** (-8.0 * (h + 1) / DA_HEADS) for h in range(DA_HEADS)], F32)
    q_scale = log2e * DA_QK_DIM ** -0.5
    head_ids = jnp.arange(SSD_D_INNER, dtype=jnp.int32) // SSD_HEAD_DIM
    expand = (jnp.arange(LANES, dtype=jnp.int32)[:, None] == head_ids[None, :]).astype(BF16)

    def pad_lanes(v, fill=0.0):
        return jnp.pad(v.reshape(1, -1), ((0, 0), (0, LANES - v.shape[-1])), constant_values=fill)

    for i in range(depth):
        wi = w_in[i]
        w_main = jnp.concatenate([
            wi[:, _R_Z:_R_Z + SSD_D_INNER], wi[:, _R_Q:_R_Q + 1024] * q_scale, wi[:, _R_K:_R_K + 1024],
            wi[:, _R_GA:_R_GA + 1024], wi[:, _R_GS:_R_GS + 1024], wi[:, _R_XBC:_R_XBC + SSD_CONV_DIM],
            wi[:, _R_V:_R_V + 1024]], axis=1).astype(BF16)
        w_dt = jnp.pad(wi[:, _R_DT:_R_DT + SSD_HEADS], ((0, 0), (0, LANES - SSD_HEADS))).astype(BF16)

        proj, dt_raw, v_t = _in_proj(x2d, mix_norm_w[i].reshape(1, -1), w_main, w_dt)

        lambda_init = 0.8 - 0.6 * math.exp(-0.3 * i)
        lam = (jnp.exp(jnp.sum(lambda_q1[i] * lambda_k1[i])) - jnp.exp(jnp.sum(lambda_q2[i] * lambda_k2[i]))
               + lambda_init).reshape(1).astype(F32)
        y_attn = _diff_attn(proj, v_t, slopes, lam, da_head_norm_w[i].reshape(1, -1), batch, seq, lambda_init)

        y_ssd = _ssd(proj, dt_raw, conv_w[i], conv_b[i].reshape(1, -1), pad_lanes(dt_bias[i]), pad_lanes(a_log[i]),
                     jnp.repeat(d_skip[i], SSD_HEAD_DIM).reshape(1, -1), ssd_norm_w[i].reshape(1, -1), expand,
                     batch, seq)

        x1 = _merge(y_attn, y_ssd, proj, x2d, w_attn_branch[i].astype(BF16), w_ssd_branch[i].astype(BF16),
                    w_out[i].astype(BF16))

        rw = jnp.pad(router_w[i], ((0, 0), (0, LANES - N_EXPERTS)))
        h2, idx, gate, rank, counts = _router(x1, moe_norm_w[i].reshape(1, -1), rw, pad_lanes(router_b[i]))

        tm = 512
        n_rows = t * TOP_K
        pos, item_tile, item_exp, item_first, item_efirst, item_valid, starts = _routing_plan(
            counts[0, :N_EXPERTS], idx[:, :TOP_K], rank[:, :TOP_K], tm, n_rows)

        x_sorted = _dispatch(h2, pos)
        y_sorted = _experts(x_sorted, item_tile, item_exp, item_first, item_efirst, item_valid, starts,
                            w_up[i], b_up[i].reshape(N_EXPERTS, 1, -1),
                            w_down[i], b_down[i].reshape(N_EXPERTS, 1, -1), tm)

        x2d = _combine(pos, gate, x1, p[i].reshape(t, PLE_DIM), ple_norm_w[i].reshape(1, -1),
                       w_ple_gate[i].astype(BF16), w_ple_proj[i].astype(BF16), final_norm_w.reshape(1, -1), y_sorted)
    return x2d.reshape(batch, seq, D_MODEL)
```

```python
import functools
import math

import jax
import jax.numpy as jnp
from jax import lax
from jax.experimental import pallas as pl
from jax.experimental.pallas import tpu as pltpu

F32 = jnp.float32
BF16 = jnp.bfloat16

D_MODEL = 1024
CHUNK = 64
PLE_DIM = 256
RMS_EPS = 1e-6

DA_HEADS = 8
DA_QK_DIM = 64
DA_V_DIM = 128

SSD_D_INNER = 2048
SSD_HEAD_DIM = 64
SSD_HEADS = 32
SSD_GROUPS = 4
SSD_STATE = 128
SSD_CONV = 4
SSD_CONV_DIM = 3072

N_EXPERTS = 32
TOP_K = 4
D_EXPERT = 1024
SWIGLU_LIMIT = 7.0
SWIGLU_ALPHA = 1.702

LANES = 128

COL_Z = 0
COL_Q = 2048
COL_K = 3072
COL_GA = 4096
COL_GS = 5120
COL_XBC = 6144
PROJ_W = 9216

ATTN_TILE = 256
ATTN_Q_TILE = 512
SUM_ROWS = 16

_R_Q, _R_K, _R_V, _R_Z, _R_XBC, _R_DT, _R_GA, _R_GS = 0, 1024, 2048, 3072, 5120, 8192, 8224, 9248

VMEM_LIMIT = 56 * 1024 * 1024


def _cparams(sem):
    return pltpu.CompilerParams(dimension_semantics=sem, vmem_limit_bytes=VMEM_LIMIT)


def _rms(x, w):
    return x * lax.rsqrt(jnp.mean(x * x, axis=-1, keepdims=True) + RMS_EPS) * w


ROW_CHUNKS = D_MODEL // LANES


def _rows_to_tiles(ref, val, n):
    for c in range(ROW_CHUNKS):
        ref[pl.ds(c, n, stride=ROW_CHUNKS), :] = val[:, c * LANES:(c + 1) * LANES]


def _tiles_to_rows(ref, n, base=0):
    return jnp.concatenate([ref[pl.ds(base + c, n, stride=ROW_CHUNKS), :] for c in range(ROW_CHUNKS)], axis=1)


def _sigmoid(x):
    return 1.0 / (1.0 + jnp.exp2(x * (-math.log2(math.e))))


def _inproj_kernel(x_ref, nw_ref, w_ref, wdt_ref, o_ref, dt_ref, vt_ref, h_ref, *, tm, n_proj):
    j = pl.program_id(1)

    @pl.when(j == 0)
    def _():
        hb = _rms(x_ref[...], nw_ref[...]).astype(BF16)
        h_ref[...] = hb
        dt_ref[...] = jnp.dot(hb, wdt_ref[...], preferred_element_type=F32)

    acc = jnp.dot(h_ref[...], w_ref[...], preferred_element_type=F32)

    @pl.when(j < n_proj)
    def _():
        o_ref[...] = acc.astype(BF16)

    @pl.when(j == n_proj)
    def _():
        vt = acc.T
        for c in range(tm // ATTN_TILE):
            vt_ref[c] = vt[:, c * ATTN_TILE:(c + 1) * ATTN_TILE].astype(BF16)


def _in_proj(x2d, norm_w, w_main, w_dt, tm=2048):
    t = x2d.shape[0]
    tm = min(tm, t)
    tn = DA_HEADS * DA_V_DIM
    n_proj = PROJ_W // tn
    return pl.pallas_call(
        functools.partial(_inproj_kernel, tm=tm, n_proj=n_proj),
        grid=(t // tm, n_proj + 1),
        in_specs=[
            pl.BlockSpec((tm, D_MODEL), lambda i, j: (i, 0)),
            pl.BlockSpec((1, D_MODEL), lambda i, j: (0, 0)),
            pl.BlockSpec((D_MODEL, tn), lambda i, j: (0, j)),
            pl.BlockSpec((D_MODEL, LANES), lambda i, j: (0, 0)),
        ],
        out_specs=[
            pl.BlockSpec((tm, tn), lambda i, j: (i, jnp.minimum(j, n_proj - 1))),
            pl.BlockSpec((tm, LANES), lambda i, j: (i, 0)),
            pl.BlockSpec((tm // ATTN_TILE, tn, ATTN_TILE), lambda i, j: (i, 0, 0)),
        ],
        out_shape=[
            jax.ShapeDtypeStruct((t, PROJ_W), BF16),
            jax.ShapeDtypeStruct((t, LANES), F32),
            jax.ShapeDtypeStruct((t // ATTN_TILE, tn, ATTN_TILE), BF16),
        ],
        scratch_shapes=[pltpu.VMEM((tm, D_MODEL), BF16)],
        compiler_params=_cparams(("parallel", "arbitrary")),
        name="in_proj",
    )(x2d, norm_w, w_main, w_dt)


def _attn_kernel(slopes_ref, lam_ref, q_ref, k_ref, vt_ref, nw_ref, o_ref,
                 bias_ref, qc_ref, s_ref, p_ref, al_ref, m_ref, a_ref, *, tq, tk, out_scale):
    h = pl.program_id(1)
    i = pl.program_id(2)
    slope = slopes_ref[h]
    lam = lam_ref[0]
    r = tq // tk
    n_tiles = r * (i + 1)

    @pl.when(i == 0)
    def _():
        kk = lax.broadcasted_iota(jnp.int32, (tk, tq), 0)
        qq = lax.broadcasted_iota(jnp.int32, (tk, tq), 1)
        bias_ref[0] = -slope * (qq - kk).astype(F32)
        for d in range(r):
            kd = kk + d * tk
            bias_ref[1 + d] = jnp.where((kd // CHUNK) <= (qq // CHUNK), -slope * jnp.abs(qq - kd).astype(F32), -jnp.inf)

    qt = q_ref[...].astype(F32).T.astype(BF16)
    dim = lax.broadcasted_iota(jnp.int32, qt.shape, 0)
    zero = jnp.zeros_like(qt)
    qc_ref[...] = jnp.concatenate([jnp.where(dim < DA_QK_DIM, qt, zero), jnp.where(dim >= DA_QK_DIM, qt, zero)], axis=1)

    m_ref[...] = jnp.full(m_ref.shape, -jnp.inf, F32)
    a_ref[...] = jnp.zeros(a_ref.shape, F32)
    al_ref[...] = jnp.zeros(al_ref.shape, F32)
    p_ref[...] = jnp.zeros(p_ref.shape, BF16)

    def scores(j):
        k = k_ref[pl.ds(pl.multiple_of(j * tk, tk), tk), :]
        return jnp.dot(k, qc_ref[...], preferred_element_type=F32)

    def softmax_step(sel, cj):
        for g in range(2 * tq // LANES):
            cols = slice(g * LANES, (g + 1) * LANES)
            bcols = slice((g * LANES) % tq, (g * LANES) % tq + LANES)
            s = s_ref[:, cols] + bias_ref[sel, :, bcols]
            m_prev = m_ref[:, cols]
            m_cur = jnp.maximum(m_prev, jnp.max(s, axis=0, keepdims=True) + cj)
            alpha = jnp.exp2(m_prev - m_cur)
            p = jnp.exp2(s + (cj - m_cur))
            m_ref[:, cols] = m_cur
            al_ref[:, cols] = alpha
            p_ref[:, cols] = p.astype(BF16)

    def pv_step(j):
        vt = jnp.concatenate([vt_ref[j], jnp.ones((SUM_ROWS, tk), BF16)], axis=0)
        for g in range(2 * tq // 256):
            cols = slice(g * 256, (g + 1) * 256)
            a_ref[:, cols] = al_ref[:, cols] * a_ref[:, cols] + jnp.dot(vt, p_ref[:, cols], preferred_element_type=F32)

    s_ref[...] = scores(0)

    def body(j, carry):
        s_next = scores(jnp.minimum(j + 1, n_tiles - 1))
        pv_step(jnp.maximum(j - 1, 0))
        past = j < r * i
        cj = jnp.where(past, -slope * (i * tq - j * tk).astype(F32), 0.0)
        softmax_step(jnp.maximum(j - r * i + 1, 0), cj)
        s_ref[...] = s_next
        return carry

    lax.fori_loop(0, n_tiles, body, 0)
    pv_step(n_tiles - 1)

    on = a_ref[0:DA_V_DIM, :] / a_ref[DA_V_DIM:DA_V_DIM + 1, :]
    ot = on[:, :tq] - lam * on[:, tq:]
    ot = ot * lax.rsqrt(jnp.mean(ot * ot, axis=0, keepdims=True) + RMS_EPS)
    o_ref[...] = (ot.T * nw_ref[...] * out_scale).astype(BF16)


def _diff_attn(proj, v_t, slopes, lam, head_norm_w, batch, seq, lambda_init):
    tk = ATTN_TILE
    tq = min(ATTN_Q_TILE, seq)
    nq = seq // tq
    nk = seq // tk
    t = batch * seq
    qb, kb = COL_Q // LANES, COL_K // LANES
    assert tq % tk == 0
    kern = functools.partial(_attn_kernel, tq=tq, tk=tk, out_scale=1.0 - lambda_init)
    smem = pl.BlockSpec(memory_space=pltpu.SMEM)
    return pl.pallas_call(
        kern,
        grid=(batch, DA_HEADS, nq),
        in_specs=[
            smem, smem,
            pl.BlockSpec((tq, LANES), lambda b, h, i: (b * nq + i, qb + h)),
            pl.BlockSpec((seq, LANES), lambda b, h, i: (b, kb + h)),
            pl.BlockSpec((nk, DA_V_DIM, tk), lambda b, h, i: (b, h, 0)),
            pl.BlockSpec((1, LANES), lambda b, h, i: (0, 0)),
        ],
        out_specs=pl.BlockSpec((tq, LANES), lambda b, h, i: (b * nq + i, h)),
        out_shape=jax.ShapeDtypeStruct((t, DA_HEADS * DA_V_DIM), BF16),
        scratch_shapes=[
            pltpu.VMEM((1 + tq // tk, tk, tq), F32),
            pltpu.VMEM((LANES, 2 * tq), BF16),
            pltpu.VMEM((tk, 2 * tq), F32),
            pltpu.VMEM((tk, 2 * tq), BF16),
            pltpu.VMEM((1, 2 * tq), F32),
            pltpu.VMEM((1, 2 * tq), F32),
            pltpu.VMEM((DA_V_DIM + SUM_ROWS, 2 * tq), F32),
        ],
        compiler_params=_cparams(("parallel", "parallel", "arbitrary")),
        name="diff_attn",
    )(slopes, lam, proj, proj, v_t, head_norm_w)


def _split2(x):
    hi = x.astype(BF16)
    lo = (x - hi.astype(F32)).astype(BF16)
    return hi, lo


def _split3(x):
    hi = x.astype(BF16)
    r = x - hi.astype(F32)
    mid = r.astype(BF16)
    lo = (r - mid.astype(F32)).astype(BF16)
    return hi, mid, lo


def _ssd_kernel(xbc_ref, z_ref, dt_ref, cw_ref, cb_ref, dtb_ref, alog_ref, dsk_ref, nw_ref, e_ref,
                o_ref, ubuf, shift_ref, xs_scr, bc_scr, acum_scr, acb, dfs_scr, xdt_scr, xw_scr, y_scr, state, *, tr):
    i = pl.program_id(1)
    nch = tr // CHUNK
    gw = SSD_D_INNER // SSD_GROUPS
    pad = 8

    head = 16

    @pl.when(i == 0)
    def _():
        ubuf[0:pad, :] = jnp.zeros((pad, SSD_CONV_DIM), F32)
        state[...] = jnp.zeros_like(state)
        r_s = lax.broadcasted_iota(jnp.int32, (tr, tr), 0)
        c_s = lax.broadcasted_iota(jnp.int32, (tr, tr), 1)
        for d in range(1, SSD_CONV):
            shift_ref[(d - 1) * tr:d * tr, :] = jnp.where(c_s == r_s - d, 1.0, 0.0).astype(BF16)

    ubuf[pad:pad + head, :] = xbc_ref[0:head, :].astype(F32)
    cs = 512
    for c0 in range(0, SSD_CONV_DIM, cs):
        ub = xbc_ref[:, c0:c0 + cs]
        sh = jnp.dot(shift_ref[...], ub, preferred_element_type=F32)
        acc = cb_ref[:, c0:c0 + cs] + cw_ref[SSD_CONV - 1:SSD_CONV, c0:c0 + cs] * ub.astype(F32)
        for d in range(1, SSD_CONV):
            acc = acc + cw_ref[SSD_CONV - 1 - d:SSD_CONV - d, c0:c0 + cs] * sh[(d - 1) * tr:d * tr, :]
        u = acc * _sigmoid(acc)
        acc_h = jnp.broadcast_to(cb_ref[:, c0:c0 + cs], (head, cs))
        for k in range(SSD_CONV):
            off = pad - (SSD_CONV - 1) + k
            acc_h = acc_h + cw_ref[k:k + 1, c0:c0 + cs] * ubuf[off:off + head, c0:c0 + cs]
        u_h = acc_h * _sigmoid(acc_h)
        if c0 < SSD_D_INNER:
            xs_scr[:, c0:c0 + cs] = u
            xs_scr[0:head, c0:c0 + cs] = u_h
        else:
            bc_scr[:, c0 - SSD_D_INNER:c0 - SSD_D_INNER + cs] = u.astype(BF16)
            bc_scr[0:head, c0 - SSD_D_INNER:c0 - SSD_D_INNER + cs] = u_h.astype(BF16)
    ubuf[0:pad, :] = xbc_ref[tr - 2 * pad:tr, :].astype(F32)[pad:, :]

    xdt_in = dt_ref[...] + dtb_ref[...]
    dtv = jnp.maximum(xdt_in, 0.0) + jnp.log1p(jnp.exp(-jnp.abs(xdt_in)))
    da = dtv * (-jnp.exp(alog_ref[...]))
    r_i = lax.broadcasted_iota(jnp.int32, (tr, tr), 0)
    c_i = lax.broadcasted_iota(jnp.int32, (tr, tr), 1)
    tril = jnp.where((c_i <= r_i) & ((c_i // CHUNK) == (r_i // CHUNK)), 1.0, 0.0).astype(BF16)
    acum = sum(jnp.dot(tril, piece, preferred_element_type=F32) for piece in _split3(da))
    acum_scr[...] = acum

    a_hi, a_lo = _split2(acum)
    d_hi, d_lo = _split2(dtv)
    for c0 in range(0, SSD_D_INNER, cs):
        e = e_ref[:, c0:c0 + cs]
        ab = jnp.dot(a_hi, e, preferred_element_type=F32) + jnp.dot(a_lo, e, preferred_element_type=F32)
        db = jnp.dot(d_hi, e, preferred_element_type=F32) + jnp.dot(d_lo, e, preferred_element_type=F32)
        acb[:, c0:c0 + cs] = ab
        dfs_scr[:, c0:c0 + cs] = jnp.exp(ab)
        xdt = xs_scr[:, c0:c0 + cs] * db
        xdt_scr[:, c0:c0 + cs] = xdt.astype(BF16)
        for c in range(nch):
            r0 = c * CHUNK
            last = ab[r0 + CHUNK - 1:r0 + CHUNK, :]
            xw_scr[r0:r0 + CHUNK, c0:c0 + cs] = (xdt[r0:r0 + CHUNK, :] * jnp.exp(last - ab[r0:r0 + CHUNK, :])).astype(BF16)

    lane2 = lax.broadcasted_iota(jnp.int32, (CHUNK, LANES), 1)
    row2 = lax.broadcasted_iota(jnp.int32, (CHUNK, LANES), 0)
    causal2 = row2 >= (lane2 % SSD_HEAD_DIM)
    lo_half = lane2 < SSD_HEAD_DIM
    zpad = jnp.zeros((LANES - CHUNK, LANES), F32)

    def chunk_body(c, carry):
        r0 = pl.multiple_of(c * CHUNK, CHUNK)
        rows = pl.ds(r0, CHUNK)
        a_c = acum_scr[rows, :]
        a_t = jnp.concatenate([a_c, zpad], axis=0).T
        a_t_r = pltpu.roll(a_t, SSD_HEAD_DIM, 1)
        last_row = acb[pl.ds(r0 + CHUNK - 1, 1), :]
        for g in range(SSD_GROUPS):
            bg = bc_scr[rows, g * SSD_STATE:(g + 1) * SSD_STATE]
            cg = bc_scr[rows, SSD_GROUPS * SSD_STATE + g * SSD_STATE:SSD_GROUPS * SSD_STATE + (g + 1) * SSD_STATE]
            cbt = lax.dot_general(cg, jnp.concatenate([bg, bg], axis=0), (((1,), (1,)), ((), ())),
                                  preferred_element_type=F32)
            g0 = g * gw
            st = state[g]
            yoff = jnp.dot(cg, st.astype(BF16), preferred_element_type=F32) * dfs_scr[rows, g0:g0 + gw]
            for jj in range(gw // LANES):
                pidx = g * (gw // LANES) + jj
                l0 = pidx * LANES
                arow = a_t[2 * pidx:2 * pidx + 1, :] + a_t_r[2 * pidx + 1:2 * pidx + 2, :]
                seg = acb[rows, l0:l0 + LANES] - arow
                decay = jnp.exp(jnp.where(causal2, seg, -jnp.inf))
                mp = (cbt * decay).astype(BF16)
                xd = xdt_scr[rows, l0:l0 + LANES]
                zb = jnp.zeros_like(xd)
                bd = jnp.concatenate([jnp.where(lo_half, xd, zb), jnp.where(lo_half, zb, xd)], axis=0)
                yd = jnp.dot(mp, bd, preferred_element_type=F32)
                y_scr[rows, l0:l0 + LANES] = yd + yoff[:, jj * LANES:(jj + 1) * LANES]
            cd = jnp.exp(last_row[:, g0:g0 + gw])
            upd = lax.dot_general(bg, xw_scr[rows, g0:g0 + gw], (((0,), (0,)), ((), ())),
                                  preferred_element_type=F32)
            state[g] = st * cd + upd
        return carry

    lax.fori_loop(0, nch, chunk_body, 0, unroll=2)

    for g in range(SSD_GROUPS):
        g0 = g * gw
        y = y_scr[:, g0:g0 + gw] + dsk_ref[:, g0:g0 + gw] * xs_scr[:, g0:g0 + gw]
        zz = z_ref[:, g0:g0 + gw].astype(F32)
        y = y * (zz * _sigmoid(zz))
        o_ref[:, g0:g0 + gw] = _rms(y, nw_ref[:, g0:g0 + gw]).astype(BF16)


def _ssd(proj, dt_raw, conv_w, conv_b, dt_bias, a_log, d_skip_b, norm_w, expand, batch, seq, tr=256):
    t = batch * seq
    nr = seq // tr
    kern = functools.partial(_ssd_kernel, tr=tr)
    const = lambda shape: pl.BlockSpec(shape, lambda b, i: (0, 0))
    return pl.pallas_call(
        kern,
        grid=(batch, nr),
        in_specs=[
            pl.BlockSpec((tr, SSD_CONV_DIM), lambda b, i: (b * nr + i, COL_XBC // SSD_CONV_DIM)),
            pl.BlockSpec((tr, SSD_D_INNER), lambda b, i: (b * nr + i, COL_Z // SSD_D_INNER)),
            pl.BlockSpec((tr, LANES), lambda b, i: (b * nr + i, 0)),
            const((SSD_CONV, SSD_CONV_DIM)),
            const((1, SSD_CONV_DIM)),
            const((1, LANES)),
            const((1, LANES)),
            const((1, SSD_D_INNER)),
            const((1, SSD_D_INNER)),
            const((LANES, SSD_D_INNER)),
        ],
        out_specs=pl.BlockSpec((tr, SSD_D_INNER), lambda b, i: (b * nr + i, 0)),
        out_shape=jax.ShapeDtypeStruct((t, SSD_D_INNER), BF16),
        scratch_shapes=[
            pltpu.VMEM((8 + 16, SSD_CONV_DIM), F32),
            pltpu.VMEM(((SSD_CONV - 1) * tr, tr), BF16),
            pltpu.VMEM((tr, SSD_D_INNER), F32),
            pltpu.VMEM((tr, 2 * SSD_GROUPS * SSD_STATE), BF16),
            pltpu.VMEM((tr, LANES), F32),
            pltpu.VMEM((tr, SSD_D_INNER), F32),
            pltpu.VMEM((tr, SSD_D_INNER), F32),
            pltpu.VMEM((tr, SSD_D_INNER), BF16),
            pltpu.VMEM((tr, SSD_D_INNER), BF16),
            pltpu.VMEM((tr, SSD_D_INNER), F32),
            pltpu.VMEM((SSD_GROUPS, SSD_STATE, SSD_D_INNER // SSD_GROUPS), F32),
        ],
        compiler_params=_cparams(("parallel", "arbitrary")),
        name="ssd",
    )(proj, proj, dt_raw, conv_w, conv_b, dt_bias, a_log, d_skip_b, norm_w, expand)


def _merge_kernel(ya_ref, ys_ref, ga_ref, gs_ref, x_ref, wa_ref, wb_ref, wo_ref, o_ref):
    a = jnp.dot(ya_ref[...], wa_ref[...], preferred_element_type=F32)
    s = jnp.dot(ys_ref[...], wb_ref[...], preferred_element_type=F32)
    merged = _sigmoid(ga_ref[...].astype(F32)) * a + _sigmoid(gs_ref[...].astype(F32)) * s
    o_ref[...] = x_ref[...] + jnp.dot(merged.astype(BF16), wo_ref[...], preferred_element_type=F32)


def _merge(y_attn, y_ssd, proj, x2d, w_a, w_b, w_o, tm=512):
    t = x2d.shape[0]
    tm = min(tm, t)
    full = lambda r, c: pl.BlockSpec((r, c), lambda i: (0, 0))
    return pl.pallas_call(
        _merge_kernel,
        grid=(t // tm,),
        in_specs=[
            pl.BlockSpec((tm, D_MODEL), lambda i: (i, 0)),
            pl.BlockSpec((tm, SSD_D_INNER), lambda i: (i, 0)),
            pl.BlockSpec((tm, D_MODEL), lambda i: (i, COL_GA // D_MODEL)),
            pl.BlockSpec((tm, D_MODEL), lambda i: (i, COL_GS // D_MODEL)),
            pl.BlockSpec((tm, D_MODEL), lambda i: (i, 0)),
            full(D_MODEL, D_MODEL), full(SSD_D_INNER, D_MODEL), full(D_MODEL, D_MODEL),
        ],
        out_specs=pl.BlockSpec((tm, D_MODEL), lambda i: (i, 0)),
        out_shape=jax.ShapeDtypeStruct((t, D_MODEL), F32),
        compiler_params=_cparams(("parallel",)),
        name="merge",
    )(y_attn, y_ssd, proj, proj, x2d, w_a, w_b, w_o)


def _router_kernel(x_ref, nw_ref, rw_ref, rb_ref, h_ref, idx_ref, gate_ref, rank_ref, cnt_ref, carry_ref, *, tt):
    i = pl.program_id(0)

    @pl.when(i == 0)
    def _():
        carry_ref[...] = jnp.zeros_like(carry_ref)

    h = _rms(x_ref[...], nw_ref[...])
    h_ref[...] = h
    logits = jnp.dot(h, rw_ref[...], preferred_element_type=F32, precision=lax.Precision.HIGHEST) + rb_ref[...]
    lane = lax.broadcasted_iota(jnp.int32, (tt, LANES), 1)
    work = jnp.where(lane < N_EXPERTS, logits, -jnp.inf)

    vals, idxs, hots = [], [], []
    for _ in range(TOP_K):
        m = jnp.max(work, axis=-1, keepdims=True)
        idx = jnp.min(jnp.where(work == m, lane, LANES), axis=-1, keepdims=True)
        hot = lane == idx
        vals.append(m)
        idxs.append(idx)
        hots.append(hot)
        work = jnp.where(hot, -jnp.inf, work)

    exps = [jnp.exp(v - vals[0]) for v in vals]
    denom = exps[0] + exps[1] + exps[2] + exps[3]

    hot_sum = sum(jnp.where(hot, 1.0, 0.0) for hot in hots)
    r_i = lax.broadcasted_iota(jnp.int32, (tt, tt), 0)
    c_i = lax.broadcasted_iota(jnp.int32, (tt, tt), 1)
    strict = jnp.where(c_i < r_i, 1.0, 0.0).astype(BF16)
    prefix = jnp.dot(strict, hot_sum.astype(BF16), preferred_element_type=F32) + carry_ref[...]

    idx_out = jnp.zeros((tt, LANES), jnp.int32)
    rank_out = jnp.zeros((tt, LANES), jnp.int32)
    gate_out = jnp.zeros((tt, LANES), F32)
    for k in range(TOP_K):
        rank_k = jnp.sum(jnp.where(hots[k], prefix, 0.0), axis=-1, keepdims=True).astype(jnp.int32)
        sel = lane == k
        idx_out = jnp.where(sel, idxs[k], idx_out)
        rank_out = jnp.where(sel, rank_k, rank_out)
        gate_out = jnp.where(sel, exps[k] / denom, gate_out)
    idx_ref[...] = idx_out
    rank_ref[...] = rank_out
    gate_ref[...] = gate_out

    carry_ref[...] = carry_ref[...] + jnp.sum(hot_sum, axis=0, keepdims=True)
    cnt_ref[...] = carry_ref[...]


def _router(x1, norm_w, router_w_pad, router_b_pad, tt=512):
    t = x1.shape[0]
    tt = min(tt, t)
    kern = functools.partial(_router_kernel, tt=tt)
    row = pl.BlockSpec((tt, LANES), lambda i: (i, 0))
    return pl.pallas_call(
        kern,
        grid=(t // tt,),
        in_specs=[
            pl.BlockSpec((tt, D_MODEL), lambda i: (i, 0)),
            pl.BlockSpec((1, D_MODEL), lambda i: (0, 0)),
            pl.BlockSpec((D_MODEL, LANES), lambda i: (0, 0)),
            pl.BlockSpec((1, LANES), lambda i: (0, 0)),
        ],
        out_specs=[pl.BlockSpec((tt, D_MODEL), lambda i: (i, 0)), row, row, row,
                   pl.BlockSpec((1, LANES), lambda i: (0, 0))],
        out_shape=[
            jax.ShapeDtypeStruct((t, D_MODEL), F32),
            jax.ShapeDtypeStruct((t, LANES), jnp.int32),
            jax.ShapeDtypeStruct((t, LANES), F32),
            jax.ShapeDtypeStruct((t, LANES), jnp.int32),
            jax.ShapeDtypeStruct((1, LANES), F32),
        ],
        scratch_shapes=[pltpu.VMEM((1, LANES), F32)],
        compiler_params=_cparams(("arbitrary",)),
        name="router",
    )(x1, norm_w, router_w_pad, router_b_pad)


def _dispatch_kernel(pos_ref, h_ref, o_ref, sem, *, tt):
    def copy(t, k):
        return pltpu.make_async_copy(h_ref.at[pl.ds(t, 1), :], o_ref.at[pl.ds(pos_ref[t * TOP_K + k], 1), :], sem)

    def issue(t, c):
        for k in range(TOP_K):
            copy(t, k).start()
        return c

    lax.fori_loop(0, tt, issue, 0)

    for _ in range(TOP_K):
        pltpu.make_async_copy(h_ref, o_ref.at[pl.ds(0, tt), :], sem).wait()


def _dispatch(h2, pos_flat, tt=256):
    t = h2.shape[0]
    tt = min(tt, t)
    kern = functools.partial(_dispatch_kernel, tt=tt)
    return pl.pallas_call(
        kern,
        grid=(t // tt,),
        in_specs=[
            pl.BlockSpec((tt * TOP_K,), lambda i: (i,), memory_space=pltpu.SMEM),
            pl.BlockSpec((tt, D_MODEL), lambda i: (i, 0)),
        ],
        out_specs=pl.BlockSpec(memory_space=pl.ANY),
        out_shape=jax.ShapeDtypeStruct((t * TOP_K, D_MODEL), F32),
        scratch_shapes=[pltpu.SemaphoreType.DMA(())],
        compiler_params=_cparams(("arbitrary",)),
        name="dispatch",
    )(pos_flat, h2)


def _experts_kernel(tile_ref, exp_ref, first_ref, efirst_ref, valid_ref, start_ref,
                    x_ref, wu_ref, bu_ref, wd_ref, bd_ref, o_ref, wub_ref, wdb_ref, *, tm):
    w = pl.program_id(0)

    @pl.when(efirst_ref[w] == 1)
    def _():
        for c0 in range(0, 2 * D_EXPERT, 512):
            wub_ref[:, c0:c0 + 512] = wu_ref[0, :, c0:c0 + 512].astype(BF16)
        for c0 in range(0, D_MODEL, 512):
            wdb_ref[:, c0:c0 + 512] = wd_ref[0, :, c0:c0 + 512].astype(BF16)

    @pl.when(valid_ref[w] == 1)
    def _():
        e = exp_ref[w]
        gu = jnp.dot(x_ref[...].astype(BF16), wub_ref[...], preferred_element_type=F32) + bu_ref[0]
        g = jnp.minimum(gu[:, :D_EXPERT], SWIGLU_LIMIT)
        u = jnp.clip(gu[:, D_EXPERT:], -SWIGLU_LIMIT, SWIGLU_LIMIT)
        act = (u + 1.0) * (g * _sigmoid(SWIGLU_ALPHA * g))
        out = jnp.dot(act.astype(BF16), wdb_ref[...], preferred_element_type=F32) + bd_ref[0]
        rows = tile_ref[w] * tm + lax.broadcasted_iota(jnp.int32, (tm, 1), 0)
        mine = (rows >= start_ref[e]) & (rows < start_ref[e + 1])

        @pl.when(first_ref[w] == 1)
        def _():
            _rows_to_tiles(o_ref, jnp.where(mine, out, 0.0), tm)

        @pl.when(first_ref[w] == 0)
        def _():
            _rows_to_tiles(o_ref, jnp.where(mine, out, _tiles_to_rows(o_ref, tm)), tm)


def _experts(x_sorted, item_tile, item_exp, item_first, item_efirst, item_valid, starts, w_up, b_up, w_down, b_down, tm):
    n = x_sorted.shape[0]
    n_items = item_tile.shape[0]
    kern = functools.partial(_experts_kernel, tm=tm)
    grid_spec = pltpu.PrefetchScalarGridSpec(
        num_scalar_prefetch=6,
        grid=(n_items,),
        in_specs=[
            pl.BlockSpec((tm, D_MODEL), lambda w, tl, ex, *_: (tl[w], 0)),
            pl.BlockSpec((1, D_MODEL, 2 * D_EXPERT), lambda w, tl, ex, *_: (ex[w], 0, 0)),
            pl.BlockSpec((1, 1, 2 * D_EXPERT), lambda w, tl, ex, *_: (ex[w], 0, 0)),
            pl.BlockSpec((1, D_EXPERT, D_MODEL), lambda w, tl, ex, *_: (ex[w], 0, 0)),
            pl.BlockSpec((1, 1, D_MODEL), lambda w, tl, ex, *_: (ex[w], 0, 0)),
        ],
        out_specs=pl.BlockSpec((tm * ROW_CHUNKS, LANES), lambda w, tl, ex, *_: (tl[w], 0)),
        scratch_shapes=[pltpu.VMEM((D_MODEL, 2 * D_EXPERT), BF16), pltpu.VMEM((D_EXPERT, D_MODEL), BF16)],
    )
    return pl.pallas_call(
        kern,
        grid_spec=grid_spec,
        out_shape=jax.ShapeDtypeStruct((n * ROW_CHUNKS, LANES), F32),
        compiler_params=_cparams(("arbitrary",)),
        name="experts",
    )(item_tile, item_exp, item_first, item_efirst, item_valid, starts, x_sorted, w_up, b_up, w_down, b_down)


def _combine_kernel(pos_ref, posn_ref, gate_ref, x_ref, p_ref, pnw_ref, wpg_ref, wpp_ref, fnw_ref, ys_ref, o_ref,
                    rows_ref, sem, *, tt):
    i = pl.program_id(0)
    n = pl.num_programs(0)
    slot = i % 2

    def gather(idx_ref, dst_slot):
        def issue(t, c):
            for k in range(TOP_K):
                src = pl.multiple_of(idx_ref[t * TOP_K + k] * ROW_CHUNKS, ROW_CHUNKS)
                dst = pl.multiple_of((k * tt + t) * ROW_CHUNKS, ROW_CHUNKS)
                pltpu.make_async_copy(ys_ref.at[pl.ds(src, ROW_CHUNKS), :],
                                      rows_ref.at[dst_slot, pl.ds(dst, ROW_CHUNKS), :], sem.at[dst_slot]).start()
            return c

        lax.fori_loop(0, tt, issue, 0)

    @pl.when(i == 0)
    def _():
        gather(pos_ref, 0)

    @pl.when(i + 1 < n)
    def _():
        gather(posn_ref, 1 - slot)

    pltpu.make_async_copy(ys_ref.at[pl.ds(0, TOP_K * tt * ROW_CHUNKS), :], rows_ref.at[slot], sem.at[slot]).wait()

    gate = gate_ref[...]
    x = x_ref[...]
    for k in range(TOP_K):
        x = x + gate[:, k:k + 1] * _tiles_to_rows(rows_ref.at[slot], tt, base=k * tt * ROW_CHUNKS)

    hg = _rms(x, pnw_ref[...]).astype(BF16)
    pg = _sigmoid(jnp.dot(hg, wpg_ref[...], preferred_element_type=F32))
    x = x + pg * jnp.dot(p_ref[...].astype(BF16), wpp_ref[...], preferred_element_type=F32)
    o_ref[...] = _rms(x, fnw_ref[...])


def _combine(pos_flat, gate, x1, p2d, ple_nw, w_pg, w_pp, final_nw, y_sorted, tt=256):
    t = x1.shape[0]
    tt = min(tt, t)
    kern = functools.partial(_combine_kernel, tt=tt)
    vec = pl.BlockSpec((1, D_MODEL), lambda i: (0, 0))
    nt = t // tt
    return pl.pallas_call(
        kern,
        grid=(nt,),
        in_specs=[
            pl.BlockSpec((tt * TOP_K,), lambda i: (i,), memory_space=pltpu.SMEM),
            pl.BlockSpec((tt * TOP_K,), lambda i: (jnp.minimum(i + 1, nt - 1),), memory_space=pltpu.SMEM),
            pl.BlockSpec((tt, LANES), lambda i: (i, 0)),
            pl.BlockSpec((tt, D_MODEL), lambda i: (i, 0)),
            pl.BlockSpec((tt, PLE_DIM), lambda i: (i, 0)),
            vec,
            pl.BlockSpec((D_MODEL, D_MODEL), lambda i: (0, 0)),
            pl.BlockSpec((PLE_DIM, D_MODEL), lambda i: (0, 0)),
            vec,
            pl.BlockSpec(memory_space=pl.ANY),
        ],
        out_specs=pl.BlockSpec((tt, D_MODEL), lambda i: (i, 0)),
        out_shape=jax.ShapeDtypeStruct((t, D_MODEL), F32),
        scratch_shapes=[pltpu.VMEM((2, TOP_K * tt * ROW_CHUNKS, LANES), F32), pltpu.SemaphoreType.DMA((2,))],
        compiler_params=_cparams(("arbitrary",)),
        name="combine",
    )(pos_flat, pos_flat, gate, x1, p2d, ple_nw, w_pg, w_pp, final_nw, y_sorted)


def _routing_plan(counts, idx, rank, tm, n_rows):
    counts = counts.astype(jnp.int32)
    ends = jnp.cumsum(counts)
    starts = ends - counts
    pos = (starts[idx] + rank).reshape(-1)
    n_tiles = n_rows // tm
    n_items = n_tiles + N_EXPERTS - 1
    first_tile = starts // tm
    last_tile = jnp.maximum(ends - 1, 0) // tm
    items_e = jnp.where(counts > 0, last_tile - first_tile + 1, 0)
    item_end = jnp.cumsum(items_e)
    item_start = item_end - items_e
    total = item_end[-1]
    w = jnp.arange(n_items, dtype=jnp.int32)
    e_w = jnp.minimum(jnp.sum(w[:, None] >= item_end[None, :], axis=1).astype(jnp.int32), N_EXPERTS - 1)
    tile_w = first_tile[e_w] + (w - item_start[e_w])
    valid = w < total
    last = jnp.maximum(total - 1, 0)
    e_w = jnp.where(valid, e_w, e_w[last])
    tile_w = jnp.where(valid, tile_w, tile_w[last])
    prev_tile = jnp.concatenate([jnp.full((1,), -1, jnp.int32), tile_w[:-1]])
    first = (tile_w != prev_tile) & valid
    prev_e = jnp.concatenate([jnp.full((1,), -1, jnp.int32), e_w[:-1]])
    efirst = (e_w != prev_e) & valid
    starts_ext = jnp.concatenate([starts, ends[-1:]]).astype(jnp.int32)
    return (pos.astype(jnp.int32), tile_w.astype(jnp.int32), e_w, first.astype(jnp.int32), efirst.astype(jnp.int32),
            valid.astype(jnp.int32), starts_ext)


def kernel(x, p, mix_norm_w, w_in, lambda_q1, lambda_k1, lambda_q2, lambda_k2, da_head_norm_w, w_attn_branch, conv_w, conv_b, dt_bias, a_log, d_skip, ssd_norm_w, w_ssd_branch, w_out, moe_norm_w, router_w, router_b, w_up, b_up, w_down, b_down, ple_norm_w, w_ple_gate, w_ple_proj, final_norm_w):
    batch, seq, _ = x.shape
    t = batch * seq
    depth = w_in.shape[0]
    assert depth == 1, "the final RMSNorm is fused into the layer's last kernel"
    x2d = x.reshape(t, D_MODEL)

    log2e = math.log2(math.e)
    slopes = jnp.asarray([log2e * 2.0 ** (-8.0 * (h + 1) / DA_HEADS) for h in range(DA_HEADS)], F32)
    q_scale = log2e * DA_QK_DIM ** -0.5
    head_ids = jnp.arange(SSD_D_INNER, dtype=jnp.int32) // SSD_HEAD_DIM
    expand = (jnp.arange(LANES, dtype=jnp.int32)[:, None] == head_ids[None, :]).astype(BF16)

    def pad_lanes(v, fill=0.0):
        return jnp.pad(v.reshape(1, -1), ((0, 0), (0, LANES - v.shape[-1])), constant_values=fill)

    for i in range(depth):
        wi = w_in[i]
        w_main = jnp.concatenate([
            wi[:, _R_Z:_R_Z + SSD_D_INNER], wi[:, _R_Q:_R_Q + 1024] * q_scale, wi[:, _R_K:_R_K + 1024],
            wi[:, _R_GA:_R_GA + 1024], wi[:, _R_GS:_R_GS + 1024], wi[:, _R_XBC:_R_XBC + SSD_CONV_DIM],
            wi[:, _R_V:_R_V + 1024]], axis=1).astype(BF16)
        w_dt = jnp.pad(wi[:, _R_DT:_R_DT + SSD_HEADS], ((0, 0), (0, LANES - SSD_HEADS))).astype(BF16)

        proj, dt_raw, v_t = _in_proj(x2d, mix_norm_w[i].reshape(1, -1), w_main, w_dt)

        lambda_init = 0.8 - 0.6 * math.exp(-0.3 * i)
        lam = (jnp.exp(jnp.sum(lambda_q1[i] * lambda_k1[i])) - jnp.exp(jnp.sum(lambda_q2[i] * lambda_k2[i]))
               + lambda_init).reshape(1).astype(F32)
        y_attn = _diff_attn(proj, v_t, slopes, lam, da_head_norm_w[i].reshape(1, -1), batch, seq, lambda_init)

        y_ssd = _ssd(proj, dt_raw, conv_w[i], conv_b[i].reshape(1, -1), pad_lanes(dt_bias[i]), pad_lanes(a_log[i]),
                     jnp.repeat(d_skip[i], SSD_HEAD_DIM).reshape(1, -1), ssd_norm_w[i].reshape(1, -1), expand,
                     batch, seq)

        x1 = _merge(y_attn, y_ssd, proj, x2d, w_attn_branch[i].astype(BF16), w_ssd_branch[i].astype(BF16),
                    w_out[i].astype(BF16))

        rw = jnp.pad(router_w[i], ((0, 0), (0, LANES - N_EXPERTS)))
        h2, idx, gate, rank, counts = _router(x1, moe_norm_w[i].reshape(1, -1), rw, pad_lanes(router_b[i]))

        tm = 512
        n_rows = t * TOP_K
        pos, item_tile, item_exp, item_first, item_efirst, item_valid, starts = _routing_plan(
            counts[0, :N_EXPERTS], idx[:, :TOP_K], rank[:, :TOP_K], tm, n_rows)

        x_sorted = _dispatch(h2, pos)
        y_sorted = _experts(x_sorted, item_tile, item_exp, item_first, item_efirst, item_valid, starts,
                            w_up[i], b_up[i].reshape(N_EXPERTS, 1, -1),
                            w_down[i], b_down[i].reshape(N_EXPERTS, 1, -1), tm)

        x2d = _combine(pos, gate, x1, p[i].reshape(t, PLE_DIM), ple_norm_w[i].reshape(1, -1),
                       w_ple_gate[i].astype(BF16), w_ple_proj[i].astype(BF16), final_norm_w.reshape(1, -1), y_sorted)
    return x2d.reshape(batch, seq, D_MODEL)
```

```python
import functools
import math

import jax
import jax.numpy as jnp
from jax import lax
from jax.experimental import pallas as pl
from jax.experimental.pallas import tpu as pltpu

F32 = jnp.float32
BF16 = jnp.bfloat16

D_MODEL = 1024
CHUNK = 64
PLE_DIM = 256
RMS_EPS = 1e-6

DA_HEADS = 8
DA_QK_DIM = 64
DA_V_DIM = 128

SSD_D_INNER = 2048
SSD_HEAD_DIM = 64
SSD_HEADS = 32
SSD_GROUPS = 4
SSD_STATE = 128
SSD_CONV = 4
SSD_CONV_DIM = 3072

N_EXPERTS = 32
TOP_K = 4
D_EXPERT = 1024
SWIGLU_LIMIT = 7.0
SWIGLU_ALPHA = 1.702

LANES = 128

COL_Z = 0
COL_Q = 2048
COL_K = 3072
COL_GA = 4096
COL_GS = 5120
COL_XBC = 6144
PROJ_W = 9216

ATTN_TILE = 256
ATTN_Q_TILE = 512
SUM_ROWS = 16

_R_Q, _R_K, _R_V, _R_Z, _R_XBC, _R_DT, _R_GA, _R_GS = 0, 1024, 2048, 3072, 5120, 8192, 8224, 9248

VMEM_LIMIT = 56 * 1024 * 1024


def _cparams(sem):
    return pltpu.CompilerParams(dimension_semantics=sem, vmem_limit_bytes=VMEM_LIMIT)


def _rms(x, w):
    return x * lax.rsqrt(jnp.mean(x * x, axis=-1, keepdims=True) + RMS_EPS) * w


ROW_CHUNKS = D_MODEL // LANES


def _rows_to_tiles(ref, val, n):
    for c in range(ROW_CHUNKS):
        ref[pl.ds(c, n, stride=ROW_CHUNKS), :] = val[:, c * LANES:(c + 1) * LANES]


def _tiles_to_rows(ref, n, base=0):
    return jnp.concatenate([ref[pl.ds(base + c, n, stride=ROW_CHUNKS), :] for c in range(ROW_CHUNKS)], axis=1)


def _sigmoid(x):
    return 1.0 / (1.0 + jnp.exp2(x * (-math.log2(math.e))))


def _inproj_kernel(x_ref, nw_ref, w_ref, wdt_ref, o_ref, dt_ref, vt_ref, h_ref, *, tm, n_proj):
    j = pl.program_id(1)

    @pl.when(j == 0)
    def _():
        hb = _rms(x_ref[...], nw_ref[...]).astype(BF16)
        h_ref[...] = hb
        dt_ref[...] = jnp.dot(hb, wdt_ref[...], preferred_element_type=F32)

    acc = jnp.dot(h_ref[...], w_ref[...], preferred_element_type=F32)

    @pl.when(j < n_proj)
    def _():
        o_ref[...] = acc.astype(BF16)

    @pl.when(j == n_proj)
    def _():
        vt = acc.T
        for c in range(tm // ATTN_TILE):
            vt_ref[c] = vt[:, c * ATTN_TILE:(c + 1) * ATTN_TILE].astype(BF16)


def _in_proj(x2d, norm_w, w_main, w_dt, tm=2048):
    t = x2d.shape[0]
    tm = min(tm, t)
    tn = DA_HEADS * DA_V_DIM
    n_proj = PROJ_W // tn
    return pl.pallas_call(
        functools.partial(_inproj_kernel, tm=tm, n_proj=n_proj),
        grid=(t // tm, n_proj + 1),
        in_specs=[
            pl.BlockSpec((tm, D_MODEL), lambda i, j: (i, 0)),
            pl.BlockSpec((1, D_MODEL), lambda i, j: (0, 0)),
            pl.BlockSpec((D_MODEL, tn), lambda i, j: (0, j)),
            pl.BlockSpec((D_MODEL, LANES), lambda i, j: (0, 0)),
        ],
        out_specs=[
            pl.BlockSpec((tm, tn), lambda i, j: (i, jnp.minimum(j, n_proj - 1))),
            pl.BlockSpec((tm, LANES), lambda i, j: (i, 0)),
            pl.BlockSpec((tm // ATTN_TILE, tn, ATTN_TILE), lambda i, j: (i, 0, 0)),
        ],
        out_shape=[
            jax.ShapeDtypeStruct((t, PROJ_W), BF16),
            jax.ShapeDtypeStruct((t, LANES), F32),
            jax.ShapeDtypeStruct((t // ATTN_TILE, tn, ATTN_TILE), BF16),
        ],
        scratch_shapes=[pltpu.VMEM((tm, D_MODEL), BF16)],
        compiler_params=_cparams(("parallel", "arbitrary")),
        name="in_proj",
    )(x2d, norm_w, w_main, w_dt)


def _attn_kernel(slopes_ref, lam_ref, q_ref, k_ref, vt_ref, nw_ref, o_ref,
                 bias_ref, qc_ref, s_ref, p_ref, al_ref, m_ref, a_ref, *, tq, tk, out_scale):
    h = pl.program_id(1)
    i = pl.program_id(2)
    slope = slopes_ref[h]
    lam = lam_ref[0]
    r = tq // tk
    n_tiles = r * (i + 1)

    @pl.when(i == 0)
    def _():
        kk = lax.broadcasted_iota(jnp.int32, (tk, tq), 0)
        qq = lax.broadcasted_iota(jnp.int32, (tk, tq), 1)
        bias_ref[0] = -slope * (qq - kk).astype(F32)
        for d in range(r):
            kd = kk + d * tk
            bias_ref[1 + d] = jnp.where((kd // CHUNK) <= (qq // CHUNK), -slope * jnp.abs(qq - kd).astype(F32), -jnp.inf)

    qt = q_ref[...].astype(F32).T.astype(BF16)
    dim = lax.broadcasted_iota(jnp.int32, qt.shape, 0)
    zero = jnp.zeros_like(qt)
    qc_ref[...] = jnp.concatenate([jnp.where(dim < DA_QK_DIM, qt, zero), jnp.where(dim >= DA_QK_DIM, qt, zero)], axis=1)

    m_ref[...] = jnp.full(m_ref.shape, -jnp.inf, F32)
    a_ref[...] = jnp.zeros(a_ref.shape, F32)

    def scores(j):
        k = k_ref[pl.ds(pl.multiple_of(j * tk, tk), tk), :]
        return jnp.dot(k, qc_ref[...], preferred_element_type=F32)

    def softmax_step(sel, cj):
        for g in range(2 * tq // LANES):
            cols = slice(g * LANES, (g + 1) * LANES)
            bcols = slice((g * LANES) % tq, (g * LANES) % tq + LANES)
            s = s_ref[:, cols] + bias_ref[sel, :, bcols]
            m_prev = m_ref[:, cols]
            m_cur = jnp.maximum(m_prev, jnp.max(s, axis=0, keepdims=True) + cj)
            alpha = jnp.exp2(m_prev - m_cur)
            p = jnp.exp2(s + (cj - m_cur))
            m_ref[:, cols] = m_cur
            al_ref[:, cols] = alpha
            p_ref[:, cols] = p.astype(BF16)

    def pv_step(j):
        vt = jnp.concatenate([vt_ref[j], jnp.ones((SUM_ROWS, tk), BF16)], axis=0)
        for g in range(2 * tq // 256):
            cols = slice(g * 256, (g + 1) * 256)
            a_ref[:, cols] = al_ref[:, cols] * a_ref[:, cols] + jnp.dot(vt, p_ref[:, cols], preferred_element_type=F32)

    def softmax_tile(j):
        past = j < r * i
        cj = jnp.where(past, -slope * (i * tq - j * tk).astype(F32), 0.0)
        softmax_step(jnp.maximum(j - r * i + 1, 0), cj)

    s_ref[...] = scores(0)
    s_next = scores(1)
    softmax_tile(0)
    s_ref[...] = s_next

    def body(j, carry):
        s_next = scores(j + 1)
        pv_step(j - 1)
        softmax_tile(j)
        s_ref[...] = s_next
        return carry

    lax.fori_loop(1, n_tiles - 1, body, 0)

    pv_step(n_tiles - 2)
    softmax_tile(n_tiles - 1)
    pv_step(n_tiles - 1)

    on = a_ref[0:DA_V_DIM, :] / a_ref[DA_V_DIM:DA_V_DIM + 1, :]
    ot = on[:, :tq] - lam * on[:, tq:]
    ot = ot * lax.rsqrt(jnp.mean(ot * ot, axis=0, keepdims=True) + RMS_EPS)
    o_ref[...] = (ot.T * nw_ref[...] * out_scale).astype(BF16)


def _diff_attn(proj, v_t, slopes, lam, head_norm_w, batch, seq, lambda_init):
    tk = ATTN_TILE
    tq = min(ATTN_Q_TILE, seq)
    nq = seq // tq
    nk = seq // tk
    t = batch * seq
    qb, kb = COL_Q // LANES, COL_K // LANES
    assert tq % tk == 0 and tq >= 2 * tk, "the peeled first/last key tiles need at least two tiles per step"
    kern = functools.partial(_attn_kernel, tq=tq, tk=tk, out_scale=1.0 - lambda_init)
    smem = pl.BlockSpec(memory_space=pltpu.SMEM)
    return pl.pallas_call(
        kern,
        grid=(batch, DA_HEADS, nq),
        in_specs=[
            smem, smem,
            pl.BlockSpec((tq, LANES), lambda b, h, i: (b * nq + i, qb + h)),
            pl.BlockSpec((seq, LANES), lambda b, h, i: (b, kb + h)),
            pl.BlockSpec((nk, DA_V_DIM, tk), lambda b, h, i: (b, h, 0)),
            pl.BlockSpec((1, LANES), lambda b, h, i: (0, 0)),
        ],
        out_specs=pl.BlockSpec((tq, LANES), lambda b, h, i: (b * nq + i, h)),
        out_shape=jax.ShapeDtypeStruct((t, DA_HEADS * DA_V_DIM), BF16),
        scratch_shapes=[
            pltpu.VMEM((1 + tq // tk, tk, tq), F32),
            pltpu.VMEM((LANES, 2 * tq), BF16),
            pltpu.VMEM((tk, 2 * tq), F32),
            pltpu.VMEM((tk, 2 * tq), BF16),
            pltpu.VMEM((1, 2 * tq), F32),
            pltpu.VMEM((1, 2 * tq), F32),
            pltpu.VMEM((DA_V_DIM + SUM_ROWS, 2 * tq), F32),
        ],
        compiler_params=_cparams(("parallel", "parallel", "arbitrary")),
        name="diff_attn",
    )(slopes, lam, proj, proj, v_t, head_norm_w)


def _split2(x):
    hi = x.astype(BF16)
    lo = (x - hi.astype(F32)).astype(BF16)
    return hi, lo


def _split3(x):
    hi = x.astype(BF16)
    r = x - hi.astype(F32)
    mid = r.astype(BF16)
    lo = (r - mid.astype(F32)).astype(BF16)
    return hi, mid, lo


def _ssd_kernel(xbc_ref, z_ref, dt_ref, cw_ref, cb_ref, dtb_ref, alog_ref, dsk_ref, nw_ref, e_ref,
                o_ref, ubuf, shift_ref, xs_scr, bc_scr, acum_scr, acb, dfs_scr, xdt_scr, xw_scr, y_scr, state, *, tr):
    i = pl.program_id(1)
    nch = tr // CHUNK
    gw = SSD_D_INNER // SSD_GROUPS
    pad = 8

    head = 16

    @pl.when(i == 0)
    def _():
        ubuf[0:pad, :] = jnp.zeros((pad, SSD_CONV_DIM), F32)
        state[...] = jnp.zeros_like(state)
        r_s = lax.broadcasted_iota(jnp.int32, (tr, tr), 0)
        c_s = lax.broadcasted_iota(jnp.int32, (tr, tr), 1)
        for d in range(1, SSD_CONV):
            shift_ref[(d - 1) * tr:d * tr, :] = jnp.where(c_s == r_s - d, 1.0, 0.0).astype(BF16)

    ubuf[pad:pad + head, :] = xbc_ref[0:head, :].astype(F32)
    cs = 512
    for c0 in range(0, SSD_CONV_DIM, cs):
        ub = xbc_ref[:, c0:c0 + cs]
        sh = jnp.dot(shift_ref[...], ub, preferred_element_type=F32)
        acc = cb_ref[:, c0:c0 + cs] + cw_ref[SSD_CONV - 1:SSD_CONV, c0:c0 + cs] * ub.astype(F32)
        for d in range(1, SSD_CONV):
            acc = acc + cw_ref[SSD_CONV - 1 - d:SSD_CONV - d, c0:c0 + cs] * sh[(d - 1) * tr:d * tr, :]
        u = acc * _sigmoid(acc)
        acc_h = jnp.broadcast_to(cb_ref[:, c0:c0 + cs], (head, cs))
        for k in range(SSD_CONV):
            off = pad - (SSD_CONV - 1) + k
            acc_h = acc_h + cw_ref[k:k + 1, c0:c0 + cs] * ubuf[off:off + head, c0:c0 + cs]
        u_h = acc_h * _sigmoid(acc_h)
        if c0 < SSD_D_INNER:
            xs_scr[:, c0:c0 + cs] = u
            xs_scr[0:head, c0:c0 + cs] = u_h
        else:
            bc_scr[:, c0 - SSD_D_INNER:c0 - SSD_D_INNER + cs] = u.astype(BF16)
            bc_scr[0:head, c0 - SSD_D_INNER:c0 - SSD_D_INNER + cs] = u_h.astype(BF16)
    ubuf[0:pad, :] = xbc_ref[tr - 2 * pad:tr, :].astype(F32)[pad:, :]

    xdt_in = dt_ref[...] + dtb_ref[...]
    dtv = jnp.maximum(xdt_in, 0.0) + jnp.log1p(jnp.exp(-jnp.abs(xdt_in)))
    da = dtv * (-jnp.exp(alog_ref[...]))
    r_i = lax.broadcasted_iota(jnp.int32, (tr, tr), 0)
    c_i = lax.broadcasted_iota(jnp.int32, (tr, tr), 1)
    tril = jnp.where((c_i <= r_i) & ((c_i // CHUNK) == (r_i // CHUNK)), 1.0, 0.0).astype(BF16)
    acum = sum(jnp.dot(tril, piece, preferred_element_type=F32) for piece in _split3(da))
    acum_scr[...] = acum

    a_hi, a_lo = _split2(acum)
    d_hi, d_lo = _split2(dtv)
    for c0 in range(0, SSD_D_INNER, cs):
        e = e_ref[:, c0:c0 + cs]
        ab = jnp.dot(a_hi, e, preferred_element_type=F32) + jnp.dot(a_lo, e, preferred_element_type=F32)
        db = jnp.dot(d_hi, e, preferred_element_type=F32) + jnp.dot(d_lo, e, preferred_element_type=F32)
        acb[:, c0:c0 + cs] = ab
        dfs_scr[:, c0:c0 + cs] = jnp.exp(ab)
        xdt = xs_scr[:, c0:c0 + cs] * db
        xdt_scr[:, c0:c0 + cs] = xdt.astype(BF16)
        for c in range(nch):
            r0 = c * CHUNK
            last = ab[r0 + CHUNK - 1:r0 + CHUNK, :]
            xw_scr[r0:r0 + CHUNK, c0:c0 + cs] = (xdt[r0:r0 + CHUNK, :] * jnp.exp(last - ab[r0:r0 + CHUNK, :])).astype(BF16)

    lane2 = lax.broadcasted_iota(jnp.int32, (CHUNK, LANES), 1)
    row2 = lax.broadcasted_iota(jnp.int32, (CHUNK, LANES), 0)
    causal2 = row2 >= (lane2 % SSD_HEAD_DIM)
    lo_half = lane2 < SSD_HEAD_DIM
    zpad = jnp.zeros((LANES - CHUNK, LANES), F32)

    def chunk_body(c, carry):
        r0 = pl.multiple_of(c * CHUNK, CHUNK)
        rows = pl.ds(r0, CHUNK)
        a_c = acum_scr[rows, :]
        a_t = jnp.concatenate([a_c, zpad], axis=0).T
        a_t_r = pltpu.roll(a_t, SSD_HEAD_DIM, 1)
        last_row = acb[pl.ds(r0 + CHUNK - 1, 1), :]
        for g in range(SSD_GROUPS):
            bg = bc_scr[rows, g * SSD_STATE:(g + 1) * SSD_STATE]
            cg = bc_scr[rows, SSD_GROUPS * SSD_STATE + g * SSD_STATE:SSD_GROUPS * SSD_STATE + (g + 1) * SSD_STATE]
            cbt = lax.dot_general(cg, jnp.concatenate([bg, bg], axis=0), (((1,), (1,)), ((), ())),
                                  preferred_element_type=F32)
            g0 = g * gw
            st = state[g]
            yoff = jnp.dot(cg, st.astype(BF16), preferred_element_type=F32) * dfs_scr[rows, g0:g0 + gw]
            for jj in range(gw // LANES):
                pidx = g * (gw // LANES) + jj
                l0 = pidx * LANES
                arow = a_t[2 * pidx:2 * pidx + 1, :] + a_t_r[2 * pidx + 1:2 * pidx + 2, :]
                seg = acb[rows, l0:l0 + LANES] - arow
                decay = jnp.exp(jnp.where(causal2, seg, -jnp.inf))
                mp = (cbt * decay).astype(BF16)
                xd = xdt_scr[rows, l0:l0 + LANES]
                zb = jnp.zeros_like(xd)
                bd = jnp.concatenate([jnp.where(lo_half, xd, zb), jnp.where(lo_half, zb, xd)], axis=0)
                yd = jnp.dot(mp, bd, preferred_element_type=F32)
                y_scr[rows, l0:l0 + LANES] = yd + yoff[:, jj * LANES:(jj + 1) * LANES]
            cd = jnp.exp(last_row[:, g0:g0 + gw])
            upd = lax.dot_general(bg, xw_scr[rows, g0:g0 + gw], (((0,), (0,)), ((), ())),
                                  preferred_element_type=F32)
            state[g] = st * cd + upd
        return carry

    lax.fori_loop(0, nch, chunk_body, 0, unroll=True)

    for g in range(SSD_GROUPS):
        g0 = g * gw
        y = y_scr[:, g0:g0 + gw] + dsk_ref[:, g0:g0 + gw] * xs_scr[:, g0:g0 + gw]
        zz = z_ref[:, g0:g0 + gw].astype(F32)
        y = y * (zz * _sigmoid(zz))
        o_ref[:, g0:g0 + gw] = _rms(y, nw_ref[:, g0:g0 + gw]).astype(BF16)


def _ssd(proj, dt_raw, conv_w, conv_b, dt_bias, a_log, d_skip_b, norm_w, expand, batch, seq, tr=256):
    t = batch * seq
    nr = seq // tr
    kern = functools.partial(_ssd_kernel, tr=tr)
    const = lambda shape: pl.BlockSpec(shape, lambda b, i: (0, 0))
    return pl.pallas_call(
        kern,
        grid=(batch, nr),
        in_specs=[
            pl.BlockSpec((tr, SSD_CONV_DIM), lambda b, i: (b * nr + i, COL_XBC // SSD_CONV_DIM)),
            pl.BlockSpec((tr, SSD_D_INNER), lambda b, i: (b * nr + i, COL_Z // SSD_D_INNER)),
            pl.BlockSpec((tr, LANES), lambda b, i: (b * nr + i, 0)),
            const((SSD_CONV, SSD_CONV_DIM)),
            const((1, SSD_CONV_DIM)),
            const((1, LANES)),
            const((1, LANES)),
            const((1, SSD_D_INNER)),
            const((1, SSD_D_INNER)),
            const((LANES, SSD_D_INNER)),
        ],
        out_specs=pl.BlockSpec((tr, SSD_D_INNER), lambda b, i: (b * nr + i, 0)),
        out_shape=jax.ShapeDtypeStruct((t, SSD_D_INNER), BF16),
        scratch_shapes=[
            pltpu.VMEM((8 + 16, SSD_CONV_DIM), F32),
            pltpu.VMEM(((SSD_CONV - 1) * tr, tr), BF16),
            pltpu.VMEM((tr, SSD_D_INNER), F32),
            pltpu.VMEM((tr, 2 * SSD_GROUPS * SSD_STATE), BF16),
            pltpu.VMEM((tr, LANES), F32),
            pltpu.VMEM((tr, SSD_D_INNER), F32),
            pltpu.VMEM((tr, SSD_D_INNER), F32),
            pltpu.VMEM((tr, SSD_D_INNER), BF16),
            pltpu.VMEM((tr, SSD_D_INNER), BF16),
            pltpu.VMEM((tr, SSD_D_INNER), F32),
            pltpu.VMEM((SSD_GROUPS, SSD_STATE, SSD_D_INNER // SSD_GROUPS), F32),
        ],
        compiler_params=_cparams(("parallel", "arbitrary")),
        name="ssd",
    )(proj, proj, dt_raw, conv_w, conv_b, dt_bias, a_log, d_skip_b, norm_w, expand)


def _merge_kernel(ya_ref, ys_ref, ga_ref, gs_ref, x_ref, wa_ref, wb_ref, wo_ref, o_ref):
    a = jnp.dot(ya_ref[...], wa_ref[...], preferred_element_type=F32)
    s = jnp.dot(ys_ref[...], wb_ref[...], preferred_element_type=F32)
    merged = _sigmoid(ga_ref[...].astype(F32)) * a + _sigmoid(gs_ref[...].astype(F32)) * s
    o_ref[...] = x_ref[...] + jnp.dot(merged.astype(BF16), wo_ref[...], preferred_element_type=F32)


def _merge(y_attn, y_ssd, proj, x2d, w_a, w_b, w_o, tm=512):
    t = x2d.shape[0]
    tm = min(tm, t)
    full = lambda r, c: pl.BlockSpec((r, c), lambda i: (0, 0))
    return pl.pallas_call(
        _merge_kernel,
        grid=(t // tm,),
        in_specs=[
            pl.BlockSpec((tm, D_MODEL), lambda i: (i, 0)),
            pl.BlockSpec((tm, SSD_D_INNER), lambda i: (i, 0)),
            pl.BlockSpec((tm, D_MODEL), lambda i: (i, COL_GA // D_MODEL)),
            pl.BlockSpec((tm, D_MODEL), lambda i: (i, COL_GS // D_MODEL)),
            pl.BlockSpec((tm, D_MODEL), lambda i: (i, 0)),
            full(D_MODEL, D_MODEL), full(SSD_D_INNER, D_MODEL), full(D_MODEL, D_MODEL),
        ],
        out_specs=pl.BlockSpec((tm, D_MODEL), lambda i: (i, 0)),
        out_shape=jax.ShapeDtypeStruct((t, D_MODEL), F32),
        compiler_params=_cparams(("parallel",)),
        name="merge",
    )(y_attn, y_ssd, proj, proj, x2d, w_a, w_b, w_o)


def _router_kernel(x_ref, nw_ref, rw_ref, rb_ref, h_ref, idx_ref, gate_ref, rank_ref, cnt_ref, carry_ref, *, tt):
    i = pl.program_id(0)

    @pl.when(i == 0)
    def _():
        carry_ref[...] = jnp.zeros_like(carry_ref)

    h = _rms(x_ref[...], nw_ref[...])
    h_ref[...] = h
    logits = jnp.dot(h, rw_ref[...], preferred_element_type=F32, precision=lax.Precision.HIGHEST) + rb_ref[...]
    lane = lax.broadcasted_iota(jnp.int32, (tt, LANES), 1)
    work = jnp.where(lane < N_EXPERTS, logits, -jnp.inf)

    vals, idxs, hots = [], [], []
    for _ in range(TOP_K):
        m = jnp.max(work, axis=-1, keepdims=True)
        idx = jnp.min(jnp.where(work == m, lane, LANES), axis=-1, keepdims=True)
        hot = lane == idx
        vals.append(m)
        idxs.append(idx)
        hots.append(hot)
        work = jnp.where(hot, -jnp.inf, work)

    exps = [jnp.exp(v - vals[0]) for v in vals]
    denom = exps[0] + exps[1] + exps[2] + exps[3]

    hot_sum = sum(jnp.where(hot, 1.0, 0.0) for hot in hots)
    r_i = lax.broadcasted_iota(jnp.int32, (tt, tt), 0)
    c_i = lax.broadcasted_iota(jnp.int32, (tt, tt), 1)
    strict = jnp.where(c_i < r_i, 1.0, 0.0).astype(BF16)
    prefix = jnp.dot(strict, hot_sum.astype(BF16), preferred_element_type=F32) + carry_ref[...]

    idx_out = jnp.zeros((tt, LANES), jnp.int32)
    rank_out = jnp.zeros((tt, LANES), jnp.int32)
    gate_out = jnp.zeros((tt, LANES), F32)
    for k in range(TOP_K):
        rank_k = jnp.sum(jnp.where(hots[k], prefix, 0.0), axis=-1, keepdims=True).astype(jnp.int32)
        sel = lane == k
        idx_out = jnp.where(sel, idxs[k], idx_out)
        rank_out = jnp.where(sel, rank_k, rank_out)
        gate_out = jnp.where(sel, exps[k] / denom, gate_out)
    idx_ref[...] = idx_out
    rank_ref[...] = rank_out
    gate_ref[...] = gate_out

    carry_ref[...] = carry_ref[...] + jnp.sum(hot_sum, axis=0, keepdims=True)
    cnt_ref[...] = carry_ref[...]


def _router(x1, norm_w, router_w_pad, router_b_pad, tt=512):
    t = x1.shape[0]
    tt = min(tt, t)
    kern = functools.partial(_router_kernel, tt=tt)
    row = pl.BlockSpec((tt, LANES), lambda i: (i, 0))
    return pl.pallas_call(
        kern,
        grid=(t // tt,),
        in_specs=[
            pl.BlockSpec((tt, D_MODEL), lambda i: (i, 0)),
            pl.BlockSpec((1, D_MODEL), lambda i: (0, 0)),
            pl.BlockSpec((D_MODEL, LANES), lambda i: (0, 0)),
            pl.BlockSpec((1, LANES), lambda i: (0, 0)),
        ],
        out_specs=[pl.BlockSpec((tt, D_MODEL), lambda i: (i, 0)), row, row, row,
                   pl.BlockSpec((1, LANES), lambda i: (0, 0))],
        out_shape=[
            jax.ShapeDtypeStruct((t, D_MODEL), F32),
            jax.ShapeDtypeStruct((t, LANES), jnp.int32),
            jax.ShapeDtypeStruct((t, LANES), F32),
            jax.ShapeDtypeStruct((t, LANES), jnp.int32),
            jax.ShapeDtypeStruct((1, LANES), F32),
        ],
        scratch_shapes=[pltpu.VMEM((1, LANES), F32)],
        compiler_params=_cparams(("arbitrary",)),
        name="router",
    )(x1, norm_w, router_w_pad, router_b_pad)


def _dispatch_kernel(pos_ref, h_ref, o_ref, sem, *, tt):
    def copy(t, k):
        return pltpu.make_async_copy(h_ref.at[pl.ds(t, 1), :], o_ref.at[pl.ds(pos_ref[t * TOP_K + k], 1), :], sem)

    def issue(t, c):
        for k in range(TOP_K):
            copy(t, k).start()
        return c

    lax.fori_loop(0, tt, issue, 0)

    for _ in range(TOP_K):
        pltpu.make_async_copy(h_ref, o_ref.at[pl.ds(0, tt), :], sem).wait()


def _dispatch(h2, pos_flat, tt=256):
    t = h2.shape[0]
    tt = min(tt, t)
    kern = functools.partial(_dispatch_kernel, tt=tt)
    return pl.pallas_call(
        kern,
        grid=(t // tt,),
        in_specs=[
            pl.BlockSpec((tt * TOP_K,), lambda i: (i,), memory_space=pltpu.SMEM),
            pl.BlockSpec((tt, D_MODEL), lambda i: (i, 0)),
        ],
        out_specs=pl.BlockSpec(memory_space=pl.ANY),
        out_shape=jax.ShapeDtypeStruct((t * TOP_K, D_MODEL), F32),
        scratch_shapes=[pltpu.SemaphoreType.DMA(())],
        compiler_params=_cparams(("arbitrary",)),
        name="dispatch",
    )(pos_flat, h2)


def _experts_kernel(tile_ref, exp_ref, first_ref, efirst_ref, valid_ref, start_ref,
                    x_ref, wu_ref, bu_ref, wd_ref, bd_ref, o_ref, wub_ref, wdb_ref, *, tm):
    w = pl.program_id(0)

    @pl.when(efirst_ref[w] == 1)
    def _():
        for c0 in range(0, 2 * D_EXPERT, 512):
            wub_ref[:, c0:c0 + 512] = wu_ref[0, :, c0:c0 + 512].astype(BF16)
        for c0 in range(0, D_MODEL, 512):
            wdb_ref[:, c0:c0 + 512] = wd_ref[0, :, c0:c0 + 512].astype(BF16)

    @pl.when(valid_ref[w] == 1)
    def _():
        e = exp_ref[w]
        gu = jnp.dot(x_ref[...].astype(BF16), wub_ref[...], preferred_element_type=F32) + bu_ref[0]
        g = jnp.minimum(gu[:, :D_EXPERT], SWIGLU_LIMIT)
        u = jnp.clip(gu[:, D_EXPERT:], -SWIGLU_LIMIT, SWIGLU_LIMIT)
        act = (u + 1.0) * (g * _sigmoid(SWIGLU_ALPHA * g))
        out = jnp.dot(act.astype(BF16), wdb_ref[...], preferred_element_type=F32) + bd_ref[0]
        rows = tile_ref[w] * tm + lax.broadcasted_iota(jnp.int32, (tm, 1), 0)
        mine = (rows >= start_ref[e]) & (rows < start_ref[e + 1])

        @pl.when(first_ref[w] == 1)
        def _():
            _rows_to_tiles(o_ref, jnp.where(mine, out, 0.0), tm)

        @pl.when(first_ref[w] == 0)
        def _():
            _rows_to_tiles(o_ref, jnp.where(mine, out, _tiles_to_rows(o_ref, tm)), tm)


def _experts(x_sorted, item_tile, item_exp, item_first, item_efirst, item_valid, starts, w_up, b_up, w_down, b_down, tm):
    n = x_sorted.shape[0]
    n_items = item_tile.shape[0]
    kern = functools.partial(_experts_kernel, tm=tm)
    grid_spec = pltpu.PrefetchScalarGridSpec(
        num_scalar_prefetch=6,
        grid=(n_items,),
        in_specs=[
            pl.BlockSpec((tm, D_MODEL), lambda w, tl, ex, *_: (tl[w], 0)),
            pl.BlockSpec((1, D_MODEL, 2 * D_EXPERT), lambda w, tl, ex, *_: (ex[w], 0, 0)),
            pl.BlockSpec((1, 1, 2 * D_EXPERT), lambda w, tl, ex, *_: (ex[w], 0, 0)),
            pl.BlockSpec((1, D_EXPERT, D_MODEL), lambda w, tl, ex, *_: (ex[w], 0, 0)),
            pl.BlockSpec((1, 1, D_MODEL), lambda w, tl, ex, *_: (ex[w], 0, 0)),
        ],
        out_specs=pl.BlockSpec((tm * ROW_CHUNKS, LANES), lambda w, tl, ex, *_: (tl[w], 0)),
        scratch_shapes=[pltpu.VMEM((D_MODEL, 2 * D_EXPERT), BF16), pltpu.VMEM((D_EXPERT, D_MODEL), BF16)],
    )
    return pl.pallas_call(
        kern,
        grid_spec=grid_spec,
        out_shape=jax.ShapeDtypeStruct((n * ROW_CHUNKS, LANES), F32),
        compiler_params=_cparams(("arbitrary",)),
        name="experts",
    )(item_tile, item_exp, item_first, item_efirst, item_valid, starts, x_sorted, w_up, b_up, w_down, b_down)


def _combine_kernel(pos_ref, posn_ref, gate_ref, x_ref, p_ref, pnw_ref, wpg_ref, wpp_ref, fnw_ref, ys_ref, o_ref,
                    rows_ref, sem, *, tt):
    i = pl.program_id(0)
    n = pl.num_programs(0)
    slot = i % 2

    def gather(idx_ref, dst_slot):
        def issue(t, c):
            for k in range(TOP_K):
                src = pl.multiple_of(idx_ref[t * TOP_K + k] * ROW_CHUNKS, ROW_CHUNKS)
                dst = pl.multiple_of((k * tt + t) * ROW_CHUNKS, ROW_CHUNKS)
                pltpu.make_async_copy(ys_ref.at[pl.ds(src, ROW_CHUNKS), :],
                                      rows_ref.at[dst_slot, pl.ds(dst, ROW_CHUNKS), :], sem.at[dst_slot]).start()
            return c

        lax.fori_loop(0, tt, issue, 0)

    @pl.when(i == 0)
    def _():
        gather(pos_ref, 0)

    @pl.when(i + 1 < n)
    def _():
        gather(posn_ref, 1 - slot)

    pltpu.make_async_copy(ys_ref.at[pl.ds(0, TOP_K * tt * ROW_CHUNKS), :], rows_ref.at[slot], sem.at[slot]).wait()

    gate = gate_ref[...]
    x = x_ref[...]
    for k in range(TOP_K):
        x = x + gate[:, k:k + 1] * _tiles_to_rows(rows_ref.at[slot], tt, base=k * tt * ROW_CHUNKS)

    hg = _rms(x, pnw_ref[...]).astype(BF16)
    pg = _sigmoid(jnp.dot(hg, wpg_ref[...], preferred_element_type=F32))
    x = x + pg * jnp.dot(p_ref[...].astype(BF16), wpp_ref[...], preferred_element_type=F32)
    o_ref[...] = _rms(x, fnw_ref[...])


def _combine(pos_flat, gate, x1, p2d, ple_nw, w_pg, w_pp, final_nw, y_sorted, tt=256):
    t = x1.shape[0]
    tt = min(tt, t)
    kern = functools.partial(_combine_kernel, tt=tt)
    vec = pl.BlockSpec((1, D_MODEL), lambda i: (0, 0))
    nt = t // tt
    return pl.pallas_call(
        kern,
        grid=(nt,),
        in_specs=[
            pl.BlockSpec((tt * TOP_K,), lambda i: (i,), memory_space=pltpu.SMEM),
            pl.BlockSpec((tt * TOP_K,), lambda i: (jnp.minimum(i + 1, nt - 1),), memory_space=pltpu.SMEM),
            pl.BlockSpec((tt, LANES), lambda i: (i, 0)),
            pl.BlockSpec((tt, D_MODEL), lambda i: (i, 0)),
            pl.BlockSpec((tt, PLE_DIM), lambda i: (i, 0)),
            vec,
            pl.BlockSpec((D_MODEL, D_MODEL), lambda i: (0, 0)),
            pl.BlockSpec((PLE_DIM, D_MODEL), lambda i: (0, 0)),
            vec,
            pl.BlockSpec(memory_space=pl.ANY),
        ],
        out_specs=pl.BlockSpec((tt, D_MODEL), lambda i: (i, 0)),
        out_shape=jax.ShapeDtypeStruct((t, D_MODEL), F32),
        scratch_shapes=[pltpu.VMEM((2, TOP_K * tt * ROW_CHUNKS, LANES), F32), pltpu.SemaphoreType.DMA((2,))],
        compiler_params=_cparams(("arbitrary",)),
        name="combine",
    )(pos_flat, pos_flat, gate, x1, p2d, ple_nw, w_pg, w_pp, final_nw, y_sorted)


def _routing_plan(counts, idx, rank, tm, n_rows):
    counts = counts.astype(jnp.int32)
    ends = jnp.cumsum(counts)
    starts = ends - counts
    pos = (starts[idx] + rank).reshape(-1)
    n_tiles = n_rows // tm
    n_items = n_tiles + N_EXPERTS - 1
    first_tile = starts // tm
    last_tile = jnp.maximum(ends - 1, 0) // tm
    items_e = jnp.where(counts > 0, last_tile - first_tile + 1, 0)
    item_end = jnp.cumsum(items_e)
    item_start = item_end - items_e
    total = item_end[-1]
    w = jnp.arange(n_items, dtype=jnp.int32)
    e_w = jnp.minimum(jnp.sum(w[:, None] >= item_end[None, :], axis=1).astype(jnp.int32), N_EXPERTS - 1)
    tile_w = first_tile[e_w] + (w - item_start[e_w])
    valid = w < total
    last = jnp.maximum(total - 1, 0)
    e_w = jnp.where(valid, e_w, e_w[last])
    tile_w = jnp.where(valid, tile_w, tile_w[last])
    prev_tile = jnp.concatenate([jnp.full((1,), -1, jnp.int32), tile_w[:-1]])
    first = (tile_w != prev_tile) & valid
    prev_e = jnp.concatenate([jnp.full((1,), -1, jnp.int32), e_w[:-1]])
    efirst = (e_w != prev_e) & valid
    starts_ext = jnp.concatenate([starts, ends[-1:]]).astype(jnp.int32)
    return (pos.astype(jnp.int32), tile_w.astype(jnp.int32), e_w, first.astype(jnp.int32), efirst.astype(jnp.int32),
            valid.astype(jnp.int32), starts_ext)


def kernel(x, p, mix_norm_w, w_in, lambda_q1, lambda_k1, lambda_q2, lambda_k2, da_head_norm_w, w_attn_branch, conv_w, conv_b, dt_bias, a_log, d_skip, ssd_norm_w, w_ssd_branch, w_out, moe_norm_w, router_w, router_b, w_up, b_up, w_down, b_down, ple_norm_w, w_ple_gate, w_ple_proj, final_norm_w):
    batch, seq, _ = x.shape
    t = batch * seq
    depth = w_in.shape[0]
    assert depth == 1, "the final RMSNorm is fused into the layer's last kernel"
    x2d = x.reshape(t, D_MODEL)

    log2e = math.log2(math.e)
    slopes = jnp.asarray([log2e * 2.0 ** (-8.0 * (h + 1) / DA_HEADS) for h in range(DA_HEADS)], F32)
    q_scale = log2e * DA_QK_DIM ** -0.5
    head_ids = jnp.arange(SSD_D_INNER, dtype=jnp.int32) // SSD_HEAD_DIM
    expand = (jnp.arange(LANES, dtype=jnp.int32)[:, None] == head_ids[None, :]).astype(BF16)

    def pad_lanes(v, fill=0.0):
        return jnp.pad(v.reshape(1, -1), ((0, 0), (0, LANES - v.shape[-1])), constant_values=fill)

    for i in range(depth):
        wi = w_in[i]
        w_main = jnp.concatenate([
            wi[:, _R_Z:_R_Z + SSD_D_INNER], wi[:, _R_Q:_R_Q + 1024] * q_scale, wi[:, _R_K:_R_K + 1024],
            wi[:, _R_GA:_R_GA + 1024], wi[:, _R_GS:_R_GS + 1024], wi[:, _R_XBC:_R_XBC + SSD_CONV_DIM],
            wi[:, _R_V:_R_V + 1024]], axis=1).astype(BF16)
        w_dt = jnp.pad(wi[:, _R_DT:_R_DT + SSD_HEADS], ((0, 0), (0, LANES - SSD_HEADS))).astype(BF16)

        proj, dt_raw, v_t = _in_proj(x2d, mix_norm_w[i].reshape(1, -1), w_main, w_dt)

        lambda_init = 0.8 - 0.6 * math.exp(-0.3 * i)
        lam = (jnp.exp(jnp.sum(lambda_q1[i] * lambda_k1[i])) - jnp.exp(jnp.sum(lambda_q2[i] * lambda_k2[i]))
               + lambda_init).reshape(1).astype(F32)
        y_attn = _diff_attn(proj, v_t, slopes, lam, da_head_norm_w[i].reshape(1, -1), batch, seq, lambda_init)

        y_ssd = _ssd(proj, dt_raw, conv_w[i], conv_b[i].reshape(1, -1), pad_lanes(dt_bias[i]), pad_lanes(a_log[i]),
                     jnp.repeat(d_skip[i], SSD_HEAD_DIM).reshape(1, -1), ssd_norm_w[i].reshape(1, -1), expand,
                     batch, seq)

        x1 = _merge(y_attn, y_ssd, proj, x2d, w_attn_branch[i].astype(BF16), w_ssd_branch[i].astype(BF16),
                    w_out[i].astype(BF16))

        rw = jnp.pad(router_w[i], ((0, 0), (0, LANES - N_EXPERTS)))
        h2, idx, gate, rank, counts = _router(x1, moe_norm_w[i].reshape(1, -1), rw, pad_lanes(router_b[i]))

        tm = 512
        n_rows = t * TOP_K
        pos, item_tile, item_exp, item_first, item_efirst, item_valid, starts = _routing_plan(
            counts[0, :N_EXPERTS], idx[:, :TOP_K], rank[:, :TOP_K], tm, n_rows)

        x_sorted = _dispatch(h2, pos)
        y_sorted = _experts(x_sorted, item_tile, item_exp, item_first, item_efirst, item_valid, starts,
                            w_up[i], b_up[i].reshape(N_EXPERTS, 1, -1),
                            w_down[i], b_down[i].reshape(N_EXPERTS, 1, -1), tm)

        x2d = _combine(pos, gate, x1, p[i].reshape(t, PLE_DIM), ple_norm_w[i].reshape(1, -1),
                       w_ple_gate[i].astype(BF16), w_ple_proj[i].astype(BF16), final_norm_w.reshape(1, -1), y_sorted)
    return x2d.reshape(batch, seq, D_MODEL)
```

```python
import functools
import math

import jax
import jax.numpy as jnp
from jax import lax
from jax.experimental import pallas as pl
from jax.experimental.pallas import tpu as pltpu

F32 = jnp.float32
BF16 = jnp.bfloat16

D_MODEL = 1024
CHUNK = 64
PLE_DIM = 256
RMS_EPS = 1e-6

DA_HEADS = 8
DA_QK_DIM = 64
DA_V_DIM = 128

SSD_D_INNER = 2048
SSD_HEAD_DIM = 64
SSD_HEADS = 32
SSD_GROUPS = 4
SSD_STATE = 128
SSD_CONV = 4
SSD_CONV_DIM = 3072

N_EXPERTS = 32
TOP_K = 4
D_EXPERT = 1024
SWIGLU_LIMIT = 7.0
SWIGLU_ALPHA = 1.702

LANES = 128

COL_Z = 0
COL_Q = 2048
COL_K = 3072
COL_GA = 4096
COL_GS = 5120
COL_XBC = 6144
PROJ_W = 9216

ATTN_TILE = 256
ATTN_Q_TILE = 512
SUM_ROWS = 16

_R_Q, _R_K, _R_V, _R_Z, _R_XBC, _R_DT, _R_GA, _R_GS = 0, 1024, 2048, 3072, 5120, 8192, 8224, 9248

VMEM_LIMIT = 56 * 1024 * 1024


def _cparams(sem):
    return pltpu.CompilerParams(dimension_semantics=sem, vmem_limit_bytes=VMEM_LIMIT)


def _rms(x, w):
    return x * lax.rsqrt(jnp.mean(x * x, axis=-1, keepdims=True) + RMS_EPS) * w


ROW_CHUNKS = D_MODEL // LANES


def _rows_to_tiles(ref, val, n):
    for c in range(ROW_CHUNKS):
        ref[pl.ds(c, n, stride=ROW_CHUNKS), :] = val[:, c * LANES:(c + 1) * LANES]


def _tiles_to_rows(ref, n, base=0):
    return jnp.concatenate([ref[pl.ds(base + c, n, stride=ROW_CHUNKS), :] for c in range(ROW_CHUNKS)], axis=1)


def _sigmoid(x):
    return 1.0 / (1.0 + jnp.exp2(x * (-math.log2(math.e))))


def _inproj_kernel(x_ref, nw_ref, w_ref, wdt_ref, o_ref, dt_ref, vt_ref, h_ref, *, tm, n_proj):
    j = pl.program_id(1)

    @pl.when(j == 0)
    def _():
        hb = _rms(x_ref[...], nw_ref[...]).astype(BF16)
        h_ref[...] = hb
        dt_ref[...] = jnp.dot(hb, wdt_ref[...], preferred_element_type=F32)

    acc = jnp.dot(h_ref[...], w_ref[...], preferred_element_type=F32)

    @pl.when(j < n_proj)
    def _():
        o_ref[...] = acc.astype(BF16)

    @pl.when(j == n_proj)
    def _():
        vt = acc.T
        for c in range(tm // ATTN_TILE):
            vt_ref[c] = vt[:, c * ATTN_TILE:(c + 1) * ATTN_TILE].astype(BF16)


def _in_proj(x2d, norm_w, w_main, w_dt, tm=2048):
    t = x2d.shape[0]
    tm = min(tm, t)
    tn = DA_HEADS * DA_V_DIM
    n_proj = PROJ_W // tn
    return pl.pallas_call(
        functools.partial(_inproj_kernel, tm=tm, n_proj=n_proj),
        grid=(t // tm, n_proj + 1),
        in_specs=[
            pl.BlockSpec((tm, D_MODEL), lambda i, j: (i, 0)),
            pl.BlockSpec((1, D_MODEL), lambda i, j: (0, 0)),
            pl.BlockSpec((D_MODEL, tn), lambda i, j: (0, j)),
            pl.BlockSpec((D_MODEL, LANES), lambda i, j: (0, 0)),
        ],
        out_specs=[
            pl.BlockSpec((tm, tn), lambda i, j: (i, jnp.minimum(j, n_proj - 1))),
            pl.BlockSpec((tm, LANES), lambda i, j: (i, 0)),
            pl.BlockSpec((tm // ATTN_TILE, tn, ATTN_TILE), lambda i, j: (i, 0, 0)),
        ],
        out_shape=[
            jax.ShapeDtypeStruct((t, PROJ_W), BF16),
            jax.ShapeDtypeStruct((t, LANES), F32),
            jax.ShapeDtypeStruct((t // ATTN_TILE, tn, ATTN_TILE), BF16),
        ],
        scratch_shapes=[pltpu.VMEM((tm, D_MODEL), BF16)],
        compiler_params=_cparams(("parallel", "arbitrary")),
        name="in_proj",
    )(x2d, norm_w, w_main, w_dt)


def _attn_kernel(slopes_ref, lam_ref, q_ref, k_ref, vt_ref, nw_ref, o_ref,
                 bias_ref, qc_ref, s_ref, p_ref, al_ref, m_ref, a_ref, *, tq, tk, out_scale):
    h = pl.program_id(1)
    i = pl.program_id(2)
    slope = slopes_ref[h]
    lam = lam_ref[0]
    r = tq // tk
    n_tiles = r * (i + 1)

    @pl.when(i == 0)
    def _():
        kk = lax.broadcasted_iota(jnp.int32, (tk, tq), 0)
        qq = lax.broadcasted_iota(jnp.int32, (tk, tq), 1)
        bias_ref[0] = -slope * (qq - kk).astype(F32)
        for d in range(r):
            kd = kk + d * tk
            bias_ref[1 + d] = jnp.where((kd // CHUNK) <= (qq // CHUNK), -slope * jnp.abs(qq - kd).astype(F32), -jnp.inf)

    qt = q_ref[...].astype(F32).T.astype(BF16)
    dim = lax.broadcasted_iota(jnp.int32, qt.shape, 0)
    zero = jnp.zeros_like(qt)
    qc_ref[...] = jnp.concatenate([jnp.where(dim < DA_QK_DIM, qt, zero), jnp.where(dim >= DA_QK_DIM, qt, zero)], axis=1)

    m_ref[...] = jnp.full(m_ref.shape, -jnp.inf, F32)
    a_ref[...] = jnp.zeros(a_ref.shape, F32)

    def scores(j):
        k = k_ref[pl.ds(pl.multiple_of(j * tk, tk), tk), :]
        return jnp.dot(k, qc_ref[...], preferred_element_type=F32)

    def softmax_step(sel, cj):
        for g in range(2 * tq // LANES):
            cols = slice(g * LANES, (g + 1) * LANES)
            bcols = slice((g * LANES) % tq, (g * LANES) % tq + LANES)
            s = s_ref[:, cols] + bias_ref[sel, :, bcols]
            m_prev = m_ref[:, cols]
            m_cur = jnp.maximum(m_prev, jnp.max(s, axis=0, keepdims=True) + cj)
            alpha = jnp.exp2(m_prev - m_cur)
            p = jnp.exp2(s + (cj - m_cur))
            m_ref[:, cols] = m_cur
            al_ref[:, cols] = alpha
            p_ref[:, cols] = p.astype(BF16)

    def pv_step(j):
        vt = jnp.concatenate([vt_ref[j], jnp.ones((SUM_ROWS, tk), BF16)], axis=0)
        for g in range(2 * tq // 256):
            cols = slice(g * 256, (g + 1) * 256)
            a_ref[:, cols] = al_ref[:, cols] * a_ref[:, cols] + jnp.dot(vt, p_ref[:, cols], preferred_element_type=F32)

    def softmax_tile(j):
        past = j < r * i
        cj = jnp.where(past, -slope * (i * tq - j * tk).astype(F32), 0.0)
        softmax_step(jnp.maximum(j - r * i + 1, 0), cj)

    s_ref[...] = scores(0)
    s_next = scores(1)
    softmax_tile(0)
    s_ref[...] = s_next

    def body(j, carry):
        s_next = scores(j + 1)
        pv_step(j - 1)
        softmax_tile(j)
        s_ref[...] = s_next
        return carry

    lax.fori_loop(1, n_tiles - 1, body, 0)

    pv_step(n_tiles - 2)
    softmax_tile(n_tiles - 1)
    pv_step(n_tiles - 1)

    on = a_ref[0:DA_V_DIM, :] / a_ref[DA_V_DIM:DA_V_DIM + 1, :]
    ot = on[:, :tq] - lam * on[:, tq:]
    ot = ot * lax.rsqrt(jnp.mean(ot * ot, axis=0, keepdims=True) + RMS_EPS)
    o_ref[...] = (ot.T * nw_ref[...] * out_scale).astype(BF16)


def _diff_attn(proj, v_t, slopes, lam, head_norm_w, batch, seq, lambda_init):
    tk = ATTN_TILE
    tq = min(ATTN_Q_TILE, seq)
    nq = seq // tq
    nk = seq // tk
    t = batch * seq
    qb, kb = COL_Q // LANES, COL_K // LANES
    assert tq % tk == 0 and tq >= 2 * tk, "the peeled first/last key tiles need at least two tiles per step"
    kern = functools.partial(_attn_kernel, tq=tq, tk=tk, out_scale=1.0 - lambda_init)
    smem = pl.BlockSpec(memory_space=pltpu.SMEM)
    return pl.pallas_call(
        kern,
        grid=(batch, DA_HEADS, nq),
        in_specs=[
            smem, smem,
            pl.BlockSpec((tq, LANES), lambda b, h, i: (b * nq + i, qb + h)),
            pl.BlockSpec((seq, LANES), lambda b, h, i: (b, kb + h)),
            pl.BlockSpec((nk, DA_V_DIM, tk), lambda b, h, i: (b, h, 0)),
            pl.BlockSpec((1, LANES), lambda b, h, i: (0, 0)),
        ],
        out_specs=pl.BlockSpec((tq, LANES), lambda b, h, i: (b * nq + i, h)),
        out_shape=jax.ShapeDtypeStruct((t, DA_HEADS * DA_V_DIM), BF16),
        scratch_shapes=[
            pltpu.VMEM((1 + tq // tk, tk, tq), F32),
            pltpu.VMEM((LANES, 2 * tq), BF16),
            pltpu.VMEM((tk, 2 * tq), F32),
            pltpu.VMEM((tk, 2 * tq), BF16),
            pltpu.VMEM((1, 2 * tq), F32),
            pltpu.VMEM((1, 2 * tq), F32),
            pltpu.VMEM((DA_V_DIM + SUM_ROWS, 2 * tq), F32),
        ],
        compiler_params=_cparams(("parallel", "parallel", "arbitrary")),
        name="diff_attn",
    )(slopes, lam, proj, proj, v_t, head_norm_w)


def _split2(x):
    hi = x.astype(BF16)
    lo = (x - hi.astype(F32)).astype(BF16)
    return hi, lo


def _split3(x):
    hi = x.astype(BF16)
    r = x - hi.astype(F32)
    mid = r.astype(BF16)
    lo = (r - mid.astype(F32)).astype(BF16)
    return hi, mid, lo


def _ssd_kernel(xbc_ref, z_ref, dt_ref, cw_ref, cb_ref, dtb_ref, alog_ref, dsk_ref, nw_ref, e_ref,
                o_ref, ubuf, shift_ref, xs_scr, bc_scr, acum_scr, acb, dfs_scr, xdt_scr, xw_scr, y_scr, state, *, tr):
    i = pl.program_id(1)
    nch = tr // CHUNK
    gw = SSD_D_INNER // SSD_GROUPS
    pad = 8

    head = 16

    @pl.when(i == 0)
    def _():
        ubuf[0:pad, :] = jnp.zeros((pad, SSD_CONV_DIM), F32)
        state[...] = jnp.zeros_like(state)
        r_s = lax.broadcasted_iota(jnp.int32, (tr, tr), 0)
        c_s = lax.broadcasted_iota(jnp.int32, (tr, tr), 1)
        for d in range(1, SSD_CONV):
            shift_ref[(d - 1) * tr:d * tr, :] = jnp.where(c_s == r_s - d, 1.0, 0.0).astype(BF16)

    ubuf[pad:pad + head, :] = xbc_ref[0:head, :].astype(F32)
    cs = 512
    for c0 in range(0, SSD_CONV_DIM, cs):
        ub = xbc_ref[:, c0:c0 + cs]
        sh = jnp.dot(shift_ref[...], ub, preferred_element_type=F32)
        acc = cb_ref[:, c0:c0 + cs] + cw_ref[SSD_CONV - 1:SSD_CONV, c0:c0 + cs] * ub.astype(F32)
        for d in range(1, SSD_CONV):
            acc = acc + cw_ref[SSD_CONV - 1 - d:SSD_CONV - d, c0:c0 + cs] * sh[(d - 1) * tr:d * tr, :]
        u = acc * _sigmoid(acc)
        acc_h = jnp.broadcast_to(cb_ref[:, c0:c0 + cs], (head, cs))
        for k in range(SSD_CONV):
            off = pad - (SSD_CONV - 1) + k
            acc_h = acc_h + cw_ref[k:k + 1, c0:c0 + cs] * ubuf[off:off + head, c0:c0 + cs]
        u_h = acc_h * _sigmoid(acc_h)
        if c0 < SSD_D_INNER:
            xs_scr[:, c0:c0 + cs] = u
            xs_scr[0:head, c0:c0 + cs] = u_h
        else:
            bc_scr[:, c0 - SSD_D_INNER:c0 - SSD_D_INNER + cs] = u.astype(BF16)
            bc_scr[0:head, c0 - SSD_D_INNER:c0 - SSD_D_INNER + cs] = u_h.astype(BF16)
    ubuf[0:pad, :] = xbc_ref[tr - 2 * pad:tr, :].astype(F32)[pad:, :]

    xdt_in = dt_ref[...] + dtb_ref[...]
    dtv = jnp.maximum(xdt_in, 0.0) + jnp.log1p(jnp.exp(-jnp.abs(xdt_in)))
    da = dtv * (-jnp.exp(alog_ref[...]))
    r_i = lax.broadcasted_iota(jnp.int32, (tr, tr), 0)
    c_i = lax.broadcasted_iota(jnp.int32, (tr, tr), 1)
    tril = jnp.where((c_i <= r_i) & ((c_i // CHUNK) == (r_i // CHUNK)), 1.0, 0.0).astype(BF16)
    acum = sum(jnp.dot(tril, piece, preferred_element_type=F32) for piece in _split3(da))
    acum_scr[...] = acum

    a_hi, a_lo = _split2(acum)
    d_hi, d_lo = _split2(dtv)
    for c0 in range(0, SSD_D_INNER, cs):
        e = e_ref[:, c0:c0 + cs]
        ab = jnp.dot(a_hi, e, preferred_element_type=F32) + jnp.dot(a_lo, e, preferred_element_type=F32)
        db = jnp.dot(d_hi, e, preferred_element_type=F32) + jnp.dot(d_lo, e, preferred_element_type=F32)
        acb[:, c0:c0 + cs] = ab
        dfs_scr[:, c0:c0 + cs] = jnp.exp(ab)
        xdt = xs_scr[:, c0:c0 + cs] * db
        xdt_scr[:, c0:c0 + cs] = xdt.astype(BF16)
        for c in range(nch):
            r0 = c * CHUNK
            last = ab[r0 + CHUNK - 1:r0 + CHUNK, :]
            xw_scr[r0:r0 + CHUNK, c0:c0 + cs] = (xdt[r0:r0 + CHUNK, :] * jnp.exp(last - ab[r0:r0 + CHUNK, :])).astype(BF16)

    lane2 = lax.broadcasted_iota(jnp.int32, (CHUNK, LANES), 1)
    row2 = lax.broadcasted_iota(jnp.int32, (CHUNK, LANES), 0)
    causal2 = row2 >= (lane2 % SSD_HEAD_DIM)
    lo_half = lane2 < SSD_HEAD_DIM
    zpad = jnp.zeros((LANES - CHUNK, LANES), F32)

    def chunk_body(c, carry):
        r0 = pl.multiple_of(c * CHUNK, CHUNK)
        rows = pl.ds(r0, CHUNK)
        a_c = acum_scr[rows, :]
        a_t = jnp.concatenate([a_c, zpad], axis=0).T
        a_t_r = pltpu.roll(a_t, SSD_HEAD_DIM, 1)
        last_row = acb[pl.ds(r0 + CHUNK - 1, 1), :]
        for g in range(SSD_GROUPS):
            bg = bc_scr[rows, g * SSD_STATE:(g + 1) * SSD_STATE]
            cg = bc_scr[rows, SSD_GROUPS * SSD_STATE + g * SSD_STATE:SSD_GROUPS * SSD_STATE + (g + 1) * SSD_STATE]
            cbt = lax.dot_general(cg, jnp.concatenate([bg, bg], axis=0), (((1,), (1,)), ((), ())),
                                  preferred_element_type=F32)
            g0 = g * gw
            st = state[g]
            yoff = jnp.dot(cg, st.astype(BF16), preferred_element_type=F32) * dfs_scr[rows, g0:g0 + gw]
            for jj in range(gw // LANES):
                pidx = g * (gw // LANES) + jj
                l0 = pidx * LANES
                arow = a_t[2 * pidx:2 * pidx + 1, :] + a_t_r[2 * pidx + 1:2 * pidx + 2, :]
                seg = acb[rows, l0:l0 + LANES] - arow
                decay = jnp.exp(jnp.where(causal2, seg, -jnp.inf))
                mp = (cbt * decay).astype(BF16)
                xd = xdt_scr[rows, l0:l0 + LANES]
                zb = jnp.zeros_like(xd)
                bd = jnp.concatenate([jnp.where(lo_half, xd, zb), jnp.where(lo_half, zb, xd)], axis=0)
                yd = jnp.dot(mp, bd, preferred_element_type=F32)
                y_scr[rows, l0:l0 + LANES] = yd + yoff[:, jj * LANES:(jj + 1) * LANES]
            cd = jnp.exp(last_row[:, g0:g0 + gw])
            upd = lax.dot_general(bg, xw_scr[rows, g0:g0 + gw], (((0,), (0,)), ((), ())),
                                  preferred_element_type=F32)
            state[g] = st * cd + upd
        return carry

    lax.fori_loop(0, nch, chunk_body, 0, unroll=True)

    for g in range(SSD_GROUPS):
        g0 = g * gw
        y = y_scr[:, g0:g0 + gw] + dsk_ref[:, g0:g0 + gw] * xs_scr[:, g0:g0 + gw]
        zz = z_ref[:, g0:g0 + gw].astype(F32)
        y = y * (zz * _sigmoid(zz))
        o_ref[:, g0:g0 + gw] = _rms(y, nw_ref[:, g0:g0 + gw]).astype(BF16)


def _ssd(proj, dt_raw, conv_w, conv_b, dt_bias, a_log, d_skip_b, norm_w, expand, batch, seq, tr=256):
    t = batch * seq
    nr = seq // tr
    kern = functools.partial(_ssd_kernel, tr=tr)
    const = lambda shape: pl.BlockSpec(shape, lambda b, i: (0, 0))
    return pl.pallas_call(
        kern,
        grid=(batch, nr),
        in_specs=[
            pl.BlockSpec((tr, SSD_CONV_DIM), lambda b, i: (b * nr + i, COL_XBC // SSD_CONV_DIM)),
            pl.BlockSpec((tr, SSD_D_INNER), lambda b, i: (b * nr + i, COL_Z // SSD_D_INNER)),
            pl.BlockSpec((tr, LANES), lambda b, i: (b * nr + i, 0)),
            const((SSD_CONV, SSD_CONV_DIM)),
            const((1, SSD_CONV_DIM)),
            const((1, LANES)),
            const((1, LANES)),
            const((1, SSD_D_INNER)),
            const((1, SSD_D_INNER)),
            const((LANES, SSD_D_INNER)),
        ],
        out_specs=pl.BlockSpec((tr, SSD_D_INNER), lambda b, i: (b * nr + i, 0)),
        out_shape=jax.ShapeDtypeStruct((t, SSD_D_INNER), BF16),
        scratch_shapes=[
            pltpu.VMEM((8 + 16, SSD_CONV_DIM), F32),
            pltpu.VMEM(((SSD_CONV - 1) * tr, tr), BF16),
            pltpu.VMEM((tr, SSD_D_INNER), F32),
            pltpu.VMEM((tr, 2 * SSD_GROUPS * SSD_STATE), BF16),
            pltpu.VMEM((tr, LANES), F32),
            pltpu.VMEM((tr, SSD_D_INNER), F32),
            pltpu.VMEM((tr, SSD_D_INNER), F32),
            pltpu.VMEM((tr, SSD_D_INNER), BF16),
            pltpu.VMEM((tr, SSD_D_INNER), BF16),
            pltpu.VMEM((tr, SSD_D_INNER), F32),
            pltpu.VMEM((SSD_GROUPS, SSD_STATE, SSD_D_INNER // SSD_GROUPS), F32),
        ],
        compiler_params=_cparams(("parallel", "arbitrary")),
        name="ssd",
    )(proj, proj, dt_raw, conv_w, conv_b, dt_bias, a_log, d_skip_b, norm_w, expand)


def _merge_kernel(ya_ref, ys_ref, ga_ref, gs_ref, x_ref, wa_ref, wb_ref, wo_ref, o_ref):
    a = jnp.dot(ya_ref[...], wa_ref[...], preferred_element_type=F32)
    s = jnp.dot(ys_ref[...], wb_ref[...], preferred_element_type=F32)
    merged = _sigmoid(ga_ref[...].astype(F32)) * a + _sigmoid(gs_ref[...].astype(F32)) * s
    o_ref[...] = x_ref[...] + jnp.dot(merged.astype(BF16), wo_ref[...], preferred_element_type=F32)


def _merge(y_attn, y_ssd, proj, x2d, w_a, w_b, w_o, tm=512):
    t = x2d.shape[0]
    tm = min(tm, t)
    full = lambda r, c: pl.BlockSpec((r, c), lambda i: (0, 0))
    return pl.pallas_call(
        _merge_kernel,
        grid=(t // tm,),
        in_specs=[
            pl.BlockSpec((tm, D_MODEL), lambda i: (i, 0)),
            pl.BlockSpec((tm, SSD_D_INNER), lambda i: (i, 0)),
            pl.BlockSpec((tm, D_MODEL), lambda i: (i, COL_GA // D_MODEL)),
            pl.BlockSpec((tm, D_MODEL), lambda i: (i, COL_GS // D_MODEL)),
            pl.BlockSpec((tm, D_MODEL), lambda i: (i, 0)),
            full(D_MODEL, D_MODEL), full(SSD_D_INNER, D_MODEL), full(D_MODEL, D_MODEL),
        ],
        out_specs=pl.BlockSpec((tm, D_MODEL), lambda i: (i, 0)),
        out_shape=jax.ShapeDtypeStruct((t, D_MODEL), F32),
        compiler_params=_cparams(("parallel",)),
        name="merge",
    )(y_attn, y_ssd, proj, proj, x2d, w_a, w_b, w_o)


def _router_kernel(x_ref, nw_ref, rw_ref, rb_ref, h_ref, idx_ref, gate_ref, rank_ref, cnt_ref, carry_ref, *, tt):
    i = pl.program_id(0)

    @pl.when(i == 0)
    def _():
        carry_ref[...] = jnp.zeros_like(carry_ref)

    h = _rms(x_ref[...], nw_ref[...])
    h_ref[...] = pltpu.einshape("r(cl)->(rc)l", h, c=ROW_CHUNKS)
    logits = jnp.dot(h, rw_ref[...], preferred_element_type=F32, precision=lax.Precision.HIGHEST) + rb_ref[...]
    lane = lax.broadcasted_iota(jnp.int32, (tt, LANES), 1)
    work = jnp.where(lane < N_EXPERTS, logits, -jnp.inf)

    vals, idxs, hots = [], [], []
    for _ in range(TOP_K):
        m = jnp.max(work, axis=-1, keepdims=True)
        idx = jnp.min(jnp.where(work == m, lane, LANES), axis=-1, keepdims=True)
        hot = lane == idx
        vals.append(m)
        idxs.append(idx)
        hots.append(hot)
        work = jnp.where(hot, -jnp.inf, work)

    exps = [jnp.exp(v - vals[0]) for v in vals]
    denom = exps[0] + exps[1] + exps[2] + exps[3]

    hot_sum = sum(jnp.where(hot, 1.0, 0.0) for hot in hots)
    r_i = lax.broadcasted_iota(jnp.int32, (tt, tt), 0)
    c_i = lax.broadcasted_iota(jnp.int32, (tt, tt), 1)
    strict = jnp.where(c_i < r_i, 1.0, 0.0).astype(BF16)
    prefix = jnp.dot(strict, hot_sum.astype(BF16), preferred_element_type=F32) + carry_ref[...]

    idx_out = jnp.zeros((tt, LANES), jnp.int32)
    rank_out = jnp.zeros((tt, LANES), jnp.int32)
    gate_out = jnp.zeros((tt, LANES), F32)
    for k in range(TOP_K):
        rank_k = jnp.sum(jnp.where(hots[k], prefix, 0.0), axis=-1, keepdims=True).astype(jnp.int32)
        sel = lane == k
        idx_out = jnp.where(sel, idxs[k], idx_out)
        rank_out = jnp.where(sel, rank_k, rank_out)
        gate_out = jnp.where(sel, exps[k] / denom, gate_out)
    idx_ref[...] = idx_out
    rank_ref[...] = rank_out
    gate_ref[...] = gate_out

    carry_ref[...] = carry_ref[...] + jnp.sum(hot_sum, axis=0, keepdims=True)
    cnt_ref[...] = carry_ref[...]


def _router(x1, norm_w, router_w_pad, router_b_pad, tt=512):
    t = x1.shape[0]
    tt = min(tt, t)
    kern = functools.partial(_router_kernel, tt=tt)
    row = pl.BlockSpec((tt, LANES), lambda i: (i, 0))
    return pl.pallas_call(
        kern,
        grid=(t // tt,),
        in_specs=[
            pl.BlockSpec((tt, D_MODEL), lambda i: (i, 0)),
            pl.BlockSpec((1, D_MODEL), lambda i: (0, 0)),
            pl.BlockSpec((D_MODEL, LANES), lambda i: (0, 0)),
            pl.BlockSpec((1, LANES), lambda i: (0, 0)),
        ],
        out_specs=[pl.BlockSpec((tt * ROW_CHUNKS, LANES), lambda i: (i, 0)), row, row, row,
                   pl.BlockSpec((1, LANES), lambda i: (0, 0))],
        out_shape=[
            jax.ShapeDtypeStruct((t * ROW_CHUNKS, LANES), F32),
            jax.ShapeDtypeStruct((t, LANES), jnp.int32),
            jax.ShapeDtypeStruct((t, LANES), F32),
            jax.ShapeDtypeStruct((t, LANES), jnp.int32),
            jax.ShapeDtypeStruct((1, LANES), F32),
        ],
        scratch_shapes=[pltpu.VMEM((1, LANES), F32)],
        compiler_params=_cparams(("arbitrary",)),
        name="router",
    )(x1, norm_w, router_w_pad, router_b_pad)


def _dispatch_kernel(pos_ref, h_ref, o_ref, sem, *, tt):
    def copy(t, k):
        src = pl.multiple_of(t * ROW_CHUNKS, ROW_CHUNKS)
        dst = pl.multiple_of(pos_ref[t * TOP_K + k] * ROW_CHUNKS, ROW_CHUNKS)
        return pltpu.make_async_copy(h_ref.at[pl.ds(src, ROW_CHUNKS), :], o_ref.at[pl.ds(dst, ROW_CHUNKS), :], sem)

    def issue(t, c):
        for k in range(TOP_K):
            copy(t, k).start()
        return c

    lax.fori_loop(0, tt, issue, 0, unroll=8)

    for _ in range(TOP_K):
        pltpu.make_async_copy(h_ref, o_ref.at[pl.ds(0, tt * ROW_CHUNKS), :], sem).wait()


def _dispatch(h2, pos_flat, tt=256):
    t = h2.shape[0] // ROW_CHUNKS
    tt = min(tt, t)
    kern = functools.partial(_dispatch_kernel, tt=tt)
    return pl.pallas_call(
        kern,
        grid=(t // tt,),
        in_specs=[
            pl.BlockSpec((tt * TOP_K,), lambda i: (i,), memory_space=pltpu.SMEM),
            pl.BlockSpec((tt * ROW_CHUNKS, LANES), lambda i: (i, 0)),
        ],
        out_specs=pl.BlockSpec(memory_space=pl.ANY),
        out_shape=jax.ShapeDtypeStruct((t * TOP_K * ROW_CHUNKS, LANES), F32),
        scratch_shapes=[pltpu.SemaphoreType.DMA(())],
        compiler_params=_cparams(("arbitrary",)),
        name="dispatch",
    )(pos_flat, h2)


def _experts_kernel(tile_ref, exp_ref, first_ref, efirst_ref, valid_ref, start_ref,
                    x_ref, wu_ref, bu_ref, wd_ref, bd_ref, o_ref, wub_ref, wdb_ref, *, tm):
    w = pl.program_id(0)

    @pl.when(efirst_ref[w] == 1)
    def _():
        for c0 in range(0, 2 * D_EXPERT, 512):
            wub_ref[:, c0:c0 + 512] = wu_ref[0, :, c0:c0 + 512].astype(BF16)
        for c0 in range(0, D_MODEL, 512):
            wdb_ref[:, c0:c0 + 512] = wd_ref[0, :, c0:c0 + 512].astype(BF16)

    @pl.when(valid_ref[w] == 1)
    def _():
        e = exp_ref[w]
        x = pltpu.einshape("(rc)l->r(cl)", x_ref[...], c=ROW_CHUNKS).astype(BF16)
        gu = jnp.dot(x, wub_ref[...], preferred_element_type=F32) + bu_ref[0]
        g = jnp.minimum(gu[:, :D_EXPERT], SWIGLU_LIMIT)
        u = jnp.clip(gu[:, D_EXPERT:], -SWIGLU_LIMIT, SWIGLU_LIMIT)
        act = (u + 1.0) * (g * _sigmoid(SWIGLU_ALPHA * g))
        out = jnp.dot(act.astype(BF16), wdb_ref[...], preferred_element_type=F32) + bd_ref[0]
        rows = tile_ref[w] * tm + lax.broadcasted_iota(jnp.int32, (tm, 1), 0)
        mine = (rows >= start_ref[e]) & (rows < start_ref[e + 1])

        def as_tiles(v):
            return pltpu.einshape("r(cl)->(rc)l", v, c=ROW_CHUNKS)

        @pl.when(first_ref[w] == 1)
        def _():
            o_ref[...] = as_tiles(jnp.where(mine, out, 0.0))

        @pl.when(first_ref[w] == 0)
        def _():
            o_ref[...] = as_tiles(jnp.where(mine, out, _tiles_to_rows(o_ref, tm)))


def _experts(x_sorted, item_tile, item_exp, item_first, item_efirst, item_valid, starts, w_up, b_up, w_down, b_down, tm):
    n = x_sorted.shape[0] // ROW_CHUNKS
    n_items = item_tile.shape[0]
    kern = functools.partial(_experts_kernel, tm=tm)
    grid_spec = pltpu.PrefetchScalarGridSpec(
        num_scalar_prefetch=6,
        grid=(n_items,),
        in_specs=[
            pl.BlockSpec((tm * ROW_CHUNKS, LANES), lambda w, tl, ex, *_: (tl[w], 0)),
            pl.BlockSpec((1, D_MODEL, 2 * D_EXPERT), lambda w, tl, ex, *_: (ex[w], 0, 0)),
            pl.BlockSpec((1, 1, 2 * D_EXPERT), lambda w, tl, ex, *_: (ex[w], 0, 0)),
            pl.BlockSpec((1, D_EXPERT, D_MODEL), lambda w, tl, ex, *_: (ex[w], 0, 0)),
            pl.BlockSpec((1, 1, D_MODEL), lambda w, tl, ex, *_: (ex[w], 0, 0)),
        ],
        out_specs=pl.BlockSpec((tm * ROW_CHUNKS, LANES), lambda w, tl, ex, *_: (tl[w], 0)),
        scratch_shapes=[pltpu.VMEM((D_MODEL, 2 * D_EXPERT), BF16), pltpu.VMEM((D_EXPERT, D_MODEL), BF16)],
    )
    return pl.pallas_call(
        kern,
        grid_spec=grid_spec,
        out_shape=jax.ShapeDtypeStruct((n * ROW_CHUNKS, LANES), F32),
        compiler_params=_cparams(("arbitrary",)),
        name="experts",
    )(item_tile, item_exp, item_first, item_efirst, item_valid, starts, x_sorted, w_up, b_up, w_down, b_down)


def _combine_kernel(pos_ref, posn_ref, gate_ref, x_ref, p_ref, pnw_ref, wpg_ref, wpp_ref, fnw_ref, ys_ref, o_ref,
                    rows_ref, sem, *, tt):
    i = pl.program_id(0)
    n = pl.num_programs(0)
    slot = i % 2

    def gather(idx_ref, dst_slot):
        def issue(t, c):
            for k in range(TOP_K):
                src = pl.multiple_of(idx_ref[t * TOP_K + k] * ROW_CHUNKS, ROW_CHUNKS)
                dst = pl.multiple_of((k * tt + t) * ROW_CHUNKS, ROW_CHUNKS)
                pltpu.make_async_copy(ys_ref.at[pl.ds(src, ROW_CHUNKS), :],
                                      rows_ref.at[dst_slot, pl.ds(dst, ROW_CHUNKS), :], sem.at[dst_slot]).start()
            return c

        lax.fori_loop(0, tt, issue, 0, unroll=8)

    @pl.when(i == 0)
    def _():
        gather(pos_ref, 0)

    @pl.when(i + 1 < n)
    def _():
        gather(posn_ref, 1 - slot)

    pltpu.make_async_copy(ys_ref.at[pl.ds(0, TOP_K * tt * ROW_CHUNKS), :], rows_ref.at[slot], sem.at[slot]).wait()

    gate = gate_ref[...]
    x = x_ref[...]
    for k in range(TOP_K):
        tiles = rows_ref[slot, pl.ds(k * tt * ROW_CHUNKS, tt * ROW_CHUNKS), :]
        x = x + gate[:, k:k + 1] * pltpu.einshape("(rc)l->r(cl)", tiles, c=ROW_CHUNKS)

    hg = _rms(x, pnw_ref[...]).astype(BF16)
    pg = _sigmoid(jnp.dot(hg, wpg_ref[...], preferred_element_type=F32))
    x = x + pg * jnp.dot(p_ref[...].astype(BF16), wpp_ref[...], preferred_element_type=F32)
    o_ref[...] = _rms(x, fnw_ref[...])


def _combine(pos_flat, gate, x1, p2d, ple_nw, w_pg, w_pp, final_nw, y_sorted, tt=256):
    t = x1.shape[0]
    tt = min(tt, t)
    kern = functools.partial(_combine_kernel, tt=tt)
    vec = pl.BlockSpec((1, D_MODEL), lambda i: (0, 0))
    nt = t // tt
    return pl.pallas_call(
        kern,
        grid=(nt,),
        in_specs=[
            pl.BlockSpec((tt * TOP_K,), lambda i: (i,), memory_space=pltpu.SMEM),
            pl.BlockSpec((tt * TOP_K,), lambda i: (jnp.minimum(i + 1, nt - 1),), memory_space=pltpu.SMEM),
            pl.BlockSpec((tt, LANES), lambda i: (i, 0)),
            pl.BlockSpec((tt, D_MODEL), lambda i: (i, 0)),
            pl.BlockSpec((tt, PLE_DIM), lambda i: (i, 0)),
            vec,
            pl.BlockSpec((D_MODEL, D_MODEL), lambda i: (0, 0)),
            pl.BlockSpec((PLE_DIM, D_MODEL), lambda i: (0, 0)),
            vec,
            pl.BlockSpec(memory_space=pl.ANY),
        ],
        out_specs=pl.BlockSpec((tt, D_MODEL), lambda i: (i, 0)),
        out_shape=jax.ShapeDtypeStruct((t, D_MODEL), F32),
        scratch_shapes=[pltpu.VMEM((2, TOP_K * tt * ROW_CHUNKS, LANES), F32), pltpu.SemaphoreType.DMA((2,))],
        compiler_params=_cparams(("arbitrary",)),
        name="combine",
    )(pos_flat, pos_flat, gate, x1, p2d, ple_nw, w_pg, w_pp, final_nw, y_sorted)


def _routing_plan(counts, idx, rank, tm, n_rows):
    counts = counts.astype(jnp.int32)
    ends = jnp.cumsum(counts)
    starts = ends - counts
    pos = (starts[idx] + rank).reshape(-1)
    n_tiles = n_rows // tm
    n_items = n_tiles + N_EXPERTS - 1
    first_tile = starts // tm
    last_tile = jnp.maximum(ends - 1, 0) // tm
    items_e = jnp.where(counts > 0, last_tile - first_tile + 1, 0)
    item_end = jnp.cumsum(items_e)
    item_start = item_end - items_e
    total = item_end[-1]
    w = jnp.arange(n_items, dtype=jnp.int32)
    e_w = jnp.minimum(jnp.sum(w[:, None] >= item_end[None, :], axis=1).astype(jnp.int32), N_EXPERTS - 1)
    tile_w = first_tile[e_w] + (w - item_start[e_w])
    valid = w < total
    last = jnp.maximum(total - 1, 0)
    e_w = jnp.where(valid, e_w, e_w[last])
    tile_w = jnp.where(valid, tile_w, tile_w[last])
    prev_tile = jnp.concatenate([jnp.full((1,), -1, jnp.int32), tile_w[:-1]])
    first = (tile_w != prev_tile) & valid
    prev_e = jnp.concatenate([jnp.full((1,), -1, jnp.int32), e_w[:-1]])
    efirst = (e_w != prev_e) & valid
    starts_ext = jnp.concatenate([starts, ends[-1:]]).astype(jnp.int32)
    return (pos.astype(jnp.int32), tile_w.astype(jnp.int32), e_w, first.astype(jnp.int32), efirst.astype(jnp.int32),
            valid.astype(jnp.int32), starts_ext)


def kernel(x, p, mix_norm_w, w_in, lambda_q1, lambda_k1, lambda_q2, lambda_k2, da_head_norm_w, w_attn_branch, conv_w, conv_b, dt_bias, a_log, d_skip, ssd_norm_w, w_ssd_branch, w_out, moe_norm_w, router_w, router_b, w_up, b_up, w_down, b_down, ple_norm_w, w_ple_gate, w_ple_proj, final_norm_w):
    batch, seq, _ = x.shape
    t = batch * seq
    depth = w_in.shape[0]
    assert depth == 1, "the final RMSNorm is fused into the layer's last kernel"
    x2d = x.reshape(t, D_MODEL)

    log2e = math.log2(math.e)
    slopes = jnp.asarray([log2e * 2.0 ** (-8.0 * (h + 1) / DA_HEADS) for h in range(DA_HEADS)], F32)
    q_scale = log2e * DA_QK_DIM ** -0.5
    head_ids = jnp.arange(SSD_D_INNER, dtype=jnp.int32) // SSD_HEAD_DIM
    expand = (jnp.arange(LANES, dtype=jnp.int32)[:, None] == head_ids[None, :]).astype(BF16)

    def pad_lanes(v, fill=0.0):
        return jnp.pad(v.reshape(1, -1), ((0, 0), (0, LANES - v.shape[-1])), constant_values=fill)

    for i in range(depth):
        wi = w_in[i]
        w_main = jnp.concatenate([
            wi[:, _R_Z:_R_Z + SSD_D_INNER], wi[:, _R_Q:_R_Q + 1024] * q_scale, wi[:, _R_K:_R_K + 1024],
            wi[:, _R_GA:_R_GA + 1024], wi[:, _R_GS:_R_GS + 1024], wi[:, _R_XBC:_R_XBC + SSD_CONV_DIM],
            wi[:, _R_V:_R_V + 1024]], axis=1).astype(BF16)
        w_dt = jnp.pad(wi[:, _R_DT:_R_DT + SSD_HEADS], ((0, 0), (0, LANES - SSD_HEADS))).astype(BF16)

        proj, dt_raw, v_t = _in_proj(x2d, mix_norm_w[i].reshape(1, -1), w_main, w_dt)

        lambda_init = 0.8 - 0.6 * math.exp(-0.3 * i)
        lam = (jnp.exp(jnp.sum(lambda_q1[i] * lambda_k1[i])) - jnp.exp(jnp.sum(lambda_q2[i] * lambda_k2[i]))
               + lambda_init).reshape(1).astype(F32)
        y_attn = _diff_attn(proj, v_t, slopes, lam, da_head_norm_w[i].reshape(1, -1), batch, seq, lambda_init)

        y_ssd = _ssd(proj, dt_raw, conv_w[i], conv_b[i].reshape(1, -1), pad_lanes(dt_bias[i]), pad_lanes(a_log[i]),
                     jnp.repeat(d_skip[i], SSD_HEAD_DIM).reshape(1, -1), ssd_norm_w[i].reshape(1, -1), expand,
                     batch, seq)

        x1 = _merge(y_attn, y_ssd, proj, x2d, w_attn_branch[i].astype(BF16), w_ssd_branch[i].astype(BF16),
                    w_out[i].astype(BF16))

        rw = jnp.pad(router_w[i], ((0, 0), (0, LANES - N_EXPERTS)))
        h2, idx, gate, rank, counts = _router(x1, moe_norm_w[i].reshape(1, -1), rw, pad_lanes(router_b[i]))

        tm = 512
        n_rows = t * TOP_K
        pos, item_tile, item_exp, item_first, item_efirst, item_valid, starts = _routing_plan(
            counts[0, :N_EXPERTS], idx[:, :TOP_K], rank[:, :TOP_K], tm, n_rows)

        x_sorted = _dispatch(h2, pos)
        y_sorted = _experts(x_sorted, item_tile, item_exp, item_first, item_efirst, item_valid, starts,
                            w_up[i], b_up[i].reshape(N_EXPERTS, 1, -1),
                            w_down[i], b_down[i].reshape(N_EXPERTS, 1, -1), tm)

        x2d = _combine(pos, gate, x1, p[i].reshape(t, PLE_DIM), ple_norm_w[i].reshape(1, -1),
                       w_ple_gate[i].astype(BF16), w_ple_proj[i].astype(BF16), final_norm_w.reshape(1, -1), y_sorted)
    return x2d.reshape(batch, seq, D_MODEL)
```

```python
import functools
import math

import jax
import jax.numpy as jnp
from jax import lax
from jax.experimental import pallas as pl
from jax.experimental.pallas import tpu as pltpu

F32 = jnp.float32
BF16 = jnp.bfloat16

D_MODEL = 1024
CHUNK = 64
PLE_DIM = 256
RMS_EPS = 1e-6

DA_HEADS = 8
DA_QK_DIM = 64
DA_V_DIM = 128

SSD_D_INNER = 2048
SSD_HEAD_DIM = 64
SSD_HEADS = 32
SSD_GROUPS = 4
SSD_STATE = 128
SSD_CONV = 4
SSD_CONV_DIM = 3072

N_EXPERTS = 32
TOP_K = 4
D_EXPERT = 1024
SWIGLU_LIMIT = 7.0
SWIGLU_ALPHA = 1.702

LANES = 128

COL_Z = 0
COL_Q = 2048
COL_K = 3072
COL_GA = 4096
COL_GS = 5120
COL_XBC = 6144
PROJ_W = 9216

ATTN_TILE = 256
ATTN_Q_TILE = 512
SUM_ROWS = 16

_R_Q, _R_K, _R_V, _R_Z, _R_XBC, _R_DT, _R_GA, _R_GS = 0, 1024, 2048, 3072, 5120, 8192, 8224, 9248

VMEM_LIMIT = 56 * 1024 * 1024


def _cparams(sem):
    return pltpu.CompilerParams(dimension_semantics=sem, vmem_limit_bytes=VMEM_LIMIT)


def _rms(x, w):
    return x * lax.rsqrt(jnp.mean(x * x, axis=-1, keepdims=True) + RMS_EPS) * w


ROW_CHUNKS = D_MODEL // LANES


def _rows_to_tiles(ref, val, n):
    for c in range(ROW_CHUNKS):
        ref[pl.ds(c, n, stride=ROW_CHUNKS), :] = val[:, c * LANES:(c + 1) * LANES]


def _tiles_to_rows(ref, n, base=0):
    return jnp.concatenate([ref[pl.ds(base + c, n, stride=ROW_CHUNKS), :] for c in range(ROW_CHUNKS)], axis=1)


def _sigmoid(x):
    return 1.0 / (1.0 + jnp.exp2(x * (-math.log2(math.e))))


def _inproj_kernel(x_ref, nw_ref, w_ref, wdt_ref, o_ref, dt_ref, vt_ref, h_ref, *, tm, n_proj):
    j = pl.program_id(1)

    @pl.when(j == 0)
    def _():
        hb = _rms(x_ref[...], nw_ref[...]).astype(BF16)
        h_ref[...] = hb
        dt_ref[...] = jnp.dot(hb, wdt_ref[...], preferred_element_type=F32)

    acc = jnp.dot(h_ref[...], w_ref[...], preferred_element_type=F32)

    @pl.when(j < n_proj)
    def _():
        o_ref[...] = acc.astype(BF16)

    @pl.when(j == n_proj)
    def _():
        vt = acc.T
        for c in range(tm // ATTN_TILE):
            vt_ref[c] = vt[:, c * ATTN_TILE:(c + 1) * ATTN_TILE].astype(BF16)


def _in_proj(x2d, norm_w, w_main, w_dt, tm=2048):
    t = x2d.shape[0]
    tm = min(tm, t)
    tn = DA_HEADS * DA_V_DIM
    n_proj = PROJ_W // tn
    return pl.pallas_call(
        functools.partial(_inproj_kernel, tm=tm, n_proj=n_proj),
        grid=(t // tm, n_proj + 1),
        in_specs=[
            pl.BlockSpec((tm, D_MODEL), lambda i, j: (i, 0)),
            pl.BlockSpec((1, D_MODEL), lambda i, j: (0, 0)),
            pl.BlockSpec((D_MODEL, tn), lambda i, j: (0, j)),
            pl.BlockSpec((D_MODEL, LANES), lambda i, j: (0, 0)),
        ],
        out_specs=[
            pl.BlockSpec((tm, tn), lambda i, j: (i, jnp.minimum(j, n_proj - 1))),
            pl.BlockSpec((tm, LANES), lambda i, j: (i, 0)),
            pl.BlockSpec((tm // ATTN_TILE, tn, ATTN_TILE), lambda i, j: (i, 0, 0)),
        ],
        out_shape=[
            jax.ShapeDtypeStruct((t, PROJ_W), BF16),
            jax.ShapeDtypeStruct((t, LANES), F32),
            jax.ShapeDtypeStruct((t // ATTN_TILE, tn, ATTN_TILE), BF16),
        ],
        scratch_shapes=[pltpu.VMEM((tm, D_MODEL), BF16)],
        compiler_params=_cparams(("parallel", "arbitrary")),
        name="in_proj",
    )(x2d, norm_w, w_main, w_dt)


def _attn_kernel(slopes_ref, lam_ref, q_ref, k_ref, vt_ref, nw_ref, o_ref,
                 bias_ref, ka_ref, kb_ref, qc_ref, s_ref, p_ref, al_ref, m_ref, a_ref, *, tq, tk, out_scale):
    h = pl.program_id(1)
    i = pl.program_id(2)
    slope = slopes_ref[h]
    lam = lam_ref[0]
    r = tq // tk
    n_past = r * i
    n_tiles = n_past + r
    seq = k_ref.shape[0]
    aux = 16

    @pl.when(i == 0)
    def _():
        kk = lax.broadcasted_iota(jnp.int32, (tk, tq), 0)
        qq = lax.broadcasted_iota(jnp.int32, (tk, tq), 1)
        bias_ref[0] = jnp.zeros((tk, tq), F32)
        for d in range(r):
            kd = kk + d * tk
            true = jnp.where((kd // CHUNK) <= (qq // CHUNK), -slope * jnp.abs(qq - kd).astype(F32), -jnp.inf)
            bias_ref[1 + d] = true + slope * (qq - kk).astype(F32)
        rows = 512
        lane = lax.broadcasted_iota(jnp.int32, (rows, LANES), 1)
        krel = (lax.broadcasted_iota(jnp.int32, (rows, LANES), 0) % tk).astype(F32)
        lo = lane - DA_QK_DIM
        aux_a = jnp.where(lo < 3, krel, jnp.where(lo < 6, 1.0, 0.0))
        aux_b = jnp.where(lane < 3, krel, jnp.where(lane < 6, 1.0, 0.0))
        for r0 in range(0, seq, rows):
            k = k_ref[r0:r0 + rows, :].astype(F32)
            ka_ref[r0:r0 + rows, :] = jnp.where(lane < DA_QK_DIM, k, aux_a).astype(BF16)
            kb_ref[r0:r0 + rows, :] = jnp.where(lane >= DA_QK_DIM, k, aux_b).astype(BF16)

    qt = q_ref[...].astype(F32).T.astype(BF16)
    arow = lax.broadcasted_iota(jnp.int32, (aux, tq), 0)
    qrel = lax.broadcasted_iota(jnp.int32, (aux, tq), 1).astype(F32)
    pieces = _split3(jnp.full((aux, tq), slope, F32)) + _split3(-slope * qrel)
    extra = jnp.zeros((aux, tq), F32)
    for n, piece in enumerate(pieces):
        extra = jnp.where(arow == n, piece.astype(F32), extra)
    extra = extra.astype(BF16)
    pad = jnp.zeros((DA_QK_DIM - aux, tq), BF16)
    qc_ref[:, 0:tq] = jnp.concatenate([qt[0:DA_QK_DIM], extra, pad], axis=0)
    qc_ref[:, tq:2 * tq] = jnp.concatenate([extra, pad, qt[DA_QK_DIM:]], axis=0)

    m_ref[...] = jnp.full(m_ref.shape, -jnp.inf, F32)
    a_ref[...] = jnp.zeros(a_ref.shape, F32)

    def scores(j):
        rows = pl.ds(pl.multiple_of(j * tk, tk), tk)
        return (jnp.dot(ka_ref[rows, :], qc_ref[:, 0:tq], preferred_element_type=F32),
                jnp.dot(kb_ref[rows, :], qc_ref[:, tq:2 * tq], preferred_element_type=F32))

    def store_scores(blocks):
        s_ref[:, 0:tq] = blocks[0]
        s_ref[:, tq:2 * tq] = blocks[1]

    def softmax_step(sel, cj):
        for g in range(2 * tq // LANES):
            cols = slice(g * LANES, (g + 1) * LANES)
            bcols = slice((g * LANES) % tq, (g * LANES) % tq + LANES)
            s = s_ref[:, cols]
            if sel is not None:
                s = s + bias_ref[sel, :, bcols]
            m_prev = m_ref[:, cols]
            m_cur = jnp.maximum(m_prev, jnp.max(s, axis=0, keepdims=True) + cj)
            alpha = jnp.exp2(m_prev - m_cur)
            p = jnp.exp2(s + (cj - m_cur))
            m_ref[:, cols] = m_cur
            al_ref[:, cols] = alpha
            p_ref[:, cols] = p.astype(BF16)

    def pv_step(j):
        vt = jnp.concatenate([vt_ref[j], jnp.ones((SUM_ROWS, tk), BF16)], axis=0)
        for g in range(2 * tq // 256):
            cols = slice(g * 256, (g + 1) * 256)
            a_ref[:, cols] = al_ref[:, cols] * a_ref[:, cols] + jnp.dot(vt, p_ref[:, cols], preferred_element_type=F32)

    def past_cj(j):
        return -slope * (i * tq - j * tk).astype(F32)

    store_scores(scores(0))
    s_next = scores(1)
    softmax_step(jnp.where(i > 0, 0, 1), jnp.where(i > 0, past_cj(0), 0.0))
    store_scores(s_next)

    def body(j, carry):
        s_next = scores(j + 1)
        pv_step(j - 1)
        softmax_step(None, past_cj(j))
        store_scores(s_next)
        return carry

    lax.fori_loop(1, n_past, body, 0)

    for d in range(r):
        t = n_past + d

        def overlap_tile(t=t, d=d):
            if d + 1 < r:
                s_next = scores(t + 1)
            pv_step(t - 1)
            softmax_step(1 + d, 0.0)
            if d + 1 < r:
                store_scores(s_next)

        if d == 0:
            pl.when(i > 0)(overlap_tile)
        else:
            overlap_tile()
    pv_step(n_tiles - 1)

    on = a_ref[0:DA_V_DIM, :] / a_ref[DA_V_DIM:DA_V_DIM + 1, :]
    ot = on[:, :tq] - lam * on[:, tq:]
    ot = ot * lax.rsqrt(jnp.mean(ot * ot, axis=0, keepdims=True) + RMS_EPS)
    o_ref[...] = (ot.T * nw_ref[...] * out_scale).astype(BF16)


def _diff_attn(proj, v_t, slopes, lam, head_norm_w, batch, seq, lambda_init):
    tk = ATTN_TILE
    tq = min(ATTN_Q_TILE, seq)
    nq = seq // tq
    nk = seq // tk
    t = batch * seq
    qb, kb = COL_Q // LANES, COL_K // LANES
    assert tq % tk == 0 and tq >= 2 * tk, "the peeled first/last key tiles need at least two tiles per step"
    kern = functools.partial(_attn_kernel, tq=tq, tk=tk, out_scale=1.0 - lambda_init)
    smem = pl.BlockSpec(memory_space=pltpu.SMEM)
    return pl.pallas_call(
        kern,
        grid=(batch, DA_HEADS, nq),
        in_specs=[
            smem, smem,
            pl.BlockSpec((tq, LANES), lambda b, h, i: (b * nq + i, qb + h)),
            pl.BlockSpec((seq, LANES), lambda b, h, i: (b, kb + h)),
            pl.BlockSpec((nk, DA_V_DIM, tk), lambda b, h, i: (b, h, 0)),
            pl.BlockSpec((1, LANES), lambda b, h, i: (0, 0)),
        ],
        out_specs=pl.BlockSpec((tq, LANES), lambda b, h, i: (b * nq + i, h)),
        out_shape=jax.ShapeDtypeStruct((t, DA_HEADS * DA_V_DIM), BF16),
        scratch_shapes=[
            pltpu.VMEM((1 + tq // tk, tk, tq), F32),
            pltpu.VMEM((seq, LANES), BF16),
            pltpu.VMEM((seq, LANES), BF16),
            pltpu.VMEM((LANES, 2 * tq), BF16),
            pltpu.VMEM((tk, 2 * tq), F32),
            pltpu.VMEM((tk, 2 * tq), BF16),
            pltpu.VMEM((1, 2 * tq), F32),
            pltpu.VMEM((1, 2 * tq), F32),
            pltpu.VMEM((DA_V_DIM + SUM_ROWS, 2 * tq), F32),
        ],
        compiler_params=_cparams(("parallel", "parallel", "arbitrary")),
        name="diff_attn",
    )(slopes, lam, proj, proj, v_t, head_norm_w)


def _split2(x):
    hi = x.astype(BF16)
    lo = (x - hi.astype(F32)).astype(BF16)
    return hi, lo


def _split3(x):
    hi = x.astype(BF16)
    r = x - hi.astype(F32)
    mid = r.astype(BF16)
    lo = (r - mid.astype(F32)).astype(BF16)
    return hi, mid, lo


def _ssd_kernel(xbc_ref, z_ref, dt_ref, cw_ref, cb_ref, dtb_ref, alog_ref, dsk_ref, nw_ref, e_ref,
                o_ref, ubuf, shift_ref, xs_scr, bc_scr, acum_scr, acb, dfs_scr, xdt_scr, xw_scr, y_scr, state, *, tr):
    i = pl.program_id(1)
    nch = tr // CHUNK
    gw = SSD_D_INNER // SSD_GROUPS
    pad = 8

    head = 16

    @pl.when(i == 0)
    def _():
        ubuf[0:pad, :] = jnp.zeros((pad, SSD_CONV_DIM), F32)
        state[...] = jnp.zeros_like(state)
        r_s = lax.broadcasted_iota(jnp.int32, (tr, tr), 0)
        c_s = lax.broadcasted_iota(jnp.int32, (tr, tr), 1)
        for d in range(1, SSD_CONV):
            shift_ref[(d - 1) * tr:d * tr, :] = jnp.where(c_s == r_s - d, 1.0, 0.0).astype(BF16)

    ubuf[pad:pad + head, :] = xbc_ref[0:head, :].astype(F32)
    cs = 512
    for c0 in range(0, SSD_CONV_DIM, cs):
        ub = xbc_ref[:, c0:c0 + cs]
        sh = jnp.dot(shift_ref[...], ub, preferred_element_type=F32)
        acc = cb_ref[:, c0:c0 + cs] + cw_ref[SSD_CONV - 1:SSD_CONV, c0:c0 + cs] * ub.astype(F32)
        for d in range(1, SSD_CONV):
            acc = acc + cw_ref[SSD_CONV - 1 - d:SSD_CONV - d, c0:c0 + cs] * sh[(d - 1) * tr:d * tr, :]
        u = acc * _sigmoid(acc)
        acc_h = jnp.broadcast_to(cb_ref[:, c0:c0 + cs], (head, cs))
        for k in range(SSD_CONV):
            off = pad - (SSD_CONV - 1) + k
            acc_h = acc_h + cw_ref[k:k + 1, c0:c0 + cs] * ubuf[off:off + head, c0:c0 + cs]
        u_h = acc_h * _sigmoid(acc_h)
        if c0 < SSD_D_INNER:
            xs_scr[:, c0:c0 + cs] = u
            xs_scr[0:head, c0:c0 + cs] = u_h
        else:
            bc_scr[:, c0 - SSD_D_INNER:c0 - SSD_D_INNER + cs] = u.astype(BF16)
            bc_scr[0:head, c0 - SSD_D_INNER:c0 - SSD_D_INNER + cs] = u_h.astype(BF16)
    ubuf[0:pad, :] = xbc_ref[tr - 2 * pad:tr, :].astype(F32)[pad:, :]

    xdt_in = dt_ref[...] + dtb_ref[...]
    dtv = jnp.maximum(xdt_in, 0.0) + jnp.log1p(jnp.exp(-jnp.abs(xdt_in)))
    da = dtv * (-jnp.exp(alog_ref[...]))
    r_i = lax.broadcasted_iota(jnp.int32, (tr, tr), 0)
    c_i = lax.broadcasted_iota(jnp.int32, (tr, tr), 1)
    tril = jnp.where((c_i <= r_i) & ((c_i // CHUNK) == (r_i // CHUNK)), 1.0, 0.0).astype(BF16)
    acum = sum(jnp.dot(tril, piece, preferred_element_type=F32) for piece in _split3(da))
    acum_scr[...] = acum

    a_hi, a_lo = _split2(acum)
    d_hi, d_lo = _split2(dtv)
    for c0 in range(0, SSD_D_INNER, cs):
        e = e_ref[:, c0:c0 + cs]
        ab = jnp.dot(a_hi, e, preferred_element_type=F32) + jnp.dot(a_lo, e, preferred_element_type=F32)
        db = jnp.dot(d_hi, e, preferred_element_type=F32) + jnp.dot(d_lo, e, preferred_element_type=F32)
        acb[:, c0:c0 + cs] = ab
        dfs_scr[:, c0:c0 + cs] = jnp.exp(ab)
        xdt = xs_scr[:, c0:c0 + cs] * db
        xdt_scr[:, c0:c0 + cs] = xdt.astype(BF16)
        for c in range(nch):
            r0 = c * CHUNK
            last = ab[r0 + CHUNK - 1:r0 + CHUNK, :]
            xw_scr[r0:r0 + CHUNK, c0:c0 + cs] = (xdt[r0:r0 + CHUNK, :] * jnp.exp(last - ab[r0:r0 + CHUNK, :])).astype(BF16)

    lane2 = lax.broadcasted_iota(jnp.int32, (CHUNK, LANES), 1)
    row2 = lax.broadcasted_iota(jnp.int32, (CHUNK, LANES), 0)
    causal2 = row2 >= (lane2 % SSD_HEAD_DIM)
    lo_half = lane2 < SSD_HEAD_DIM
    zpad = jnp.zeros((LANES - CHUNK, LANES), F32)

    def chunk_body(c, carry):
        r0 = pl.multiple_of(c * CHUNK, CHUNK)
        rows = pl.ds(r0, CHUNK)
        a_c = acum_scr[rows, :]
        a_t = jnp.concatenate([a_c, zpad], axis=0).T
        a_t_r = pltpu.roll(a_t, SSD_HEAD_DIM, 1)
        last_row = acb[pl.ds(r0 + CHUNK - 1, 1), :]
        for g in range(SSD_GROUPS):
            bg = bc_scr[rows, g * SSD_STATE:(g + 1) * SSD_STATE]
            cg = bc_scr[rows, SSD_GROUPS * SSD_STATE + g * SSD_STATE:SSD_GROUPS * SSD_STATE + (g + 1) * SSD_STATE]
            cbt = lax.dot_general(cg, jnp.concatenate([bg, bg], axis=0), (((1,), (1,)), ((), ())),
                                  preferred_element_type=F32)
            g0 = g * gw
            st = state[g]
            yoff = jnp.dot(cg, st.astype(BF16), preferred_element_type=F32) * dfs_scr[rows, g0:g0 + gw]
            for jj in range(gw // LANES):
                pidx = g * (gw // LANES) + jj
                l0 = pidx * LANES
                arow = a_t[2 * pidx:2 * pidx + 1, :] + a_t_r[2 * pidx + 1:2 * pidx + 2, :]
                seg = acb[rows, l0:l0 + LANES] - arow
                decay = jnp.exp(jnp.where(causal2, seg, -jnp.inf))
                mp = (cbt * decay).astype(BF16)
                xd = xdt_scr[rows, l0:l0 + LANES]
                zb = jnp.zeros_like(xd)
                bd = jnp.concatenate([jnp.where(lo_half, xd, zb), jnp.where(lo_half, zb, xd)], axis=0)
                yd = jnp.dot(mp, bd, preferred_element_type=F32)
                y_scr[rows, l0:l0 + LANES] = yd + yoff[:, jj * LANES:(jj + 1) * LANES]
            cd = jnp.exp(last_row[:, g0:g0 + gw])
            upd = lax.dot_general(bg, xw_scr[rows, g0:g0 + gw], (((0,), (0,)), ((), ())),
                                  preferred_element_type=F32)
            state[g] = st * cd + upd
        return carry

    lax.fori_loop(0, nch, chunk_body, 0, unroll=True)

    for g in range(SSD_GROUPS):
        g0 = g * gw
        y = y_scr[:, g0:g0 + gw] + dsk_ref[:, g0:g0 + gw] * xs_scr[:, g0:g0 + gw]
        zz = z_ref[:, g0:g0 + gw].astype(F32)
        y = y * (zz * _sigmoid(zz))
        o_ref[:, g0:g0 + gw] = _rms(y, nw_ref[:, g0:g0 + gw]).astype(BF16)


def _ssd(proj, dt_raw, conv_w, conv_b, dt_bias, a_log, d_skip_b, norm_w, expand, batch, seq, tr=256):
    t = batch * seq
    nr = seq // tr
    kern = functools.partial(_ssd_kernel, tr=tr)
    const = lambda shape: pl.BlockSpec(shape, lambda b, i: (0, 0))
    return pl.pallas_call(
        kern,
        grid=(batch, nr),
        in_specs=[
            pl.BlockSpec((tr, SSD_CONV_DIM), lambda b, i: (b * nr + i, COL_XBC // SSD_CONV_DIM)),
            pl.BlockSpec((tr, SSD_D_INNER), lambda b, i: (b * nr + i, COL_Z // SSD_D_INNER)),
            pl.BlockSpec((tr, LANES), lambda b, i: (b * nr + i, 0)),
            const((SSD_CONV, SSD_CONV_DIM)),
            const((1, SSD_CONV_DIM)),
            const((1, LANES)),
            const((1, LANES)),
            const((1, SSD_D_INNER)),
            const((1, SSD_D_INNER)),
            const((LANES, SSD_D_INNER)),
        ],
        out_specs=pl.BlockSpec((tr, SSD_D_INNER), lambda b, i: (b * nr + i, 0)),
        out_shape=jax.ShapeDtypeStruct((t, SSD_D_INNER), BF16),
        scratch_shapes=[
            pltpu.VMEM((8 + 16, SSD_CONV_DIM), F32),
            pltpu.VMEM(((SSD_CONV - 1) * tr, tr), BF16),
            pltpu.VMEM((tr, SSD_D_INNER), F32),
            pltpu.VMEM((tr, 2 * SSD_GROUPS * SSD_STATE), BF16),
            pltpu.VMEM((tr, LANES), F32),
            pltpu.VMEM((tr, SSD_D_INNER), F32),
            pltpu.VMEM((tr, SSD_D_INNER), F32),
            pltpu.VMEM((tr, SSD_D_INNER), BF16),
            pltpu.VMEM((tr, SSD_D_INNER), BF16),
            pltpu.VMEM((tr, SSD_D_INNER), F32),
            pltpu.VMEM((SSD_GROUPS, SSD_STATE, SSD_D_INNER // SSD_GROUPS), F32),
        ],
        compiler_params=_cparams(("parallel", "arbitrary")),
        name="ssd",
    )(proj, proj, dt_raw, conv_w, conv_b, dt_bias, a_log, d_skip_b, norm_w, expand)


def _merge_kernel(ya_ref, ys_ref, ga_ref, gs_ref, x_ref, wa_ref, wb_ref, wo_ref, o_ref):
    a = jnp.dot(ya_ref[...], wa_ref[...], preferred_element_type=F32)
    s = jnp.dot(ys_ref[...], wb_ref[...], preferred_element_type=F32)
    merged = _sigmoid(ga_ref[...].astype(F32)) * a + _sigmoid(gs_ref[...].astype(F32)) * s
    o_ref[...] = x_ref[...] + jnp.dot(merged.astype(BF16), wo_ref[...], preferred_element_type=F32)


def _merge(y_attn, y_ssd, proj, x2d, w_a, w_b, w_o, tm=512):
    t = x2d.shape[0]
    tm = min(tm, t)
    full = lambda r, c: pl.BlockSpec((r, c), lambda i: (0, 0))
    return pl.pallas_call(
        _merge_kernel,
        grid=(t // tm,),
        in_specs=[
            pl.BlockSpec((tm, D_MODEL), lambda i: (i, 0)),
            pl.BlockSpec((tm, SSD_D_INNER), lambda i: (i, 0)),
            pl.BlockSpec((tm, D_MODEL), lambda i: (i, COL_GA // D_MODEL)),
            pl.BlockSpec((tm, D_MODEL), lambda i: (i, COL_GS // D_MODEL)),
            pl.BlockSpec((tm, D_MODEL), lambda i: (i, 0)),
            full(D_MODEL, D_MODEL), full(SSD_D_INNER, D_MODEL), full(D_MODEL, D_MODEL),
        ],
        out_specs=pl.BlockSpec((tm, D_MODEL), lambda i: (i, 0)),
        out_shape=jax.ShapeDtypeStruct((t, D_MODEL), F32),
        compiler_params=_cparams(("parallel",)),
        name="merge",
    )(y_attn, y_ssd, proj, proj, x2d, w_a, w_b, w_o)


def _router_kernel(x_ref, nw_ref, rw_ref, rb_ref, h_ref, idx_ref, gate_ref, rank_ref, cnt_ref, carry_ref, *, tt):
    i = pl.program_id(0)

    @pl.when(i == 0)
    def _():
        carry_ref[...] = jnp.zeros_like(carry_ref)

    h = _rms(x_ref[...], nw_ref[...])
    h_ref[...] = pltpu.einshape("r(cl)->(rc)l", h, c=ROW_CHUNKS)
    logits = jnp.dot(h, rw_ref[...], preferred_element_type=F32, precision=lax.Precision.HIGHEST) + rb_ref[...]
    lane = lax.broadcasted_iota(jnp.int32, (tt, LANES), 1)
    work = jnp.where(lane < N_EXPERTS, logits, -jnp.inf)

    vals, idxs, hots = [], [], []
    for _ in range(TOP_K):
        m = jnp.max(work, axis=-1, keepdims=True)
        idx = jnp.min(jnp.where(work == m, lane, LANES), axis=-1, keepdims=True)
        hot = lane == idx
        vals.append(m)
        idxs.append(idx)
        hots.append(hot)
        work = jnp.where(hot, -jnp.inf, work)

    exps = [jnp.exp(v - vals[0]) for v in vals]
    denom = exps[0] + exps[1] + exps[2] + exps[3]

    hot_sum = sum(jnp.where(hot, 1.0, 0.0) for hot in hots)
    r_i = lax.broadcasted_iota(jnp.int32, (tt, tt), 0)
    c_i = lax.broadcasted_iota(jnp.int32, (tt, tt), 1)
    strict = jnp.where(c_i < r_i, 1.0, 0.0).astype(BF16)
    prefix = jnp.dot(strict, hot_sum.astype(BF16), preferred_element_type=F32) + carry_ref[...]

    idx_out = jnp.zeros((tt, LANES), jnp.int32)
    rank_out = jnp.zeros((tt, LANES), jnp.int32)
    gate_out = jnp.zeros((tt, LANES), F32)
    for k in range(TOP_K):
        rank_k = jnp.sum(jnp.where(hots[k], prefix, 0.0), axis=-1, keepdims=True).astype(jnp.int32)
        sel = lane == k
        idx_out = jnp.where(sel, idxs[k], idx_out)
        rank_out = jnp.where(sel, rank_k, rank_out)
        gate_out = jnp.where(sel, exps[k] / denom, gate_out)
    idx_ref[...] = idx_out
    rank_ref[...] = rank_out
    gate_ref[...] = gate_out

    carry_ref[...] = carry_ref[...] + jnp.sum(hot_sum, axis=0, keepdims=True)
    cnt_ref[...] = carry_ref[...]


def _router(x1, norm_w, router_w_pad, router_b_pad, tt=512):
    t = x1.shape[0]
    tt = min(tt, t)
    kern = functools.partial(_router_kernel, tt=tt)
    row = pl.BlockSpec((tt, LANES), lambda i: (i, 0))
    return pl.pallas_call(
        kern,
        grid=(t // tt,),
        in_specs=[
            pl.BlockSpec((tt, D_MODEL), lambda i: (i, 0)),
            pl.BlockSpec((1, D_MODEL), lambda i: (0, 0)),
            pl.BlockSpec((D_MODEL, LANES), lambda i: (0, 0)),
            pl.BlockSpec((1, LANES), lambda i: (0, 0)),
        ],
        out_specs=[pl.BlockSpec((tt * ROW_CHUNKS, LANES), lambda i: (i, 0)), row, row, row,
                   pl.BlockSpec((1, LANES), lambda i: (0, 0))],
        out_shape=[
            jax.ShapeDtypeStruct((t * ROW_CHUNKS, LANES), F32),
            jax.ShapeDtypeStruct((t, LANES), jnp.int32),
            jax.ShapeDtypeStruct((t, LANES), F32),
            jax.ShapeDtypeStruct((t, LANES), jnp.int32),
            jax.ShapeDtypeStruct((1, LANES), F32),
        ],
        scratch_shapes=[pltpu.VMEM((1, LANES), F32)],
        compiler_params=_cparams(("arbitrary",)),
        name="router",
    )(x1, norm_w, router_w_pad, router_b_pad)


def _dispatch_kernel(pos_ref, h_ref, o_ref, sem, *, tt):
    def copy(t, k):
        src = pl.multiple_of(t * ROW_CHUNKS, ROW_CHUNKS)
        dst = pl.multiple_of(pos_ref[t * TOP_K + k] * ROW_CHUNKS, ROW_CHUNKS)
        return pltpu.make_async_copy(h_ref.at[pl.ds(src, ROW_CHUNKS), :], o_ref.at[pl.ds(dst, ROW_CHUNKS), :], sem)

    def issue(t, c):
        for k in range(TOP_K):
            copy(t, k).start()
        return c

    lax.fori_loop(0, tt, issue, 0, unroll=8)

    for _ in range(TOP_K):
        pltpu.make_async_copy(h_ref, o_ref.at[pl.ds(0, tt * ROW_CHUNKS), :], sem).wait()


def _dispatch(h2, pos_flat, tt=256):
    t = h2.shape[0] // ROW_CHUNKS
    tt = min(tt, t)
    kern = functools.partial(_dispatch_kernel, tt=tt)
    return pl.pallas_call(
        kern,
        grid=(t // tt,),
        in_specs=[
            pl.BlockSpec((tt * TOP_K,), lambda i: (i,), memory_space=pltpu.SMEM),
            pl.BlockSpec((tt * ROW_CHUNKS, LANES), lambda i: (i, 0)),
        ],
        out_specs=pl.BlockSpec(memory_space=pl.ANY),
        out_shape=jax.ShapeDtypeStruct((t * TOP_K * ROW_CHUNKS, LANES), F32),
        scratch_shapes=[pltpu.SemaphoreType.DMA(())],
        compiler_params=_cparams(("arbitrary",)),
        name="dispatch",
    )(pos_flat, h2)


def _experts_kernel(tile_ref, exp_ref, first_ref, efirst_ref, valid_ref, start_ref,
                    x_ref, wu_ref, bu_ref, wd_ref, bd_ref, o_ref, wub_ref, wdb_ref, *, tm):
    w = pl.program_id(0)

    @pl.when(efirst_ref[w] == 1)
    def _():
        for c0 in range(0, 2 * D_EXPERT, 512):
            wub_ref[:, c0:c0 + 512] = wu_ref[0, :, c0:c0 + 512].astype(BF16)
        for c0 in range(0, D_MODEL, 512):
            wdb_ref[:, c0:c0 + 512] = wd_ref[0, :, c0:c0 + 512].astype(BF16)

    @pl.when(valid_ref[w] == 1)
    def _():
        e = exp_ref[w]
        x = pltpu.einshape("(rc)l->r(cl)", x_ref[...], c=ROW_CHUNKS).astype(BF16)
        gu = jnp.dot(x, wub_ref[...], preferred_element_type=F32) + bu_ref[0]
        g = jnp.minimum(gu[:, :D_EXPERT], SWIGLU_LIMIT)
        u = jnp.clip(gu[:, D_EXPERT:], -SWIGLU_LIMIT, SWIGLU_LIMIT)
        act = (u + 1.0) * (g * _sigmoid(SWIGLU_ALPHA * g))
        out = jnp.dot(act.astype(BF16), wdb_ref[...], preferred_element_type=F32) + bd_ref[0]
        rows = tile_ref[w] * tm + lax.broadcasted_iota(jnp.int32, (tm, 1), 0)
        mine = (rows >= start_ref[e]) & (rows < start_ref[e + 1])

        def as_tiles(v):
            return pltpu.einshape("r(cl)->(rc)l", v, c=ROW_CHUNKS)

        @pl.when(first_ref[w] == 1)
        def _():
            o_ref[...] = as_tiles(jnp.where(mine, out, 0.0))

        @pl.when(first_ref[w] == 0)
        def _():
            o_ref[...] = as_tiles(jnp.where(mine, out, _tiles_to_rows(o_ref, tm)))


def _experts(x_sorted, item_tile, item_exp, item_first, item_efirst, item_valid, starts, w_up, b_up, w_down, b_down, tm):
    n = x_sorted.shape[0] // ROW_CHUNKS
    n_items = item_tile.shape[0]
    kern = functools.partial(_experts_kernel, tm=tm)
    grid_spec = pltpu.PrefetchScalarGridSpec(
        num_scalar_prefetch=6,
        grid=(n_items,),
        in_specs=[
            pl.BlockSpec((tm * ROW_CHUNKS, LANES), lambda w, tl, ex, *_: (tl[w], 0)),
            pl.BlockSpec((1, D_MODEL, 2 * D_EXPERT), lambda w, tl, ex, *_: (ex[w], 0, 0)),
            pl.BlockSpec((1, 1, 2 * D_EXPERT), lambda w, tl, ex, *_: (ex[w], 0, 0)),
            pl.BlockSpec((1, D_EXPERT, D_MODEL), lambda w, tl, ex, *_: (ex[w], 0, 0)),
            pl.BlockSpec((1, 1, D_MODEL), lambda w, tl, ex, *_: (ex[w], 0, 0)),
        ],
        out_specs=pl.BlockSpec((tm * ROW_CHUNKS, LANES), lambda w, tl, ex, *_: (tl[w], 0)),
        scratch_shapes=[pltpu.VMEM((D_MODEL, 2 * D_EXPERT), BF16), pltpu.VMEM((D_EXPERT, D_MODEL), BF16)],
    )
    return pl.pallas_call(
        kern,
        grid_spec=grid_spec,
        out_shape=jax.ShapeDtypeStruct((n * ROW_CHUNKS, LANES), F32),
        compiler_params=_cparams(("arbitrary",)),
        name="experts",
    )(item_tile, item_exp, item_first, item_efirst, item_valid, starts, x_sorted, w_up, b_up, w_down, b_down)


def _combine_kernel(pos_ref, posn_ref, gate_ref, x_ref, p_ref, pnw_ref, wpg_ref, wpp_ref, fnw_ref, ys_ref, o_ref,
                    rows_ref, sem, *, tt):
    i = pl.program_id(0)
    n = pl.num_programs(0)
    slot = i % 2

    def gather(idx_ref, dst_slot):
        def issue(t, c):
            for k in range(TOP_K):
                src = pl.multiple_of(idx_ref[t * TOP_K + k] * ROW_CHUNKS, ROW_CHUNKS)
                dst = pl.multiple_of((k * tt + t) * ROW_CHUNKS, ROW_CHUNKS)
                pltpu.make_async_copy(ys_ref.at[pl.ds(src, ROW_CHUNKS), :],
                                      rows_ref.at[dst_slot, pl.ds(dst, ROW_CHUNKS), :], sem.at[dst_slot]).start()
            return c

        lax.fori_loop(0, tt, issue, 0, unroll=8)

    @pl.when(i == 0)
    def _():
        gather(pos_ref, 0)

    @pl.when(i + 1 < n)
    def _():
        gather(posn_ref, 1 - slot)

    pltpu.make_async_copy(ys_ref.at[pl.ds(0, TOP_K * tt * ROW_CHUNKS), :], rows_ref.at[slot], sem.at[slot]).wait()

    gate = gate_ref[...]
    x = x_ref[...]
    for k in range(TOP_K):
        tiles = rows_ref[slot, pl.ds(k * tt * ROW_CHUNKS, tt * ROW_CHUNKS), :]
        x = x + gate[:, k:k + 1] * pltpu.einshape("(rc)l->r(cl)", tiles, c=ROW_CHUNKS)

    hg = _rms(x, pnw_ref[...]).astype(BF16)
    pg = _sigmoid(jnp.dot(hg, wpg_ref[...], preferred_element_type=F32))
    x = x + pg * jnp.dot(p_ref[...].astype(BF16), wpp_ref[...], preferred_element_type=F32)
    o_ref[...] = _rms(x, fnw_ref[...])


def _combine(pos_flat, gate, x1, p2d, ple_nw, w_pg, w_pp, final_nw, y_sorted, tt=256):
    t = x1.shape[0]
    tt = min(tt, t)
    kern = functools.partial(_combine_kernel, tt=tt)
    vec = pl.BlockSpec((1, D_MODEL), lambda i: (0, 0))
    nt = t // tt
    return pl.pallas_call(
        kern,
        grid=(nt,),
        in_specs=[
            pl.BlockSpec((tt * TOP_K,), lambda i: (i,), memory_space=pltpu.SMEM),
            pl.BlockSpec((tt * TOP_K,), lambda i: (jnp.minimum(i + 1, nt - 1),), memory_space=pltpu.SMEM),
            pl.BlockSpec((tt, LANES), lambda i: (i, 0)),
            pl.BlockSpec((tt, D_MODEL), lambda i: (i, 0)),
            pl.BlockSpec((tt, PLE_DIM), lambda i: (i, 0)),
            vec,
            pl.BlockSpec((D_MODEL, D_MODEL), lambda i: (0, 0)),
            pl.BlockSpec((PLE_DIM, D_MODEL), lambda i: (0, 0)),
            vec,
            pl.BlockSpec(memory_space=pl.ANY),
        ],
        out_specs=pl.BlockSpec((tt, D_MODEL), lambda i: (i, 0)),
        out_shape=jax.ShapeDtypeStruct((t, D_MODEL), F32),
        scratch_shapes=[pltpu.VMEM((2, TOP_K * tt * ROW_CHUNKS, LANES), F32), pltpu.SemaphoreType.DMA((2,))],
        compiler_params=_cparams(("arbitrary",)),
        name="combine",
    )(pos_flat, pos_flat, gate, x1, p2d, ple_nw, w_pg, w_pp, final_nw, y_sorted)


def _routing_plan(counts, idx, rank, tm, n_rows):
    counts = counts.astype(jnp.int32)
    ends = jnp.cumsum(counts)
    starts = ends - counts
    pos = (starts[idx] + rank).reshape(-1)
    n_tiles = n_rows // tm
    n_items = n_tiles + N_EXPERTS - 1
    first_tile = starts // tm
    last_tile = jnp.maximum(ends - 1, 0) // tm
    items_e = jnp.where(counts > 0, last_tile - first_tile + 1, 0)
    item_end = jnp.cumsum(items_e)
    item_start = item_end - items_e
    total = item_end[-1]
    w = jnp.arange(n_items, dtype=jnp.int32)
    e_w = jnp.minimum(jnp.sum(w[:, None] >= item_end[None, :], axis=1).astype(jnp.int32), N_EXPERTS - 1)
    tile_w = first_tile[e_w] + (w - item_start[e_w])
    valid = w < total
    last = jnp.maximum(total - 1, 0)
    e_w = jnp.where(valid, e_w, e_w[last])
    tile_w = jnp.where(valid, tile_w, tile_w[last])
    prev_tile = jnp.concatenate([jnp.full((1,), -1, jnp.int32), tile_w[:-1]])
    first = (tile_w != prev_tile) & valid
    prev_e = jnp.concatenate([jnp.full((1,), -1, jnp.int32), e_w[:-1]])
    efirst = (e_w != prev_e) & valid
    starts_ext = jnp.concatenate([starts, ends[-1:]]).astype(jnp.int32)
    return (pos.astype(jnp.int32), tile_w.astype(jnp.int32), e_w, first.astype(jnp.int32), efirst.astype(jnp.int32),
            valid.astype(jnp.int32), starts_ext)


def kernel(x, p, mix_norm_w, w_in, lambda_q1, lambda_k1, lambda_q2, lambda_k2, da_head_norm_w, w_attn_branch, conv_w, conv_b, dt_bias, a_log, d_skip, ssd_norm_w, w_ssd_branch, w_out, moe_norm_w, router_w, router_b, w_up, b_up, w_down, b_down, ple_norm_w, w_ple_gate, w_ple_proj, final_norm_w):
    batch, seq, _ = x.shape
    t = batch * seq
    depth = w_in.shape[0]
    assert depth == 1, "the final RMSNorm is fused into the layer's last kernel"
    x2d = x.reshape(t, D_MODEL)

    log2e = math.log2(math.e)
    slopes = jnp.asarray([log2e * 2.0 ** (-8.0 * (h + 1) / DA_HEADS) for h in range(DA_HEADS)], F32)
    q_scale = log2e * DA_QK_DIM ** -0.5
    head_ids = jnp.arange(SSD_D_INNER, dtype=jnp.int32) // SSD_HEAD_DIM
    expand = (jnp.arange(LANES, dtype=jnp.int32)[:, None] == head_ids[None, :]).astype(BF16)

    def pad_lanes(v, fill=0.0):
        return jnp.pad(v.reshape(1, -1), ((0, 0), (0, LANES - v.shape[-1])), constant_values=fill)

    for i in range(depth):
        wi = w_in[i]
        w_main = jnp.concatenate([
            wi[:, _R_Z:_R_Z + SSD_D_INNER], wi[:, _R_Q:_R_Q + 1024] * q_scale, wi[:, _R_K:_R_K + 1024],
            wi[:, _R_GA:_R_GA + 1024], wi[:, _R_GS:_R_GS + 1024], wi[:, _R_XBC:_R_XBC + SSD_CONV_DIM],
            wi[:, _R_V:_R_V + 1024]], axis=1).astype(BF16)
        w_dt = jnp.pad(wi[:, _R_DT:_R_DT + SSD_HEADS], ((0, 0), (0, LANES - SSD_HEADS))).astype(BF16)

        proj, dt_raw, v_t = _in_proj(x2d, mix_norm_w[i].reshape(1, -1), w_main, w_dt)

        lambda_init = 0.8 - 0.6 * math.exp(-0.3 * i)
        lam = (jnp.exp(jnp.sum(lambda_q1[i] * lambda_k1[i])) - jnp.exp(jnp.sum(lambda_q2[i] * lambda_k2[i]))
               + lambda_init).reshape(1).astype(F32)
        y_attn = _diff_attn(proj, v_t, slopes, lam, da_head_norm_w[i].reshape(1, -1), batch, seq, lambda_init)

        y_ssd = _ssd(proj, dt_raw, conv_w[i], conv_b[i].reshape(1, -1), pad_lanes(dt_bias[i]), pad_lanes(a_log[i]),
                     jnp.repeat(d_skip[i], SSD_HEAD_DIM).reshape(1, -1), ssd_norm_w[i].reshape(1, -1), expand,
                     batch, seq)

        x1 = _merge(y_attn, y_ssd, proj, x2d, w_attn_branch[i].astype(BF16), w_ssd_branch[i].astype(BF16),
                    w_out[i].astype(BF16))

        rw = jnp.pad(router_w[i], ((0, 0), (0, LANES - N_EXPERTS)))
        h2, idx, gate, rank, counts = _router(x1, moe_norm_w[i].reshape(1, -1), rw, pad_lanes(router_b[i]))

        tm = 512
        n_rows = t * TOP_K
        pos, item_tile, item_exp, item_first, item_efirst, item_valid, starts = _routing_plan(
            counts[0, :N_EXPERTS], idx[:, :TOP_K], rank[:, :TOP_K], tm, n_rows)

        x_sorted = _dispatch(h2, pos)
        y_sorted = _experts(x_sorted, item_tile, item_exp, item_first, item_efirst, item_valid, starts,
                            w_up[i], b_up[i].reshape(N_EXPERTS, 1, -1),
                            w_down[i], b_down[i].reshape(N_EXPERTS, 1, -1), tm)

        x2d = _combine(pos, gate, x1, p[i].reshape(t, PLE_DIM), ple_norm_w[i].reshape(1, -1),
                       w_ple_gate[i].astype(BF16), w_ple_proj[i].astype(BF16), final_norm_w.reshape(1, -1), y_sorted)
    return x2d.reshape(batch, seq, D_MODEL)
```

```python
import functools
import math

import jax
import jax.numpy as jnp
from jax import lax
from jax.experimental import pallas as pl
from jax.experimental.pallas import tpu as pltpu

F32 = jnp.float32
BF16 = jnp.bfloat16

D_MODEL = 1024
CHUNK = 64
PLE_DIM = 256
RMS_EPS = 1e-6

DA_HEADS = 8
DA_QK_DIM = 64
DA_V_DIM = 128

SSD_D_INNER = 2048
SSD_HEAD_DIM = 64
SSD_HEADS = 32
SSD_GROUPS = 4
SSD_STATE = 128
SSD_CONV = 4
SSD_CONV_DIM = 3072

N_EXPERTS = 32
TOP_K = 4
D_EXPERT = 1024
SWIGLU_LIMIT = 7.0
SWIGLU_ALPHA = 1.702

LANES = 128

COL_Z = 0
COL_Q = 2048
COL_K = 3072
COL_GA = 4096
COL_GS = 5120
COL_XBC = 6144
PROJ_W = 9216

ATTN_TILE = 256
ATTN_Q_TILE = 512
SUM_ROWS = 16

_R_Q, _R_K, _R_V, _R_Z, _R_XBC, _R_DT, _R_GA, _R_GS = 0, 1024, 2048, 3072, 5120, 8192, 8224, 9248

VMEM_LIMIT = 56 * 1024 * 1024


def _cparams(sem):
    return pltpu.CompilerParams(dimension_semantics=sem, vmem_limit_bytes=VMEM_LIMIT)


def _rms(x, w):
    return x * lax.rsqrt(jnp.mean(x * x, axis=-1, keepdims=True) + RMS_EPS) * w


ROW_CHUNKS = D_MODEL // LANES


def _rows_to_tiles(ref, val, n):
    for c in range(ROW_CHUNKS):
        ref[pl.ds(c, n, stride=ROW_CHUNKS), :] = val[:, c * LANES:(c + 1) * LANES]


def _tiles_to_rows(ref, n, base=0):
    return jnp.concatenate([ref[pl.ds(base + c, n, stride=ROW_CHUNKS), :] for c in range(ROW_CHUNKS)], axis=1)


def _sigmoid(x):
    return 1.0 / (1.0 + jnp.exp2(x * (-math.log2(math.e))))


def _inproj_kernel(x_ref, nw_ref, w_ref, wdt_ref, o_ref, dt_ref, vt_ref, h_ref, *, tm, n_proj):
    j = pl.program_id(1)

    @pl.when(j == 0)
    def _():
        hb = _rms(x_ref[...], nw_ref[...]).astype(BF16)
        h_ref[...] = hb
        dt_ref[...] = jnp.dot(hb, wdt_ref[...], preferred_element_type=F32)

    acc = jnp.dot(h_ref[...], w_ref[...], preferred_element_type=F32)

    @pl.when(j < n_proj)
    def _():
        o_ref[...] = acc.astype(BF16)

    @pl.when(j == n_proj)
    def _():
        vt = acc.T
        for c in range(tm // ATTN_TILE):
            vt_ref[c] = vt[:, c * ATTN_TILE:(c + 1) * ATTN_TILE].astype(BF16)


def _in_proj(x2d, norm_w, w_main, w_dt, tm=2048):
    t = x2d.shape[0]
    tm = min(tm, t)
    tn = DA_HEADS * DA_V_DIM
    n_proj = PROJ_W // tn
    return pl.pallas_call(
        functools.partial(_inproj_kernel, tm=tm, n_proj=n_proj),
        grid=(t // tm, n_proj + 1),
        in_specs=[
            pl.BlockSpec((tm, D_MODEL), lambda i, j: (i, 0)),
            pl.BlockSpec((1, D_MODEL), lambda i, j: (0, 0)),
            pl.BlockSpec((D_MODEL, tn), lambda i, j: (0, j)),
            pl.BlockSpec((D_MODEL, LANES), lambda i, j: (0, 0)),
        ],
        out_specs=[
            pl.BlockSpec((tm, tn), lambda i, j: (i, jnp.minimum(j, n_proj - 1))),
            pl.BlockSpec((tm, LANES), lambda i, j: (i, 0)),
            pl.BlockSpec((tm // ATTN_TILE, tn, ATTN_TILE), lambda i, j: (i, 0, 0)),
        ],
        out_shape=[
            jax.ShapeDtypeStruct((t, PROJ_W), BF16),
            jax.ShapeDtypeStruct((t, LANES), F32),
            jax.ShapeDtypeStruct((t // ATTN_TILE, tn, ATTN_TILE), BF16),
        ],
        scratch_shapes=[pltpu.VMEM((tm, D_MODEL), BF16)],
        compiler_params=_cparams(("parallel", "arbitrary")),
        name="in_proj",
    )(x2d, norm_w, w_main, w_dt)


def _attn_kernel(slopes_ref, lam_ref, q_ref, k_ref, vt_ref, nw_ref, o_ref,
                 bias_ref, ka_ref, kb_ref, qc_ref, s_ref, p_ref, al_ref, m_ref, a_ref, *, tq, tk, out_scale):
    h = pl.program_id(1)
    i = pl.program_id(2)
    slope = slopes_ref[h]
    lam = lam_ref[0]
    r = tq // tk
    n_past = r * i
    n_tiles = n_past + r
    seq = k_ref.shape[0]
    aux = 16

    @pl.when(i == 0)
    def _():
        kk = lax.broadcasted_iota(jnp.int32, (tk, tq), 0)
        qq = lax.broadcasted_iota(jnp.int32, (tk, tq), 1)
        bias_ref[0] = jnp.zeros((tk, tq), F32)
        for d in range(r):
            kd = kk + d * tk
            true = jnp.where((kd // CHUNK) <= (qq // CHUNK), -slope * jnp.abs(qq - kd).astype(F32), -jnp.inf)
            bias_ref[1 + d] = true + slope * (qq - kk).astype(F32)
        rows = 512
        lane = lax.broadcasted_iota(jnp.int32, (rows, LANES), 1)
        krel = (lax.broadcasted_iota(jnp.int32, (rows, LANES), 0) % tk).astype(F32)
        lo = lane - DA_QK_DIM
        aux_a = jnp.where(lo < 3, krel, jnp.where(lo < 6, 1.0, 0.0))
        aux_b = jnp.where(lane < 3, krel, jnp.where(lane < 6, 1.0, 0.0))
        for r0 in range(0, seq, rows):
            k = k_ref[r0:r0 + rows, :].astype(F32)
            ka_ref[r0:r0 + rows, :] = jnp.where(lane < DA_QK_DIM, k, aux_a).astype(BF16)
            kb_ref[r0:r0 + rows, :] = jnp.where(lane >= DA_QK_DIM, k, aux_b).astype(BF16)

    qt = q_ref[...].astype(F32).T.astype(BF16)
    arow = lax.broadcasted_iota(jnp.int32, (aux, tq), 0)
    qrel = lax.broadcasted_iota(jnp.int32, (aux, tq), 1).astype(F32)
    pieces = _split3(jnp.full((aux, tq), slope, F32)) + _split3(-slope * qrel)
    extra = jnp.zeros((aux, tq), F32)
    for n, piece in enumerate(pieces):
        extra = jnp.where(arow == n, piece.astype(F32), extra)
    extra = extra.astype(BF16)
    pad = jnp.zeros((DA_QK_DIM - aux, tq), BF16)
    qc_ref[:, 0:tq] = jnp.concatenate([qt[0:DA_QK_DIM], extra, pad], axis=0)
    qc_ref[:, tq:2 * tq] = jnp.concatenate([extra, pad, qt[DA_QK_DIM:]], axis=0)

    m_ref[...] = jnp.full(m_ref.shape, -jnp.inf, F32)
    a_ref[...] = jnp.zeros(a_ref.shape, F32)

    def scores(j):
        rows = pl.ds(pl.multiple_of(j * tk, tk), tk)
        return (jnp.dot(ka_ref[rows, :], qc_ref[:, 0:tq], preferred_element_type=F32),
                jnp.dot(kb_ref[rows, :], qc_ref[:, tq:2 * tq], preferred_element_type=F32))

    def store_scores(blocks):
        s_ref[:, 0:tq] = blocks[0]
        s_ref[:, tq:2 * tq] = blocks[1]

    def softmax_step(sel, cj):
        for g in range(2 * tq // LANES):
            cols = slice(g * LANES, (g + 1) * LANES)
            bcols = slice((g * LANES) % tq, (g * LANES) % tq + LANES)
            s = s_ref[:, cols]
            if sel is not None:
                s = s + bias_ref[sel, :, bcols]
            m_prev = m_ref[:, cols]
            m_cur = jnp.maximum(m_prev, jnp.max(s, axis=0, keepdims=True) + cj)
            alpha = jnp.exp2(m_prev - m_cur)
            p = jnp.exp2(s + (cj - m_cur))
            m_ref[:, cols] = m_cur
            al_ref[:, cols] = alpha
            p_ref[:, cols] = p.astype(BF16)

    def pv_step(j):
        vt = jnp.concatenate([vt_ref[j], jnp.ones((SUM_ROWS, tk), BF16)], axis=0)
        for g in range(2 * tq // 256):
            cols = slice(g * 256, (g + 1) * 256)
            a_ref[:, cols] = al_ref[:, cols] * a_ref[:, cols] + jnp.dot(vt, p_ref[:, cols], preferred_element_type=F32)

    def past_cj(j):
        return -slope * (i * tq - j * tk).astype(F32)

    store_scores(scores(0))
    s_next = scores(1)
    softmax_step(jnp.where(i > 0, 0, 1), jnp.where(i > 0, past_cj(0), 0.0))
    store_scores(s_next)

    def body(j, carry):
        s_next = scores(j + 1)
        pv_step(j - 1)
        softmax_step(None, past_cj(j))
        store_scores(s_next)
        return carry

    lax.fori_loop(1, n_past, body, 0)

    for d in range(r):
        t = n_past + d

        def overlap_tile(t=t, d=d):
            if d + 1 < r:
                s_next = scores(t + 1)
            pv_step(t - 1)
            softmax_step(1 + d, 0.0)
            if d + 1 < r:
                store_scores(s_next)

        if d == 0:
            pl.when(i > 0)(overlap_tile)
        else:
            overlap_tile()
    pv_step(n_tiles - 1)

    on = a_ref[0:DA_V_DIM, :] / a_ref[DA_V_DIM:DA_V_DIM + 1, :]
    ot = on[:, :tq] - lam * on[:, tq:]
    ot = ot * lax.rsqrt(jnp.mean(ot * ot, axis=0, keepdims=True) + RMS_EPS)
    o_ref[...] = (ot.T * nw_ref[...] * out_scale).astype(BF16)


def _diff_attn(proj, v_t, slopes, lam, head_norm_w, batch, seq, lambda_init):
    tk = ATTN_TILE
    tq = min(ATTN_Q_TILE, seq)
    nq = seq // tq
    nk = seq // tk
    t = batch * seq
    qb, kb = COL_Q // LANES, COL_K // LANES
    assert tq % tk == 0 and tq >= 2 * tk, "the peeled first/last key tiles need at least two tiles per step"
    kern = functools.partial(_attn_kernel, tq=tq, tk=tk, out_scale=1.0 - lambda_init)
    smem = pl.BlockSpec(memory_space=pltpu.SMEM)
    return pl.pallas_call(
        kern,
        grid=(batch, DA_HEADS, nq),
        in_specs=[
            smem, smem,
            pl.BlockSpec((tq, LANES), lambda b, h, i: (b * nq + i, qb + h)),
            pl.BlockSpec((seq, LANES), lambda b, h, i: (b, kb + h)),
            pl.BlockSpec((nk, DA_V_DIM, tk), lambda b, h, i: (b, h, 0)),
            pl.BlockSpec((1, LANES), lambda b, h, i: (0, 0)),
        ],
        out_specs=pl.BlockSpec((tq, LANES), lambda b, h, i: (b * nq + i, h)),
        out_shape=jax.ShapeDtypeStruct((t, DA_HEADS * DA_V_DIM), BF16),
        scratch_shapes=[
            pltpu.VMEM((1 + tq // tk, tk, tq), F32),
            pltpu.VMEM((seq, LANES), BF16),
            pltpu.VMEM((seq, LANES), BF16),
            pltpu.VMEM((LANES, 2 * tq), BF16),
            pltpu.VMEM((tk, 2 * tq), F32),
            pltpu.VMEM((tk, 2 * tq), BF16),
            pltpu.VMEM((1, 2 * tq), F32),
            pltpu.VMEM((1, 2 * tq), F32),
            pltpu.VMEM((DA_V_DIM + SUM_ROWS, 2 * tq), F32),
        ],
        compiler_params=_cparams(("parallel", "parallel", "arbitrary")),
        name="diff_attn",
    )(slopes, lam, proj, proj, v_t, head_norm_w)


def _split2(x):
    hi = x.astype(BF16)
    lo = (x - hi.astype(F32)).astype(BF16)
    return hi, lo


def _split3(x):
    hi = x.astype(BF16)
    r = x - hi.astype(F32)
    mid = r.astype(BF16)
    lo = (r - mid.astype(F32)).astype(BF16)
    return hi, mid, lo


def _ssd_kernel(xbc_ref, z_ref, dt_ref, cw_ref, cb_ref, dtb_ref, alog_ref, dsk_ref, nw_ref, e_ref,
                o_ref, ubuf, shift_ref, xs_scr, bc_scr, acum_scr, acb, dfs_scr, xdt_scr, xw_scr, y_scr, state, *, tr):
    i = pl.program_id(1)
    nch = tr // CHUNK
    gw = SSD_D_INNER // SSD_GROUPS
    pad = 8

    head = 16

    @pl.when(i == 0)
    def _():
        ubuf[0:pad, :] = jnp.zeros((pad, SSD_CONV_DIM), F32)
        state[...] = jnp.zeros_like(state)
        r_s = lax.broadcasted_iota(jnp.int32, (tr, tr), 0)
        c_s = lax.broadcasted_iota(jnp.int32, (tr, tr), 1)
        for d in range(1, SSD_CONV):
            shift_ref[(d - 1) * tr:d * tr, :] = jnp.where(c_s == r_s - d, 1.0, 0.0).astype(BF16)

    ubuf[pad:pad + head, :] = xbc_ref[0:head, :].astype(F32)
    cs = 512
    for c0 in range(0, SSD_CONV_DIM, cs):
        ub = xbc_ref[:, c0:c0 + cs]
        sh = jnp.dot(shift_ref[...], ub, preferred_element_type=F32)
        acc = cb_ref[:, c0:c0 + cs] + cw_ref[SSD_CONV - 1:SSD_CONV, c0:c0 + cs] * ub.astype(F32)
        for d in range(1, SSD_CONV):
            acc = acc + cw_ref[SSD_CONV - 1 - d:SSD_CONV - d, c0:c0 + cs] * sh[(d - 1) * tr:d * tr, :]
        u = acc * _sigmoid(acc)
        acc_h = jnp.broadcast_to(cb_ref[:, c0:c0 + cs], (head, cs))
        for k in range(SSD_CONV):
            off = pad - (SSD_CONV - 1) + k
            acc_h = acc_h + cw_ref[k:k + 1, c0:c0 + cs] * ubuf[off:off + head, c0:c0 + cs]
        u_h = acc_h * _sigmoid(acc_h)
        if c0 < SSD_D_INNER:
            xs_scr[:, c0:c0 + cs] = u
            xs_scr[0:head, c0:c0 + cs] = u_h
        else:
            bc_scr[:, c0 - SSD_D_INNER:c0 - SSD_D_INNER + cs] = u.astype(BF16)
            bc_scr[0:head, c0 - SSD_D_INNER:c0 - SSD_D_INNER + cs] = u_h.astype(BF16)
    ubuf[0:pad, :] = xbc_ref[tr - 2 * pad:tr, :].astype(F32)[pad:, :]

    xdt_in = dt_ref[...] + dtb_ref[...]
    dtv = jnp.maximum(xdt_in, 0.0) + jnp.log1p(jnp.exp(-jnp.abs(xdt_in)))
    da = dtv * (-jnp.exp(alog_ref[...]))
    r_i = lax.broadcasted_iota(jnp.int32, (tr, tr), 0)
    c_i = lax.broadcasted_iota(jnp.int32, (tr, tr), 1)
    tril = jnp.where((c_i <= r_i) & ((c_i // CHUNK) == (r_i // CHUNK)), 1.0, 0.0).astype(BF16)
    acum = sum(jnp.dot(tril, piece, preferred_element_type=F32) for piece in _split3(da))
    acum_scr[...] = acum

    a_hi, a_lo = _split2(acum)
    d_hi, d_lo = _split2(dtv)
    for c0 in range(0, SSD_D_INNER, cs):
        e = e_ref[:, c0:c0 + cs]
        ab = jnp.dot(a_hi, e, preferred_element_type=F32) + jnp.dot(a_lo, e, preferred_element_type=F32)
        db = jnp.dot(d_hi, e, preferred_element_type=F32) + jnp.dot(d_lo, e, preferred_element_type=F32)
        acb[:, c0:c0 + cs] = ab
        dfs_scr[:, c0:c0 + cs] = jnp.exp(ab)
        xdt = xs_scr[:, c0:c0 + cs] * db
        xdt_scr[:, c0:c0 + cs] = xdt.astype(BF16)
        for c in range(nch):
            r0 = c * CHUNK
            last = ab[r0 + CHUNK - 1:r0 + CHUNK, :]
            xw_scr[r0:r0 + CHUNK, c0:c0 + cs] = (xdt[r0:r0 + CHUNK, :] * jnp.exp(last - ab[r0:r0 + CHUNK, :])).astype(BF16)

    lane2 = lax.broadcasted_iota(jnp.int32, (CHUNK, LANES), 1)
    row2 = lax.broadcasted_iota(jnp.int32, (CHUNK, LANES), 0)
    causal2 = row2 >= (lane2 % SSD_HEAD_DIM)
    lo_half = lane2 < SSD_HEAD_DIM
    zpad = jnp.zeros((LANES - CHUNK, LANES), F32)

    def chunk_body(c, carry):
        r0 = pl.multiple_of(c * CHUNK, CHUNK)
        rows = pl.ds(r0, CHUNK)
        a_c = acum_scr[rows, :]
        a_t = jnp.concatenate([a_c, zpad], axis=0).T
        a_t_r = pltpu.roll(a_t, SSD_HEAD_DIM, 1)
        last_row = acb[pl.ds(r0 + CHUNK - 1, 1), :]
        for g in range(SSD_GROUPS):
            bg = bc_scr[rows, g * SSD_STATE:(g + 1) * SSD_STATE]
            cg = bc_scr[rows, SSD_GROUPS * SSD_STATE + g * SSD_STATE:SSD_GROUPS * SSD_STATE + (g + 1) * SSD_STATE]
            cbt = lax.dot_general(cg, jnp.concatenate([bg, bg], axis=0), (((1,), (1,)), ((), ())),
                                  preferred_element_type=F32)
            g0 = g * gw
            st = state[g]
            yoff = jnp.dot(cg, st.astype(BF16), preferred_element_type=F32) * dfs_scr[rows, g0:g0 + gw]
            for jj in range(gw // LANES):
                pidx = g * (gw // LANES) + jj
                l0 = pidx * LANES
                arow = a_t[2 * pidx:2 * pidx + 1, :] + a_t_r[2 * pidx + 1:2 * pidx + 2, :]
                seg = acb[rows, l0:l0 + LANES] - arow
                decay = jnp.exp(jnp.where(causal2, seg, -jnp.inf))
                mp = (cbt * decay).astype(BF16)
                xd = xdt_scr[rows, l0:l0 + LANES]
                zb = jnp.zeros_like(xd)
                bd = jnp.concatenate([jnp.where(lo_half, xd, zb), jnp.where(lo_half, zb, xd)], axis=0)
                yd = jnp.dot(mp, bd, preferred_element_type=F32)
                y_scr[rows, l0:l0 + LANES] = yd + yoff[:, jj * LANES:(jj + 1) * LANES]
            cd = jnp.exp(last_row[:, g0:g0 + gw])
            upd = lax.dot_general(bg, xw_scr[rows, g0:g0 + gw], (((0,), (0,)), ((), ())),
                                  preferred_element_type=F32)
            state[g] = st * cd + upd
        return carry

    lax.fori_loop(0, nch, chunk_body, 0, unroll=True)

    for g in range(SSD_GROUPS):
        g0 = g * gw
        y = y_scr[:, g0:g0 + gw] + dsk_ref[:, g0:g0 + gw] * xs_scr[:, g0:g0 + gw]
        zz = z_ref[:, g0:g0 + gw].astype(F32)
        y = y * (zz * _sigmoid(zz))
        o_ref[:, g0:g0 + gw] = _rms(y, nw_ref[:, g0:g0 + gw]).astype(BF16)


def _ssd(proj, dt_raw, conv_w, conv_b, dt_bias, a_log, d_skip_b, norm_w, expand, batch, seq, tr=256):
    t = batch * seq
    nr = seq // tr
    kern = functools.partial(_ssd_kernel, tr=tr)
    const = lambda shape: pl.BlockSpec(shape, lambda b, i: (0, 0))
    return pl.pallas_call(
        kern,
        grid=(batch, nr),
        in_specs=[
            pl.BlockSpec((tr, SSD_CONV_DIM), lambda b, i: (b * nr + i, COL_XBC // SSD_CONV_DIM)),
            pl.BlockSpec((tr, SSD_D_INNER), lambda b, i: (b * nr + i, COL_Z // SSD_D_INNER)),
            pl.BlockSpec((tr, LANES), lambda b, i: (b * nr + i, 0)),
            const((SSD_CONV, SSD_CONV_DIM)),
            const((1, SSD_CONV_DIM)),
            const((1, LANES)),
            const((1, LANES)),
            const((1, SSD_D_INNER)),
            const((1, SSD_D_INNER)),
            const((LANES, SSD_D_INNER)),
        ],
        out_specs=pl.BlockSpec((tr, SSD_D_INNER), lambda b, i: (b * nr + i, 0)),
        out_shape=jax.ShapeDtypeStruct((t, SSD_D_INNER), BF16),
        scratch_shapes=[
            pltpu.VMEM((8 + 16, SSD_CONV_DIM), F32),
            pltpu.VMEM(((SSD_CONV - 1) * tr, tr), BF16),
            pltpu.VMEM((tr, SSD_D_INNER), F32),
            pltpu.VMEM((tr, 2 * SSD_GROUPS * SSD_STATE), BF16),
            pltpu.VMEM((tr, LANES), F32),
            pltpu.VMEM((tr, SSD_D_INNER), F32),
            pltpu.VMEM((tr, SSD_D_INNER), F32),
            pltpu.VMEM((tr, SSD_D_INNER), BF16),
            pltpu.VMEM((tr, SSD_D_INNER), BF16),
            pltpu.VMEM((tr, SSD_D_INNER), F32),
            pltpu.VMEM((SSD_GROUPS, SSD_STATE, SSD_D_INNER // SSD_GROUPS), F32),
        ],
        compiler_params=_cparams(("parallel", "arbitrary")),
        name="ssd",
    )(proj, proj, dt_raw, conv_w, conv_b, dt_bias, a_log, d_skip_b, norm_w, expand)


def _merge_kernel(ya_ref, ys_ref, ga_ref, gs_ref, x_ref, wa_ref, wb_ref, wo_ref, o_ref):
    a = jnp.dot(ya_ref[...], wa_ref[...], preferred_element_type=F32)
    s = jnp.dot(ys_ref[...], wb_ref[...], preferred_element_type=F32)
    merged = _sigmoid(ga_ref[...].astype(F32)) * a + _sigmoid(gs_ref[...].astype(F32)) * s
    o_ref[...] = x_ref[...] + jnp.dot(merged.astype(BF16), wo_ref[...], preferred_element_type=F32)


def _merge(y_attn, y_ssd, proj, x2d, w_a, w_b, w_o, tm=512):
    t = x2d.shape[0]
    tm = min(tm, t)
    full = lambda r, c: pl.BlockSpec((r, c), lambda i: (0, 0))
    return pl.pallas_call(
        _merge_kernel,
        grid=(t // tm,),
        in_specs=[
            pl.BlockSpec((tm, D_MODEL), lambda i: (i, 0)),
            pl.BlockSpec((tm, SSD_D_INNER), lambda i: (i, 0)),
            pl.BlockSpec((tm, D_MODEL), lambda i: (i, COL_GA // D_MODEL)),
            pl.BlockSpec((tm, D_MODEL), lambda i: (i, COL_GS // D_MODEL)),
            pl.BlockSpec((tm, D_MODEL), lambda i: (i, 0)),
            full(D_MODEL, D_MODEL), full(SSD_D_INNER, D_MODEL), full(D_MODEL, D_MODEL),
        ],
        out_specs=pl.BlockSpec((tm, D_MODEL), lambda i: (i, 0)),
        out_shape=jax.ShapeDtypeStruct((t, D_MODEL), F32),
        compiler_params=_cparams(("parallel",)),
        name="merge",
    )(y_attn, y_ssd, proj, proj, x2d, w_a, w_b, w_o)


def _router_kernel(x_ref, nw_ref, rw_ref, rb_ref, h_ref, idx_ref, gate_ref, rank_ref, cnt_ref, carry_ref, *, tt):
    i = pl.program_id(0)

    @pl.when(i == 0)
    def _():
        carry_ref[...] = jnp.zeros_like(carry_ref)

    h = _rms(x_ref[...], nw_ref[...])
    h_ref[...] = pltpu.einshape("r(cl)->(rc)l", h, c=ROW_CHUNKS)
    h_hi, h_lo = _split2(h)
    logits = (jnp.dot(h_hi, rw_ref[0], preferred_element_type=F32) + jnp.dot(h_hi, rw_ref[1], preferred_element_type=F32)
              + jnp.dot(h_lo, rw_ref[0], preferred_element_type=F32)) + rb_ref[...]
    lane = lax.broadcasted_iota(jnp.int32, (tt, LANES), 1)
    work = jnp.where(lane < N_EXPERTS, logits, -jnp.inf)

    vals, idxs, hots = [], [], []
    for _ in range(TOP_K):
        m = jnp.max(work, axis=-1, keepdims=True)
        idx = jnp.min(jnp.where(work == m, lane, LANES), axis=-1, keepdims=True)
        hot = lane == idx
        vals.append(m)
        idxs.append(idx)
        hots.append(hot)
        work = jnp.where(hot, -jnp.inf, work)

    exps = [jnp.exp(v - vals[0]) for v in vals]
    denom = exps[0] + exps[1] + exps[2] + exps[3]

    hot_sum = sum(jnp.where(hot, 1.0, 0.0) for hot in hots)
    r_i = lax.broadcasted_iota(jnp.int32, (tt, tt), 0)
    c_i = lax.broadcasted_iota(jnp.int32, (tt, tt), 1)
    strict = jnp.where(c_i < r_i, 1.0, 0.0).astype(BF16)
    prefix = jnp.dot(strict, hot_sum.astype(BF16), preferred_element_type=F32) + carry_ref[...]

    idx_out = jnp.zeros((tt, LANES), jnp.int32)
    rank_out = jnp.zeros((tt, LANES), jnp.int32)
    gate_out = jnp.zeros((tt, LANES), F32)
    for k in range(TOP_K):
        rank_k = jnp.sum(jnp.where(hots[k], prefix, 0.0), axis=-1, keepdims=True).astype(jnp.int32)
        sel = lane == k
        idx_out = jnp.where(sel, idxs[k], idx_out)
        rank_out = jnp.where(sel, rank_k, rank_out)
        gate_out = jnp.where(sel, exps[k] / denom, gate_out)
    idx_ref[...] = idx_out
    rank_ref[...] = rank_out
    gate_ref[...] = gate_out

    carry_ref[...] = carry_ref[...] + jnp.sum(hot_sum, axis=0, keepdims=True)
    cnt_ref[...] = carry_ref[...]


def _router(x1, norm_w, router_w_pad, router_b_pad, tt=512):
    t = x1.shape[0]
    tt = min(tt, t)
    kern = functools.partial(_router_kernel, tt=tt)
    row = pl.BlockSpec((tt, LANES), lambda i: (i, 0))
    return pl.pallas_call(
        kern,
        grid=(t // tt,),
        in_specs=[
            pl.BlockSpec((tt, D_MODEL), lambda i: (i, 0)),
            pl.BlockSpec((1, D_MODEL), lambda i: (0, 0)),
            pl.BlockSpec((2, D_MODEL, LANES), lambda i: (0, 0, 0)),
            pl.BlockSpec((1, LANES), lambda i: (0, 0)),
        ],
        out_specs=[pl.BlockSpec((tt * ROW_CHUNKS, LANES), lambda i: (i, 0)), row, row, row,
                   pl.BlockSpec((1, LANES), lambda i: (0, 0))],
        out_shape=[
            jax.ShapeDtypeStruct((t * ROW_CHUNKS, LANES), F32),
            jax.ShapeDtypeStruct((t, LANES), jnp.int32),
            jax.ShapeDtypeStruct((t, LANES), F32),
            jax.ShapeDtypeStruct((t, LANES), jnp.int32),
            jax.ShapeDtypeStruct((1, LANES), F32),
        ],
        scratch_shapes=[pltpu.VMEM((1, LANES), F32)],
        compiler_params=_cparams(("arbitrary",)),
        name="router",
    )(x1, norm_w, router_w_pad, router_b_pad)


def _dispatch_kernel(pos_ref, h_ref, o_ref, sem, *, tt):
    def copy(t, k):
        src = pl.multiple_of(t * ROW_CHUNKS, ROW_CHUNKS)
        dst = pl.multiple_of(pos_ref[t * TOP_K + k] * ROW_CHUNKS, ROW_CHUNKS)
        return pltpu.make_async_copy(h_ref.at[pl.ds(src, ROW_CHUNKS), :], o_ref.at[pl.ds(dst, ROW_CHUNKS), :], sem)

    def issue(t, c):
        for k in range(TOP_K):
            copy(t, k).start()
        return c

    lax.fori_loop(0, tt, issue, 0, unroll=8)

    for _ in range(TOP_K):
        pltpu.make_async_copy(h_ref, o_ref.at[pl.ds(0, tt * ROW_CHUNKS), :], sem).wait()


def _dispatch(h2, pos_flat, tt=512):
    t = h2.shape[0] // ROW_CHUNKS
    tt = min(tt, t)
    kern = functools.partial(_dispatch_kernel, tt=tt)
    return pl.pallas_call(
        kern,
        grid=(t // tt,),
        in_specs=[
            pl.BlockSpec((tt * TOP_K,), lambda i: (i,), memory_space=pltpu.SMEM),
            pl.BlockSpec((tt * ROW_CHUNKS, LANES), lambda i: (i, 0)),
        ],
        out_specs=pl.BlockSpec(memory_space=pl.ANY),
        out_shape=jax.ShapeDtypeStruct((t * TOP_K * ROW_CHUNKS, LANES), F32),
        scratch_shapes=[pltpu.SemaphoreType.DMA(())],
        compiler_params=_cparams(("arbitrary",)),
        name="dispatch",
    )(pos_flat, h2)


def _experts_kernel(tile_ref, exp_ref, first_ref, efirst_ref, valid_ref, start_ref,
                    x_ref, wu_ref, bu_ref, wd_ref, bd_ref, o_ref, wub_ref, wdb_ref, *, tm):
    w = pl.program_id(0)

    @pl.when(efirst_ref[w] == 1)
    def _():
        for c0 in range(0, 2 * D_EXPERT, 512):
            wub_ref[:, c0:c0 + 512] = wu_ref[0, :, c0:c0 + 512].astype(BF16)
        for c0 in range(0, D_MODEL, 512):
            wdb_ref[:, c0:c0 + 512] = wd_ref[0, :, c0:c0 + 512].astype(BF16)

    @pl.when(valid_ref[w] == 1)
    def _():
        e = exp_ref[w]
        x = pltpu.einshape("(rc)l->r(cl)", x_ref[...], c=ROW_CHUNKS).astype(BF16)
        gu = jnp.dot(x, wub_ref[...], preferred_element_type=F32) + bu_ref[0]
        g = jnp.minimum(gu[:, :D_EXPERT], SWIGLU_LIMIT)
        u = jnp.clip(gu[:, D_EXPERT:], -SWIGLU_LIMIT, SWIGLU_LIMIT)
        act = (u + 1.0) * (g * _sigmoid(SWIGLU_ALPHA * g))
        out = jnp.dot(act.astype(BF16), wdb_ref[...], preferred_element_type=F32) + bd_ref[0]
        rows = tile_ref[w] * tm + lax.broadcasted_iota(jnp.int32, (tm, 1), 0)
        mine = (rows >= start_ref[e]) & (rows < start_ref[e + 1])

        def as_tiles(v):
            return pltpu.einshape("r(cl)->(rc)l", v, c=ROW_CHUNKS)

        @pl.when(first_ref[w] == 1)
        def _():
            o_ref[...] = as_tiles(jnp.where(mine, out, 0.0))

        @pl.when(first_ref[w] == 0)
        def _():
            o_ref[...] = as_tiles(jnp.where(mine, out, _tiles_to_rows(o_ref, tm)))


def _experts(x_sorted, item_tile, item_exp, item_first, item_efirst, item_valid, starts, w_up, b_up, w_down, b_down, tm):
    n = x_sorted.shape[0] // ROW_CHUNKS
    n_items = item_tile.shape[0]
    kern = functools.partial(_experts_kernel, tm=tm)
    grid_spec = pltpu.PrefetchScalarGridSpec(
        num_scalar_prefetch=6,
        grid=(n_items,),
        in_specs=[
            pl.BlockSpec((tm * ROW_CHUNKS, LANES), lambda w, tl, ex, *_: (tl[w], 0)),
            pl.BlockSpec((1, D_MODEL, 2 * D_EXPERT), lambda w, tl, ex, *_: (ex[w], 0, 0)),
            pl.BlockSpec((1, 1, 2 * D_EXPERT), lambda w, tl, ex, *_: (ex[w], 0, 0)),
            pl.BlockSpec((1, D_EXPERT, D_MODEL), lambda w, tl, ex, *_: (ex[w], 0, 0)),
            pl.BlockSpec((1, 1, D_MODEL), lambda w, tl, ex, *_: (ex[w], 0, 0)),
        ],
        out_specs=pl.BlockSpec((tm * ROW_CHUNKS, LANES), lambda w, tl, ex, *_: (tl[w], 0)),
        scratch_shapes=[pltpu.VMEM((D_MODEL, 2 * D_EXPERT), BF16), pltpu.VMEM((D_EXPERT, D_MODEL), BF16)],
    )
    return pl.pallas_call(
        kern,
        grid_spec=grid_spec,
        out_shape=jax.ShapeDtypeStruct((n * ROW_CHUNKS, LANES), F32),
        compiler_params=_cparams(("arbitrary",)),
        name="experts",
    )(item_tile, item_exp, item_first, item_efirst, item_valid, starts, x_sorted, w_up, b_up, w_down, b_down)


def _combine_kernel(pos_ref, posn_ref, gate_ref, x_ref, p_ref, pnw_ref, wpg_ref, wpp_ref, fnw_ref, ys_ref, o_ref,
                    rows_ref, sem, *, tt):
    i = pl.program_id(0)
    n = pl.num_programs(0)
    slot = i % 2

    def gather(idx_ref, dst_slot):
        def issue(t, c):
            for k in range(TOP_K):
                src = pl.multiple_of(idx_ref[t * TOP_K + k] * ROW_CHUNKS, ROW_CHUNKS)
                dst = pl.multiple_of((k * tt + t) * ROW_CHUNKS, ROW_CHUNKS)
                pltpu.make_async_copy(ys_ref.at[pl.ds(src, ROW_CHUNKS), :],
                                      rows_ref.at[dst_slot, pl.ds(dst, ROW_CHUNKS), :], sem.at[dst_slot]).start()
            return c

        lax.fori_loop(0, tt, issue, 0, unroll=8)

    @pl.when(i == 0)
    def _():
        gather(pos_ref, 0)

    @pl.when(i + 1 < n)
    def _():
        gather(posn_ref, 1 - slot)

    pltpu.make_async_copy(ys_ref.at[pl.ds(0, TOP_K * tt * ROW_CHUNKS), :], rows_ref.at[slot], sem.at[slot]).wait()

    gate = gate_ref[...]
    x = x_ref[...]
    for k in range(TOP_K):
        tiles = rows_ref[slot, pl.ds(k * tt * ROW_CHUNKS, tt * ROW_CHUNKS), :]
        x = x + gate[:, k:k + 1] * pltpu.einshape("(rc)l->r(cl)", tiles, c=ROW_CHUNKS)

    hg = _rms(x, pnw_ref[...]).astype(BF16)
    pg = _sigmoid(jnp.dot(hg, wpg_ref[...], preferred_element_type=F32))
    x = x + pg * jnp.dot(p_ref[...].astype(BF16), wpp_ref[...], preferred_element_type=F32)
    o_ref[...] = _rms(x, fnw_ref[...])


def _combine(pos_flat, gate, x1, p2d, ple_nw, w_pg, w_pp, final_nw, y_sorted, tt=512):
    t = x1.shape[0]
    tt = min(tt, t)
    kern = functools.partial(_combine_kernel, tt=tt)
    vec = pl.BlockSpec((1, D_MODEL), lambda i: (0, 0))
    nt = t // tt
    return pl.pallas_call(
        kern,
        grid=(nt,),
        in_specs=[
            pl.BlockSpec((tt * TOP_K,), lambda i: (i,), memory_space=pltpu.SMEM),
            pl.BlockSpec((tt * TOP_K,), lambda i: (jnp.minimum(i + 1, nt - 1),), memory_space=pltpu.SMEM),
            pl.BlockSpec((tt, LANES), lambda i: (i, 0)),
            pl.BlockSpec((tt, D_MODEL), lambda i: (i, 0)),
            pl.BlockSpec((tt, PLE_DIM), lambda i: (i, 0)),
            vec,
            pl.BlockSpec((D_MODEL, D_MODEL), lambda i: (0, 0)),
            pl.BlockSpec((PLE_DIM, D_MODEL), lambda i: (0, 0)),
            vec,
            pl.BlockSpec(memory_space=pl.ANY),
        ],
        out_specs=pl.BlockSpec((tt, D_MODEL), lambda i: (i, 0)),
        out_shape=jax.ShapeDtypeStruct((t, D_MODEL), F32),
        scratch_shapes=[pltpu.VMEM((2, TOP_K * tt * ROW_CHUNKS, LANES), F32), pltpu.SemaphoreType.DMA((2,))],
        compiler_params=_cparams(("arbitrary",)),
        name="combine",
    )(pos_flat, pos_flat, gate, x1, p2d, ple_nw, w_pg, w_pp, final_nw, y_sorted)


def _routing_plan(counts, idx, rank, tm, n_rows):
    counts = counts.astype(jnp.int32)
    ends = jnp.cumsum(counts)
    starts = ends - counts
    pos = (starts[idx] + rank).reshape(-1)
    n_tiles = n_rows // tm
    n_items = n_tiles + N_EXPERTS - 1
    first_tile = starts // tm
    last_tile = jnp.maximum(ends - 1, 0) // tm
    items_e = jnp.where(counts > 0, last_tile - first_tile + 1, 0)
    item_end = jnp.cumsum(items_e)
    item_start = item_end - items_e
    total = item_end[-1]
    w = jnp.arange(n_items, dtype=jnp.int32)
    e_w = jnp.minimum(jnp.sum(w[:, None] >= item_end[None, :], axis=1).astype(jnp.int32), N_EXPERTS - 1)
    tile_w = first_tile[e_w] + (w - item_start[e_w])
    valid = w < total
    last = jnp.maximum(total - 1, 0)
    e_w = jnp.where(valid, e_w, e_w[last])
    tile_w = jnp.where(valid, tile_w, tile_w[last])
    prev_tile = jnp.concatenate([jnp.full((1,), -1, jnp.int32), tile_w[:-1]])
    first = (tile_w != prev_tile) & valid
    prev_e = jnp.concatenate([jnp.full((1,), -1, jnp.int32), e_w[:-1]])
    efirst = (e_w != prev_e) & valid
    starts_ext = jnp.concatenate([starts, ends[-1:]]).astype(jnp.int32)
    return (pos.astype(jnp.int32), tile_w.astype(jnp.int32), e_w, first.astype(jnp.int32), efirst.astype(jnp.int32),
            valid.astype(jnp.int32), starts_ext)


def kernel(x, p, mix_norm_w, w_in, lambda_q1, lambda_k1, lambda_q2, lambda_k2, da_head_norm_w, w_attn_branch, conv_w, conv_b, dt_bias, a_log, d_skip, ssd_norm_w, w_ssd_branch, w_out, moe_norm_w, router_w, router_b, w_up, b_up, w_down, b_down, ple_norm_w, w_ple_gate, w_ple_proj, final_norm_w):
    batch, seq, _ = x.shape
    t = batch * seq
    depth = w_in.shape[0]
    assert depth == 1, "the final RMSNorm is fused into the layer's last kernel"
    x2d = x.reshape(t, D_MODEL)

    log2e = math.log2(math.e)
    slopes = jnp.asarray([log2e * 2.0 ** (-8.0 * (h + 1) / DA_HEADS) for h in range(DA_HEADS)], F32)
    q_scale = log2e * DA_QK_DIM ** -0.5
    head_ids = jnp.arange(SSD_D_INNER, dtype=jnp.int32) // SSD_HEAD_DIM
    expand = (jnp.arange(LANES, dtype=jnp.int32)[:, None] == head_ids[None, :]).astype(BF16)

    def pad_lanes(v, fill=0.0):
        return jnp.pad(v.reshape(1, -1), ((0, 0), (0, LANES - v.shape[-1])), constant_values=fill)

    for i in range(depth):
        wi = w_in[i]
        w_main = jnp.concatenate([
            wi[:, _R_Z:_R_Z + SSD_D_INNER], wi[:, _R_Q:_R_Q + 1024] * q_scale, wi[:, _R_K:_R_K + 1024],
            wi[:, _R_GA:_R_GA + 1024], wi[:, _R_GS:_R_GS + 1024], wi[:, _R_XBC:_R_XBC + SSD_CONV_DIM],
            wi[:, _R_V:_R_V + 1024]], axis=1).astype(BF16)
        w_dt = jnp.pad(wi[:, _R_DT:_R_DT + SSD_HEADS], ((0, 0), (0, LANES - SSD_HEADS))).astype(BF16)

        proj, dt_raw, v_t = _in_proj(x2d, mix_norm_w[i].reshape(1, -1), w_main, w_dt)

        lambda_init = 0.8 - 0.6 * math.exp(-0.3 * i)
        lam = (jnp.exp(jnp.sum(lambda_q1[i] * lambda_k1[i])) - jnp.exp(jnp.sum(lambda_q2[i] * lambda_k2[i]))
               + lambda_init).reshape(1).astype(F32)
        y_attn = _diff_attn(proj, v_t, slopes, lam, da_head_norm_w[i].reshape(1, -1), batch, seq, lambda_init)

        y_ssd = _ssd(proj, dt_raw, conv_w[i], conv_b[i].reshape(1, -1), pad_lanes(dt_bias[i]), pad_lanes(a_log[i]),
                     jnp.repeat(d_skip[i], SSD_HEAD_DIM).reshape(1, -1), ssd_norm_w[i].reshape(1, -1), expand,
                     batch, seq)

        x1 = _merge(y_attn, y_ssd, proj, x2d, w_attn_branch[i].astype(BF16), w_ssd_branch[i].astype(BF16),
                    w_out[i].astype(BF16))

        rw = jnp.stack(_split2(jnp.pad(router_w[i], ((0, 0), (0, LANES - N_EXPERTS)))))
        h2, idx, gate, rank, counts = _router(x1, moe_norm_w[i].reshape(1, -1), rw, pad_lanes(router_b[i]))

        tm = 512
        n_rows = t * TOP_K
        pos, item_tile, item_exp, item_first, item_efirst, item_valid, starts = _routing_plan(
            counts[0, :N_EXPERTS], idx[:, :TOP_K], rank[:, :TOP_K], tm, n_rows)

        x_sorted = _dispatch(h2, pos)
        y_sorted = _experts(x_sorted, item_tile, item_exp, item_first, item_efirst, item_valid, starts,
                            w_up[i], b_up[i].reshape(N_EXPERTS, 1, -1),
                            w_down[i], b_down[i].reshape(N_EXPERTS, 1, -1), tm)

        x2d = _combine(pos, gate, x1, p[i].reshape(t, PLE_DIM), ple_norm_w[i].reshape(1, -1),
                       w_ple_gate[i].astype(BF16), w_ple_proj[i].astype(BF16), final_norm_w.reshape(1, -1), y_sorted)
    return x2d.reshape(batch, seq, D_MODEL)
```

```python
import functools
import math

import jax
import jax.numpy as jnp
from jax import lax
from jax.experimental import pallas as pl
from jax.experimental.pallas import tpu as pltpu

F32 = jnp.float32
BF16 = jnp.bfloat16

D_MODEL = 1024
CHUNK = 64
PLE_DIM = 256
RMS_EPS = 1e-6

DA_HEADS = 8
DA_QK_DIM = 64
DA_V_DIM = 128

SSD_D_INNER = 2048
SSD_HEAD_DIM = 64
SSD_HEADS = 32
SSD_GROUPS = 4
SSD_STATE = 128
SSD_CONV = 4
SSD_CONV_DIM = 3072

N_EXPERTS = 32
TOP_K = 4
D_EXPERT = 1024
SWIGLU_LIMIT = 7.0
SWIGLU_ALPHA = 1.702

LANES = 128

COL_Z = 0
COL_Q = 2048
COL_K = 3072
COL_GA = 4096
COL_GS = 5120
COL_XBC = 6144
PROJ_W = 9216

ATTN_TILE = 256
ATTN_Q_TILE = 512
SUM_ROWS = 16

_R_Q, _R_K, _R_V, _R_Z, _R_XBC, _R_DT, _R_GA, _R_GS = 0, 1024, 2048, 3072, 5120, 8192, 8224, 9248

VMEM_LIMIT = 56 * 1024 * 1024


def _cparams(sem):
    return pltpu.CompilerParams(dimension_semantics=sem, vmem_limit_bytes=VMEM_LIMIT)


def _rms(x, w):
    return x * lax.rsqrt(jnp.mean(x * x, axis=-1, keepdims=True) + RMS_EPS) * w


ROW_CHUNKS = D_MODEL // LANES


def _rows_to_tiles(ref, val, n):
    for c in range(ROW_CHUNKS):
        ref[pl.ds(c, n, stride=ROW_CHUNKS), :] = val[:, c * LANES:(c + 1) * LANES]


def _tiles_to_rows(ref, n, base=0):
    return jnp.concatenate([ref[pl.ds(base + c, n, stride=ROW_CHUNKS), :] for c in range(ROW_CHUNKS)], axis=1)


def _sigmoid(x):
    return 1.0 / (1.0 + jnp.exp2(x * (-math.log2(math.e))))


def _inproj_kernel(x_ref, nw_ref, w_ref, wdt_ref, o_ref, dt_ref, vt_ref, h_ref, *, tm, n_proj):
    j = pl.program_id(1)

    @pl.when(j == 0)
    def _():
        hb = _rms(x_ref[...], nw_ref[...]).astype(BF16)
        h_ref[...] = hb
        dt_ref[...] = jnp.dot(hb, wdt_ref[...], preferred_element_type=F32)

    acc = jnp.dot(h_ref[...], w_ref[...], preferred_element_type=F32)

    @pl.when(j < n_proj)
    def _():
        o_ref[...] = acc.astype(BF16)

    @pl.when(j == n_proj)
    def _():
        vt = acc.T
        for c in range(tm // ATTN_TILE):
            vt_ref[c] = vt[:, c * ATTN_TILE:(c + 1) * ATTN_TILE].astype(BF16)


def _in_proj(x2d, norm_w, w_main, w_dt, tm=2048):
    t = x2d.shape[0]
    tm = min(tm, t)
    tn = DA_HEADS * DA_V_DIM
    n_proj = PROJ_W // tn
    return pl.pallas_call(
        functools.partial(_inproj_kernel, tm=tm, n_proj=n_proj),
        grid=(t // tm, n_proj + 1),
        in_specs=[
            pl.BlockSpec((tm, D_MODEL), lambda i, j: (i, 0)),
            pl.BlockSpec((1, D_MODEL), lambda i, j: (0, 0)),
            pl.BlockSpec((D_MODEL, tn), lambda i, j: (0, j)),
            pl.BlockSpec((D_MODEL, LANES), lambda i, j: (0, 0)),
        ],
        out_specs=[
            pl.BlockSpec((tm, tn), lambda i, j: (i, jnp.minimum(j, n_proj - 1))),
            pl.BlockSpec((tm, LANES), lambda i, j: (i, 0)),
            pl.BlockSpec((tm // ATTN_TILE, tn, ATTN_TILE), lambda i, j: (i, 0, 0)),
        ],
        out_shape=[
            jax.ShapeDtypeStruct((t, PROJ_W), BF16),
            jax.ShapeDtypeStruct((t, LANES), F32),
            jax.ShapeDtypeStruct((t // ATTN_TILE, tn, ATTN_TILE), BF16),
        ],
        scratch_shapes=[pltpu.VMEM((tm, D_MODEL), BF16)],
        compiler_params=_cparams(("parallel", "arbitrary")),
        name="in_proj",
    )(x2d, norm_w, w_main, w_dt)


def _attn_kernel(slopes_ref, lam_ref, q_ref, k_ref, vt_ref, nw_ref, o_ref,
                 bias_ref, ka_ref, kb_ref, qc_ref, s_ref, p_ref, al_ref, m_ref, a_ref, *, tq, tk, out_scale):
    h = pl.program_id(1)
    i = pl.program_id(2)
    slope = slopes_ref[h]
    lam = lam_ref[0]
    r = tq // tk
    n_past = r * i
    n_tiles = n_past + r
    seq = k_ref.shape[0]
    aux = 16

    @pl.when(i == 0)
    def _():
        kk = lax.broadcasted_iota(jnp.int32, (tk, tq), 0)
        qq = lax.broadcasted_iota(jnp.int32, (tk, tq), 1)
        bias_ref[0] = jnp.zeros((tk, tq), F32)
        for d in range(r):
            kd = kk + d * tk
            true = jnp.where((kd // CHUNK) <= (qq // CHUNK), -slope * jnp.abs(qq - kd).astype(F32), -jnp.inf)
            bias_ref[1 + d] = true + slope * (qq - kk).astype(F32)
        rows = 512
        lane = lax.broadcasted_iota(jnp.int32, (rows, LANES), 1)
        krel = (lax.broadcasted_iota(jnp.int32, (rows, LANES), 0) % tk).astype(F32)
        lo = lane - DA_QK_DIM
        aux_a = jnp.where(lo < 3, krel, jnp.where(lo < 6, 1.0, 0.0))
        aux_b = jnp.where(lane < 3, krel, jnp.where(lane < 6, 1.0, 0.0))
        for r0 in range(0, seq, rows):
            k = k_ref[r0:r0 + rows, :].astype(F32)
            ka_ref[r0:r0 + rows, :] = jnp.where(lane < DA_QK_DIM, k, aux_a).astype(BF16)
            kb_ref[r0:r0 + rows, :] = jnp.where(lane >= DA_QK_DIM, k, aux_b).astype(BF16)

    qt = q_ref[...].astype(F32).T.astype(BF16)
    arow = lax.broadcasted_iota(jnp.int32, (aux, tq), 0)
    qrel = lax.broadcasted_iota(jnp.int32, (aux, tq), 1).astype(F32)
    pieces = _split3(jnp.full((aux, tq), slope, F32)) + _split3(-slope * qrel)
    extra = jnp.zeros((aux, tq), F32)
    for n, piece in enumerate(pieces):
        extra = jnp.where(arow == n, piece.astype(F32), extra)
    extra = extra.astype(BF16)
    pad = jnp.zeros((DA_QK_DIM - aux, tq), BF16)
    qc_ref[:, 0:tq] = jnp.concatenate([qt[0:DA_QK_DIM], extra, pad], axis=0)
    qc_ref[:, tq:2 * tq] = jnp.concatenate([extra, pad, qt[DA_QK_DIM:]], axis=0)

    m_ref[...] = jnp.full(m_ref.shape, -jnp.inf, F32)
    a_ref[...] = jnp.zeros(a_ref.shape, F32)

    def scores(j):
        rows = pl.ds(pl.multiple_of(j * tk, tk), tk)
        return (jnp.dot(ka_ref[rows, :], qc_ref[:, 0:tq], preferred_element_type=F32),
                jnp.dot(kb_ref[rows, :], qc_ref[:, tq:2 * tq], preferred_element_type=F32))

    def store_scores(blocks):
        s_ref[:, 0:tq] = blocks[0]
        s_ref[:, tq:2 * tq] = blocks[1]

    def softmax_step(sel, cj):
        for g in range(2 * tq // LANES):
            cols = slice(g * LANES, (g + 1) * LANES)
            bcols = slice((g * LANES) % tq, (g * LANES) % tq + LANES)
            s = s_ref[:, cols]
            if sel is not None:
                s = s + bias_ref[sel, :, bcols]
            m_prev = m_ref[:, cols]
            m_cur = jnp.maximum(m_prev, jnp.max(s, axis=0, keepdims=True) + cj)
            alpha = jnp.exp2(m_prev - m_cur)
            p = jnp.exp2(s + (cj - m_cur))
            m_ref[:, cols] = m_cur
            al_ref[:, cols] = alpha
            p_ref[:, cols] = p.astype(BF16)

    def pv_step(j):
        vt = jnp.concatenate([vt_ref[j], jnp.ones((SUM_ROWS, tk), BF16)], axis=0)
        for g in range(2 * tq // 256):
            cols = slice(g * 256, (g + 1) * 256)
            a_ref[:, cols] = al_ref[:, cols] * a_ref[:, cols] + jnp.dot(vt, p_ref[:, cols], preferred_element_type=F32)

    def past_cj(j):
        return -slope * (i * tq - j * tk).astype(F32)

    store_scores(scores(0))
    s_next = scores(1)
    softmax_step(jnp.where(i > 0, 0, 1), jnp.where(i > 0, past_cj(0), 0.0))
    store_scores(s_next)

    def body(j, carry):
        s_next = scores(j + 1)
        pv_step(j - 1)
        softmax_step(None, past_cj(j))
        store_scores(s_next)
        return carry

    lax.fori_loop(1, n_past, body, 0)

    for d in range(r):
        t = n_past + d

        def overlap_tile(t=t, d=d):
            if d + 1 < r:
                s_next = scores(t + 1)
            pv_step(t - 1)
            softmax_step(1 + d, 0.0)
            if d + 1 < r:
                store_scores(s_next)

        if d == 0:
            pl.when(i > 0)(overlap_tile)
        else:
            overlap_tile()
    pv_step(n_tiles - 1)

    on = a_ref[0:DA_V_DIM, :] / a_ref[DA_V_DIM:DA_V_DIM + 1, :]
    ot = on[:, :tq] - lam * on[:, tq:]
    ot = ot * lax.rsqrt(jnp.mean(ot * ot, axis=0, keepdims=True) + RMS_EPS)
    o_ref[...] = (ot.T * nw_ref[...] * out_scale).astype(BF16)


def _diff_attn(proj, v_t, slopes, lam, head_norm_w, batch, seq, lambda_init):
    tk = ATTN_TILE
    tq = min(ATTN_Q_TILE, seq)
    nq = seq // tq
    nk = seq // tk
    t = batch * seq
    qb, kb = COL_Q // LANES, COL_K // LANES
    assert tq % tk == 0 and tq >= 2 * tk, "the peeled first/last key tiles need at least two tiles per step"
    kern = functools.partial(_attn_kernel, tq=tq, tk=tk, out_scale=1.0 - lambda_init)
    smem = pl.BlockSpec(memory_space=pltpu.SMEM)
    return pl.pallas_call(
        kern,
        grid=(batch, DA_HEADS, nq),
        in_specs=[
            smem, smem,
            pl.BlockSpec((tq, LANES), lambda b, h, i: (b * nq + i, qb + h)),
            pl.BlockSpec((seq, LANES), lambda b, h, i: (b, kb + h)),
            pl.BlockSpec((nk, DA_V_DIM, tk), lambda b, h, i: (b, h, 0)),
            pl.BlockSpec((1, LANES), lambda b, h, i: (0, 0)),
        ],
        out_specs=pl.BlockSpec((tq, LANES), lambda b, h, i: (b * nq + i, h)),
        out_shape=jax.ShapeDtypeStruct((t, DA_HEADS * DA_V_DIM), BF16),
        scratch_shapes=[
            pltpu.VMEM((1 + tq // tk, tk, tq), F32),
            pltpu.VMEM((seq, LANES), BF16),
            pltpu.VMEM((seq, LANES), BF16),
            pltpu.VMEM((LANES, 2 * tq), BF16),
            pltpu.VMEM((tk, 2 * tq), F32),
            pltpu.VMEM((tk, 2 * tq), BF16),
            pltpu.VMEM((1, 2 * tq), F32),
            pltpu.VMEM((1, 2 * tq), F32),
            pltpu.VMEM((DA_V_DIM + SUM_ROWS, 2 * tq), F32),
        ],
        compiler_params=_cparams(("parallel", "parallel", "arbitrary")),
        name="diff_attn",
    )(slopes, lam, proj, proj, v_t, head_norm_w)


def _split2(x):
    hi = x.astype(BF16)
    lo = (x - hi.astype(F32)).astype(BF16)
    return hi, lo


def _split3(x):
    hi = x.astype(BF16)
    r = x - hi.astype(F32)
    mid = r.astype(BF16)
    lo = (r - mid.astype(F32)).astype(BF16)
    return hi, mid, lo


def _ssd_kernel(xbc_ref, z_ref, dt_ref, cw_ref, cb_ref, dtb_ref, alog_ref, dsk_ref, nw_ref, e_ref,
                o_ref, ubuf, shift_ref, xs_scr, bc_scr, acum_scr, acb, dfs_scr, xdt_scr, xw_scr, y_scr, state, *, tr):
    i = pl.program_id(1)
    nch = tr // CHUNK
    gw = SSD_D_INNER // SSD_GROUPS
    pad = 8

    head = 16

    @pl.when(i == 0)
    def _():
        ubuf[0:pad, :] = jnp.zeros((pad, SSD_CONV_DIM), F32)
        state[...] = jnp.zeros_like(state)
        r_s = lax.broadcasted_iota(jnp.int32, (tr, tr), 0)
        c_s = lax.broadcasted_iota(jnp.int32, (tr, tr), 1)
        for d in range(1, SSD_CONV):
            shift_ref[(d - 1) * tr:d * tr, :] = jnp.where(c_s == r_s - d, 1.0, 0.0).astype(BF16)

    ubuf[pad:pad + head, :] = xbc_ref[0:head, :].astype(F32)
    cs = 512
    for c0 in range(0, SSD_CONV_DIM, cs):
        ub = xbc_ref[:, c0:c0 + cs]
        sh = jnp.dot(shift_ref[...], ub, preferred_element_type=F32)
        acc = cb_ref[:, c0:c0 + cs] + cw_ref[SSD_CONV - 1:SSD_CONV, c0:c0 + cs] * ub.astype(F32)
        for d in range(1, SSD_CONV):
            acc = acc + cw_ref[SSD_CONV - 1 - d:SSD_CONV - d, c0:c0 + cs] * sh[(d - 1) * tr:d * tr, :]
        u = acc * _sigmoid(acc)
        acc_h = jnp.broadcast_to(cb_ref[:, c0:c0 + cs], (head, cs))
        for k in range(SSD_CONV):
            off = pad - (SSD_CONV - 1) + k
            acc_h = acc_h + cw_ref[k:k + 1, c0:c0 + cs] * ubuf[off:off + head, c0:c0 + cs]
        u_h = acc_h * _sigmoid(acc_h)
        if c0 < SSD_D_INNER:
            xs_scr[:, c0:c0 + cs] = u
            xs_scr[0:head, c0:c0 + cs] = u_h
        else:
            bc_scr[:, c0 - SSD_D_INNER:c0 - SSD_D_INNER + cs] = u.astype(BF16)
            bc_scr[0:head, c0 - SSD_D_INNER:c0 - SSD_D_INNER + cs] = u_h.astype(BF16)
    ubuf[0:pad, :] = xbc_ref[tr - 2 * pad:tr, :].astype(F32)[pad:, :]

    xdt_in = dt_ref[...] + dtb_ref[...]
    dtv = jnp.maximum(xdt_in, 0.0) + jnp.log1p(jnp.exp(-jnp.abs(xdt_in)))
    da = dtv * (-jnp.exp(alog_ref[...]))
    r_i = lax.broadcasted_iota(jnp.int32, (tr, tr), 0)
    c_i = lax.broadcasted_iota(jnp.int32, (tr, tr), 1)
    tril = jnp.where((c_i <= r_i) & ((c_i // CHUNK) == (r_i // CHUNK)), 1.0, 0.0).astype(BF16)
    acum = sum(jnp.dot(tril, piece, preferred_element_type=F32) for piece in _split3(da))
    acum_scr[...] = acum

    a_hi, a_lo = _split2(acum)
    d_hi, d_lo = _split2(dtv)
    for c0 in range(0, SSD_D_INNER, cs):
        e = e_ref[:, c0:c0 + cs]
        ab = jnp.dot(a_hi, e, preferred_element_type=F32) + jnp.dot(a_lo, e, preferred_element_type=F32)
        db = jnp.dot(d_hi, e, preferred_element_type=F32) + jnp.dot(d_lo, e, preferred_element_type=F32)
        acb[:, c0:c0 + cs] = ab
        dfs_scr[:, c0:c0 + cs] = jnp.exp(ab)
        xdt = xs_scr[:, c0:c0 + cs] * db
        xdt_scr[:, c0:c0 + cs] = xdt.astype(BF16)
        for c in range(nch):
            r0 = c * CHUNK
            last = ab[r0 + CHUNK - 1:r0 + CHUNK, :]
            xw_scr[r0:r0 + CHUNK, c0:c0 + cs] = (xdt[r0:r0 + CHUNK, :] * jnp.exp(last - ab[r0:r0 + CHUNK, :])).astype(BF16)

    lane2 = lax.broadcasted_iota(jnp.int32, (CHUNK, LANES), 1)
    row2 = lax.broadcasted_iota(jnp.int32, (CHUNK, LANES), 0)
    causal2 = row2 >= (lane2 % SSD_HEAD_DIM)
    lo_half = lane2 < SSD_HEAD_DIM
    zpad = jnp.zeros((LANES - CHUNK, LANES), F32)

    def chunk_body(c, carry):
        r0 = pl.multiple_of(c * CHUNK, CHUNK)
        rows = pl.ds(r0, CHUNK)
        a_c = acum_scr[rows, :]
        a_t = jnp.concatenate([a_c, zpad], axis=0).T
        a_t_r = pltpu.roll(a_t, SSD_HEAD_DIM, 1)
        last_row = acb[pl.ds(r0 + CHUNK - 1, 1), :]
        for g in range(SSD_GROUPS):
            bg = bc_scr[rows, g * SSD_STATE:(g + 1) * SSD_STATE]
            cg = bc_scr[rows, SSD_GROUPS * SSD_STATE + g * SSD_STATE:SSD_GROUPS * SSD_STATE + (g + 1) * SSD_STATE]
            cbt = lax.dot_general(cg, jnp.concatenate([bg, bg], axis=0), (((1,), (1,)), ((), ())),
                                  preferred_element_type=F32)
            g0 = g * gw
            st = state[g]
            yoff = jnp.dot(cg, st.astype(BF16), preferred_element_type=F32) * dfs_scr[rows, g0:g0 + gw]
            for jj in range(gw // LANES):
                pidx = g * (gw // LANES) + jj
                l0 = pidx * LANES
                arow = a_t[2 * pidx:2 * pidx + 1, :] + a_t_r[2 * pidx + 1:2 * pidx + 2, :]
                seg = acb[rows, l0:l0 + LANES] - arow
                decay = jnp.exp(jnp.where(causal2, seg, -jnp.inf))
                mp = (cbt * decay).astype(BF16)
                xd = xdt_scr[rows, l0:l0 + LANES]
                zb = jnp.zeros_like(xd)
                bd = jnp.concatenate([jnp.where(lo_half, xd, zb), jnp.where(lo_half, zb, xd)], axis=0)
                yd = jnp.dot(mp, bd, preferred_element_type=F32)
                y_scr[rows, l0:l0 + LANES] = yd + yoff[:, jj * LANES:(jj + 1) * LANES]
            cd = jnp.exp(last_row[:, g0:g0 + gw])
            upd = lax.dot_general(bg, xw_scr[rows, g0:g0 + gw], (((0,), (0,)), ((), ())),
                                  preferred_element_type=F32)
            state[g] = st * cd + upd
        return carry

    lax.fori_loop(0, nch, chunk_body, 0, unroll=True)

    for g in range(SSD_GROUPS):
        g0 = g * gw
        y = y_scr[:, g0:g0 + gw] + dsk_ref[:, g0:g0 + gw] * xs_scr[:, g0:g0 + gw]
        zz = z_ref[:, g0:g0 + gw].astype(F32)
        y = y * (zz * _sigmoid(zz))
        o_ref[:, g0:g0 + gw] = _rms(y, nw_ref[:, g0:g0 + gw]).astype(BF16)


def _ssd(proj, dt_raw, conv_w, conv_b, dt_bias, a_log, d_skip_b, norm_w, expand, batch, seq, tr=256):
    t = batch * seq
    nr = seq // tr
    kern = functools.partial(_ssd_kernel, tr=tr)
    const = lambda shape: pl.BlockSpec(shape, lambda b, i: (0, 0))
    return pl.pallas_call(
        kern,
        grid=(batch, nr),
        in_specs=[
            pl.BlockSpec((tr, SSD_CONV_DIM), lambda b, i: (b * nr + i, COL_XBC // SSD_CONV_DIM)),
            pl.BlockSpec((tr, SSD_D_INNER), lambda b, i: (b * nr + i, COL_Z // SSD_D_INNER)),
            pl.BlockSpec((tr, LANES), lambda b, i: (b * nr + i, 0)),
            const((SSD_CONV, SSD_CONV_DIM)),
            const((1, SSD_CONV_DIM)),
            const((1, LANES)),
            const((1, LANES)),
            const((1, SSD_D_INNER)),
            const((1, SSD_D_INNER)),
            const((LANES, SSD_D_INNER)),
        ],
        out_specs=pl.BlockSpec((tr, SSD_D_INNER), lambda b, i: (b * nr + i, 0)),
        out_shape=jax.ShapeDtypeStruct((t, SSD_D_INNER), BF16),
        scratch_shapes=[
            pltpu.VMEM((8 + 16, SSD_CONV_DIM), F32),
            pltpu.VMEM(((SSD_CONV - 1) * tr, tr), BF16),
            pltpu.VMEM((tr, SSD_D_INNER), F32),
            pltpu.VMEM((tr, 2 * SSD_GROUPS * SSD_STATE), BF16),
            pltpu.VMEM((tr, LANES), F32),
            pltpu.VMEM((tr, SSD_D_INNER), F32),
            pltpu.VMEM((tr, SSD_D_INNER), F32),
            pltpu.VMEM((tr, SSD_D_INNER), BF16),
            pltpu.VMEM((tr, SSD_D_INNER), BF16),
            pltpu.VMEM((tr, SSD_D_INNER), F32),
            pltpu.VMEM((SSD_GROUPS, SSD_STATE, SSD_D_INNER // SSD_GROUPS), F32),
        ],
        compiler_params=_cparams(("parallel", "arbitrary")),
        name="ssd",
    )(proj, proj, dt_raw, conv_w, conv_b, dt_bias, a_log, d_skip_b, norm_w, expand)


def _merge_kernel(ya_ref, ys_ref, ga_ref, gs_ref, x_ref, wa_ref, wb_ref, wo_ref, o_ref):
    a = jnp.dot(ya_ref[...], wa_ref[...], preferred_element_type=F32)
    s = jnp.dot(ys_ref[...], wb_ref[...], preferred_element_type=F32)
    merged = _sigmoid(ga_ref[...].astype(F32)) * a + _sigmoid(gs_ref[...].astype(F32)) * s
    o_ref[...] = x_ref[...] + jnp.dot(merged.astype(BF16), wo_ref[...], preferred_element_type=F32)


def _merge(y_attn, y_ssd, proj, x2d, w_a, w_b, w_o, tm=512):
    t = x2d.shape[0]
    tm = min(tm, t)
    full = lambda r, c: pl.BlockSpec((r, c), lambda i: (0, 0))
    return pl.pallas_call(
        _merge_kernel,
        grid=(t // tm,),
        in_specs=[
            pl.BlockSpec((tm, D_MODEL), lambda i: (i, 0)),
            pl.BlockSpec((tm, SSD_D_INNER), lambda i: (i, 0)),
            pl.BlockSpec((tm, D_MODEL), lambda i: (i, COL_GA // D_MODEL)),
            pl.BlockSpec((tm, D_MODEL), lambda i: (i, COL_GS // D_MODEL)),
            pl.BlockSpec((tm, D_MODEL), lambda i: (i, 0)),
            full(D_MODEL, D_MODEL), full(SSD_D_INNER, D_MODEL), full(D_MODEL, D_MODEL),
        ],
        out_specs=pl.BlockSpec((tm, D_MODEL), lambda i: (i, 0)),
        out_shape=jax.ShapeDtypeStruct((t, D_MODEL), F32),
        compiler_params=_cparams(("parallel",)),
        name="merge",
    )(y_attn, y_ssd, proj, proj, x2d, w_a, w_b, w_o)


def _router_kernel(x_ref, nw_ref, rw_ref, rb_ref, h_ref, idx_ref, gate_ref, rank_ref, cnt_ref, carry_ref, *, tt):
    i = pl.program_id(0)

    @pl.when(i == 0)
    def _():
        carry_ref[...] = jnp.zeros_like(carry_ref)

    h = _rms(x_ref[...], nw_ref[...])
    h_ref[...] = pltpu.einshape("r(cl)->(rc)l", h, c=ROW_CHUNKS)
    h_hi, h_lo = _split2(h)
    logits = (jnp.dot(h_hi, rw_ref[0], preferred_element_type=F32) + jnp.dot(h_hi, rw_ref[1], preferred_element_type=F32)
              + jnp.dot(h_lo, rw_ref[0], preferred_element_type=F32)) + rb_ref[...]
    lane = lax.broadcasted_iota(jnp.int32, (tt, LANES), 1)
    work = jnp.where(lane < N_EXPERTS, logits, -jnp.inf)

    vals, idxs, hots = [], [], []
    for _ in range(TOP_K):
        m = jnp.max(work, axis=-1, keepdims=True)
        idx = jnp.min(jnp.where(work == m, lane, LANES), axis=-1, keepdims=True)
        hot = lane == idx
        vals.append(m)
        idxs.append(idx)
        hots.append(hot)
        work = jnp.where(hot, -jnp.inf, work)

    exps = [jnp.exp(v - vals[0]) for v in vals]
    denom = exps[0] + exps[1] + exps[2] + exps[3]

    hot_sum = sum(jnp.where(hot, 1.0, 0.0) for hot in hots)
    r_i = lax.broadcasted_iota(jnp.int32, (tt, tt), 0)
    c_i = lax.broadcasted_iota(jnp.int32, (tt, tt), 1)
    strict = jnp.where(c_i < r_i, 1.0, 0.0).astype(BF16)
    prefix = jnp.dot(strict, hot_sum.astype(BF16), preferred_element_type=F32) + carry_ref[...]

    idx_out = jnp.zeros((tt, LANES), jnp.int32)
    rank_out = jnp.zeros((tt, LANES), jnp.int32)
    gate_out = jnp.zeros((tt, LANES), F32)
    for k in range(TOP_K):
        rank_k = jnp.sum(jnp.where(hots[k], prefix, 0.0), axis=-1, keepdims=True).astype(jnp.int32)
        sel = lane == k
        idx_out = jnp.where(sel, idxs[k], idx_out)
        rank_out = jnp.where(sel, rank_k, rank_out)
        gate_out = jnp.where(sel, exps[k] / denom, gate_out)
    idx_ref[...] = idx_out
    rank_ref[...] = rank_out
    gate_ref[...] = gate_out

    carry_ref[...] = carry_ref[...] + jnp.sum(hot_sum, axis=0, keepdims=True)
    cnt_ref[...] = carry_ref[...]


def _router(x1, norm_w, router_w_pad, router_b_pad, tt=512):
    t = x1.shape[0]
    tt = min(tt, t)
    kern = functools.partial(_router_kernel, tt=tt)
    row = pl.BlockSpec((tt, LANES), lambda i: (i, 0))
    return pl.pallas_call(
        kern,
        grid=(t // tt,),
        in_specs=[
            pl.BlockSpec((tt, D_MODEL), lambda i: (i, 0)),
            pl.BlockSpec((1, D_MODEL), lambda i: (0, 0)),
            pl.BlockSpec((2, D_MODEL, LANES), lambda i: (0, 0, 0)),
            pl.BlockSpec((1, LANES), lambda i: (0, 0)),
        ],
        out_specs=[pl.BlockSpec((tt * ROW_CHUNKS, LANES), lambda i: (i, 0)), row, row, row,
                   pl.BlockSpec((1, LANES), lambda i: (0, 0))],
        out_shape=[
            jax.ShapeDtypeStruct((t * ROW_CHUNKS, LANES), F32),
            jax.ShapeDtypeStruct((t, LANES), jnp.int32),
            jax.ShapeDtypeStruct((t, LANES), F32),
            jax.ShapeDtypeStruct((t, LANES), jnp.int32),
            jax.ShapeDtypeStruct((1, LANES), F32),
        ],
        scratch_shapes=[pltpu.VMEM((1, LANES), F32)],
        compiler_params=_cparams(("arbitrary",)),
        name="router",
    )(x1, norm_w, router_w_pad, router_b_pad)


def _dispatch_kernel(pos_ref, h_ref, o_ref, sem, *, tt):
    def copy(t, k):
        src = pl.multiple_of(t * ROW_CHUNKS, ROW_CHUNKS)
        dst = pl.multiple_of(pos_ref[t * TOP_K + k] * ROW_CHUNKS, ROW_CHUNKS)
        return pltpu.make_async_copy(h_ref.at[pl.ds(src, ROW_CHUNKS), :], o_ref.at[pl.ds(dst, ROW_CHUNKS), :], sem)

    def issue(t, c):
        for k in range(TOP_K):
            copy(t, k).start()
        return c

    lax.fori_loop(0, tt, issue, 0, unroll=8)

    for _ in range(TOP_K):
        pltpu.make_async_copy(h_ref, o_ref.at[pl.ds(0, tt * ROW_CHUNKS), :], sem).wait()


def _dispatch(h2, pos_flat, tt=512):
    t = h2.shape[0] // ROW_CHUNKS
    tt = min(tt, t)
    kern = functools.partial(_dispatch_kernel, tt=tt)
    return pl.pallas_call(
        kern,
        grid=(t // tt,),
        in_specs=[
            pl.BlockSpec((tt * TOP_K,), lambda i: (i,), memory_space=pltpu.SMEM),
            pl.BlockSpec((tt * ROW_CHUNKS, LANES), lambda i: (i, 0)),
        ],
        out_specs=pl.BlockSpec(memory_space=pl.ANY),
        out_shape=jax.ShapeDtypeStruct((t * TOP_K * ROW_CHUNKS, LANES), F32),
        scratch_shapes=[pltpu.SemaphoreType.DMA(())],
        compiler_params=_cparams(("arbitrary",)),
        name="dispatch",
    )(pos_flat, h2)


def _experts_kernel(tile_ref, exp_ref, first_ref, efirst_ref, valid_ref, start_ref,
                    x_ref, wu_ref, bu_ref, wd_ref, bd_ref, o_ref, wub_ref, wdb_ref, *, tm):
    w = pl.program_id(0)

    @pl.when(efirst_ref[w] == 1)
    def _():
        for c0 in range(0, 2 * D_EXPERT, 512):
            wub_ref[:, c0:c0 + 512] = wu_ref[0, :, c0:c0 + 512].astype(BF16)
        for c0 in range(0, D_MODEL, 512):
            wdb_ref[:, c0:c0 + 512] = wd_ref[0, :, c0:c0 + 512].astype(BF16)

    @pl.when(valid_ref[w] == 1)
    def _():
        e = exp_ref[w]
        x = pltpu.einshape("(rc)l->r(cl)", x_ref[...], c=ROW_CHUNKS).astype(BF16)
        gu = jnp.dot(x, wub_ref[...], preferred_element_type=F32) + bu_ref[0]
        g = jnp.minimum(gu[:, :D_EXPERT], SWIGLU_LIMIT)
        u = jnp.clip(gu[:, D_EXPERT:], -SWIGLU_LIMIT, SWIGLU_LIMIT)
        act = (u + 1.0) * (g * _sigmoid(SWIGLU_ALPHA * g))
        out = jnp.dot(act.astype(BF16), wdb_ref[...], preferred_element_type=F32) + bd_ref[0]
        rows = tile_ref[w] * tm + lax.broadcasted_iota(jnp.int32, (tm, 1), 0)
        mine = (rows >= start_ref[e]) & (rows < start_ref[e + 1])

        def as_tiles(v):
            return pltpu.einshape("r(cl)->(rc)l", v, c=ROW_CHUNKS)

        @pl.when(first_ref[w] == 1)
        def _():
            o_ref[...] = as_tiles(jnp.where(mine, out, 0.0))

        @pl.when(first_ref[w] == 0)
        def _():
            o_ref[...] = as_tiles(jnp.where(mine, out, _tiles_to_rows(o_ref, tm)))


def _experts(x_sorted, item_tile, item_exp, item_first, item_efirst, item_valid, starts, w_up, b_up, w_down, b_down, tm):
    n = x_sorted.shape[0] // ROW_CHUNKS
    n_items = item_tile.shape[0]
    kern = functools.partial(_experts_kernel, tm=tm)
    grid_spec = pltpu.PrefetchScalarGridSpec(
        num_scalar_prefetch=6,
        grid=(n_items,),
        in_specs=[
            pl.BlockSpec((tm * ROW_CHUNKS, LANES), lambda w, tl, ex, *_: (tl[w], 0)),
            pl.BlockSpec((1, D_MODEL, 2 * D_EXPERT), lambda w, tl, ex, *_: (ex[w], 0, 0)),
            pl.BlockSpec((1, 1, 2 * D_EXPERT), lambda w, tl, ex, *_: (ex[w], 0, 0)),
            pl.BlockSpec((1, D_EXPERT, D_MODEL), lambda w, tl, ex, *_: (ex[w], 0, 0)),
            pl.BlockSpec((1, 1, D_MODEL), lambda w, tl, ex, *_: (ex[w], 0, 0)),
        ],
        out_specs=pl.BlockSpec((tm * ROW_CHUNKS, LANES), lambda w, tl, ex, *_: (tl[w], 0)),
        scratch_shapes=[pltpu.VMEM((D_MODEL, 2 * D_EXPERT), BF16), pltpu.VMEM((D_EXPERT, D_MODEL), BF16)],
    )
    return pl.pallas_call(
        kern,
        grid_spec=grid_spec,
        out_shape=jax.ShapeDtypeStruct((n * ROW_CHUNKS, LANES), F32),
        compiler_params=_cparams(("arbitrary",)),
        name="experts",
    )(item_tile, item_exp, item_first, item_efirst, item_valid, starts, x_sorted, w_up, b_up, w_down, b_down)


def _combine_kernel(pos_ref, posn_ref, gate_ref, x_ref, p_ref, pnw_ref, wpg_ref, wpp_ref, fnw_ref, ys_ref, o_ref,
                    rows_ref, sem, *, tt):
    i = pl.program_id(0)
    n = pl.num_programs(0)
    slot = i % 2

    def gather(idx_ref, dst_slot):
        def issue(t, c):
            for k in range(TOP_K):
                src = pl.multiple_of(idx_ref[t * TOP_K + k] * ROW_CHUNKS, ROW_CHUNKS)
                dst = pl.multiple_of((k * tt + t) * ROW_CHUNKS, ROW_CHUNKS)
                pltpu.make_async_copy(ys_ref.at[pl.ds(src, ROW_CHUNKS), :],
                                      rows_ref.at[dst_slot, pl.ds(dst, ROW_CHUNKS), :], sem.at[dst_slot]).start()
            return c

        lax.fori_loop(0, tt, issue, 0, unroll=8)

    @pl.when(i == 0)
    def _():
        gather(pos_ref, 0)

    @pl.when(i + 1 < n)
    def _():
        gather(posn_ref, 1 - slot)

    pltpu.make_async_copy(ys_ref.at[pl.ds(0, TOP_K * tt * ROW_CHUNKS), :], rows_ref.at[slot], sem.at[slot]).wait()

    gate = gate_ref[...]
    x = x_ref[...]
    for k in range(TOP_K):
        tiles = rows_ref[slot, pl.ds(k * tt * ROW_CHUNKS, tt * ROW_CHUNKS), :]
        x = x + gate[:, k:k + 1] * pltpu.einshape("(rc)l->r(cl)", tiles, c=ROW_CHUNKS)

    hg = _rms(x, pnw_ref[...]).astype(BF16)
    pg = _sigmoid(jnp.dot(hg, wpg_ref[...], preferred_element_type=F32))
    x = x + pg * jnp.dot(p_ref[...].astype(BF16), wpp_ref[...], preferred_element_type=F32)
    o_ref[...] = _rms(x, fnw_ref[...])


def _combine(pos_flat, gate, x1, p2d, ple_nw, w_pg, w_pp, final_nw, y_sorted, tt=256):
    t = x1.shape[0]
    tt = min(tt, t)
    kern = functools.partial(_combine_kernel, tt=tt)
    vec = pl.BlockSpec((1, D_MODEL), lambda i: (0, 0))
    nt = t // tt
    return pl.pallas_call(
        kern,
        grid=(nt,),
        in_specs=[
            pl.BlockSpec((tt * TOP_K,), lambda i: (i,), memory_space=pltpu.SMEM),
            pl.BlockSpec((tt * TOP_K,), lambda i: (jnp.minimum(i + 1, nt - 1),), memory_space=pltpu.SMEM),
            pl.BlockSpec((tt, LANES), lambda i: (i, 0)),
            pl.BlockSpec((tt, D_MODEL), lambda i: (i, 0)),
            pl.BlockSpec((tt, PLE_DIM), lambda i: (i, 0)),
            vec,
            pl.BlockSpec((D_MODEL, D_MODEL), lambda i: (0, 0)),
            pl.BlockSpec((PLE_DIM, D_MODEL), lambda i: (0, 0)),
            vec,
            pl.BlockSpec(memory_space=pl.ANY),
        ],
        out_specs=pl.BlockSpec((tt, D_MODEL), lambda i: (i, 0)),
        out_shape=jax.ShapeDtypeStruct((t, D_MODEL), F32),
        scratch_shapes=[pltpu.VMEM((2, TOP_K * tt * ROW_CHUNKS, LANES), F32), pltpu.SemaphoreType.DMA((2,))],
        compiler_params=_cparams(("arbitrary",)),
        name="combine",
    )(pos_flat, pos_flat, gate, x1, p2d, ple_nw, w_pg, w_pp, final_nw, y_sorted)


def _routing_plan(counts, idx, rank, tm, n_rows):
    counts = counts.astype(jnp.int32)
    ends = jnp.cumsum(counts)
    starts = ends - counts
    pos = (starts[idx] + rank).reshape(-1)
    n_tiles = n_rows // tm
    n_items = n_tiles + N_EXPERTS - 1
    first_tile = starts // tm
    last_tile = jnp.maximum(ends - 1, 0) // tm
    items_e = jnp.where(counts > 0, last_tile - first_tile + 1, 0)
    item_end = jnp.cumsum(items_e)
    item_start = item_end - items_e
    total = item_end[-1]
    w = jnp.arange(n_items, dtype=jnp.int32)
    e_w = jnp.minimum(jnp.sum(w[:, None] >= item_end[None, :], axis=1).astype(jnp.int32), N_EXPERTS - 1)
    tile_w = first_tile[e_w] + (w - item_start[e_w])
    valid = w < total
    last = jnp.maximum(total - 1, 0)
    e_w = jnp.where(valid, e_w, e_w[last])
    tile_w = jnp.where(valid, tile_w, tile_w[last])
    prev_tile = jnp.concatenate([jnp.full((1,), -1, jnp.int32), tile_w[:-1]])
    first = (tile_w != prev_tile) & valid
    prev_e = jnp.concatenate([jnp.full((1,), -1, jnp.int32), e_w[:-1]])
    efirst = (e_w != prev_e) & valid
    starts_ext = jnp.concatenate([starts, ends[-1:]]).astype(jnp.int32)
    return (pos.astype(jnp.int32), tile_w.astype(jnp.int32), e_w, first.astype(jnp.int32), efirst.astype(jnp.int32),
            valid.astype(jnp.int32), starts_ext)


def kernel(x, p, mix_norm_w, w_in, lambda_q1, lambda_k1, lambda_q2, lambda_k2, da_head_norm_w, w_attn_branch, conv_w, conv_b, dt_bias, a_log, d_skip, ssd_norm_w, w_ssd_branch, w_out, moe_norm_w, router_w, router_b, w_up, b_up, w_down, b_down, ple_norm_w, w_ple_gate, w_ple_proj, final_norm_w):
    batch, seq, _ = x.shape
    t = batch * seq
    depth = w_in.shape[0]
    assert depth == 1, "the final RMSNorm is fused into the layer's last kernel"
    x2d = x.reshape(t, D_MODEL)

    log2e = math.log2(math.e)
    slopes = jnp.asarray([log2e * 2.0 ** (-8.0 * (h + 1) / DA_HEADS) for h in range(DA_HEADS)], F32)
    q_scale = log2e * DA_QK_DIM ** -0.5
    head_ids = jnp.arange(SSD_D_INNER, dtype=jnp.int32) // SSD_HEAD_DIM
    expand = (jnp.arange(LANES, dtype=jnp.int32)[:, None] == head_ids[None, :]).astype(BF16)

    def pad_lanes(v, fill=0.0):
        return jnp.pad(v.reshape(1, -1), ((0, 0), (0, LANES - v.shape[-1])), constant_values=fill)

    for i in range(depth):
        wi = w_in[i]
        w_main = jnp.concatenate([
            wi[:, _R_Z:_R_Z + SSD_D_INNER], wi[:, _R_Q:_R_Q + 1024] * q_scale, wi[:, _R_K:_R_K + 1024],
            wi[:, _R_GA:_R_GA + 1024], wi[:, _R_GS:_R_GS + 1024], wi[:, _R_XBC:_R_XBC + SSD_CONV_DIM],
            wi[:, _R_V:_R_V + 1024]], axis=1).astype(BF16)
        w_dt = jnp.pad(wi[:, _R_DT:_R_DT + SSD_HEADS], ((0, 0), (0, LANES - SSD_HEADS))).astype(BF16)

        proj, dt_raw, v_t = _in_proj(x2d, mix_norm_w[i].reshape(1, -1), w_main, w_dt)

        lambda_init = 0.8 - 0.6 * math.exp(-0.3 * i)
        lam = (jnp.exp(jnp.sum(lambda_q1[i] * lambda_k1[i])) - jnp.exp(jnp.sum(lambda_q2[i] * lambda_k2[i]))
               + lambda_init).reshape(1).astype(F32)
        y_attn = _diff_attn(proj, v_t, slopes, lam, da_head_norm_w[i].reshape(1, -1), batch, seq, lambda_init)

        y_ssd = _ssd(proj, dt_raw, conv_w[i], conv_b[i].reshape(1, -1), pad_lanes(dt_bias[i]), pad_lanes(a_log[i]),
                     jnp.repeat(d_skip[i], SSD_HEAD_DIM).reshape(1, -1), ssd_norm_w[i].reshape(1, -1), expand,
                     batch, seq)

        x1 = _merge(y_attn, y_ssd, proj, x2d, w_attn_branch[i].astype(BF16), w_ssd_branch[i].astype(BF16),
                    w_out[i].astype(BF16))

        rw = jnp.stack(_split2(jnp.pad(router_w[i], ((0, 0), (0, LANES - N_EXPERTS)))))
        h2, idx, gate, rank, counts = _router(x1, moe_norm_w[i].reshape(1, -1), rw, pad_lanes(router_b[i]))

        tm = 512
        n_rows = t * TOP_K
        pos, item_tile, item_exp, item_first, item_efirst, item_valid, starts = _routing_plan(
            counts[0, :N_EXPERTS], idx[:, :TOP_K], rank[:, :TOP_K], tm, n_rows)

        x_sorted = _dispatch(h2, pos)
        y_sorted = _experts(x_sorted, item_tile, item_exp, item_first, item_efirst, item_valid, starts,
                            w_up[i], b_up[i].reshape(N_EXPERTS, 1, -1),
                            w_down[i], b_down[i].reshape(N_EXPERTS, 1, -1), tm)

        x2d = _combine(pos, gate, x1, p[i].reshape(t, PLE_DIM), ple_norm_w[i].reshape(1, -1),
                       w_ple_gate[i].astype(BF16), w_ple_proj[i].astype(BF16), final_norm_w.reshape(1, -1), y_sorted)
    return x2d.reshape(batch, seq, D_MODEL)
```

```python
import functools
import math

import jax
import jax.numpy as jnp
from jax import lax
from jax.experimental import pallas as pl
from jax.experimental.pallas import tpu as pltpu

F32 = jnp.float32
BF16 = jnp.bfloat16

D_MODEL = 1024
CHUNK = 64
PLE_DIM = 256
RMS_EPS = 1e-6

DA_HEADS = 8
DA_QK_DIM = 64
DA_V_DIM = 128

SSD_D_INNER = 2048
SSD_HEAD_DIM = 64
SSD_HEADS = 32
SSD_GROUPS = 4
SSD_STATE = 128
SSD_CONV = 4
SSD_CONV_DIM = 3072

N_EXPERTS = 32
TOP_K = 4
D_EXPERT = 1024
SWIGLU_LIMIT = 7.0
SWIGLU_ALPHA = 1.702

LANES = 128

COL_Z = 0
COL_Q = 2048
COL_K = 3072
COL_GA = 4096
COL_GS = 5120
COL_XBC = 6144
PROJ_W = 9216

ATTN_TILE = 256
ATTN_Q_TILE = 512
SUM_ROWS = 16

_R_Q, _R_K, _R_V, _R_Z, _R_XBC, _R_DT, _R_GA, _R_GS = 0, 1024, 2048, 3072, 5120, 8192, 8224, 9248

VMEM_LIMIT = 56 * 1024 * 1024


def _cparams(sem):
    return pltpu.CompilerParams(dimension_semantics=sem, vmem_limit_bytes=VMEM_LIMIT)


def _rms(x, w):
    return x * lax.rsqrt(jnp.mean(x * x, axis=-1, keepdims=True) + RMS_EPS) * w


ROW_CHUNKS = D_MODEL // LANES


def _rows_to_tiles(ref, val, n):
    for c in range(ROW_CHUNKS):
        ref[pl.ds(c, n, stride=ROW_CHUNKS), :] = val[:, c * LANES:(c + 1) * LANES]


def _tiles_to_rows(ref, n, base=0):
    return jnp.concatenate([ref[pl.ds(base + c, n, stride=ROW_CHUNKS), :] for c in range(ROW_CHUNKS)], axis=1)


def _sigmoid(x):
    return 1.0 / (1.0 + jnp.exp2(x * (-math.log2(math.e))))


def _inproj_kernel(x_ref, nw_ref, w_ref, wdt_ref, o_ref, dt_ref, vt_ref, h_ref, *, tm, n_proj):
    j = pl.program_id(1)

    @pl.when(j == 0)
    def _():
        hb = _rms(x_ref[...], nw_ref[...]).astype(BF16)
        h_ref[...] = hb
        dt_ref[...] = jnp.dot(hb, wdt_ref[...], preferred_element_type=F32)

    acc = jnp.dot(h_ref[...], w_ref[...], preferred_element_type=F32)

    @pl.when(j < n_proj)
    def _():
        o_ref[...] = acc.astype(BF16)

    @pl.when(j == n_proj)
    def _():
        vt = acc.T
        for c in range(tm // ATTN_TILE):
            vt_ref[c] = vt[:, c * ATTN_TILE:(c + 1) * ATTN_TILE].astype(BF16)


def _in_proj(x2d, norm_w, w_main, w_dt, tm=2048):
    t = x2d.shape[0]
    tm = min(tm, t)
    tn = DA_HEADS * DA_V_DIM
    n_proj = PROJ_W // tn
    return pl.pallas_call(
        functools.partial(_inproj_kernel, tm=tm, n_proj=n_proj),
        grid=(t // tm, n_proj + 1),
        in_specs=[
            pl.BlockSpec((tm, D_MODEL), lambda i, j: (i, 0)),
            pl.BlockSpec((1, D_MODEL), lambda i, j: (0, 0)),
            pl.BlockSpec((D_MODEL, tn), lambda i, j: (0, j)),
            pl.BlockSpec((D_MODEL, LANES), lambda i, j: (0, 0)),
        ],
        out_specs=[
            pl.BlockSpec((tm, tn), lambda i, j: (i, jnp.minimum(j, n_proj - 1))),
            pl.BlockSpec((tm, LANES), lambda i, j: (i, 0)),
            pl.BlockSpec((tm // ATTN_TILE, tn, ATTN_TILE), lambda i, j: (i, 0, 0)),
        ],
        out_shape=[
            jax.ShapeDtypeStruct((t, PROJ_W), BF16),
            jax.ShapeDtypeStruct((t, LANES), F32),
            jax.ShapeDtypeStruct((t // ATTN_TILE, tn, ATTN_TILE), BF16),
        ],
        scratch_shapes=[pltpu.VMEM((tm, D_MODEL), BF16)],
        compiler_params=_cparams(("parallel", "arbitrary")),
        name="in_proj",
    )(x2d, norm_w, w_main, w_dt)


def _attn_kernel(slopes_ref, lam_ref, q_ref, k_ref, vt_ref, nw_ref, o_ref,
                 bias_ref, ka_ref, kb_ref, qc_ref, s_ref, p_ref, al_ref, m_ref, a_ref, *, tq, tk, out_scale):
    h = pl.program_id(1)
    i = pl.program_id(2)
    slope = slopes_ref[h]
    lam = lam_ref[0]
    r = tq // tk
    n_past = r * i
    n_tiles = n_past + r
    seq = k_ref.shape[0]
    aux = 16

    @pl.when(i == 0)
    def _():
        kk = lax.broadcasted_iota(jnp.int32, (tk, tq), 0)
        qq = lax.broadcasted_iota(jnp.int32, (tk, tq), 1)
        bias_ref[0] = jnp.zeros((tk, tq), F32)
        for d in range(r):
            kd = kk + d * tk
            true = jnp.where((kd // CHUNK) <= (qq // CHUNK), -slope * jnp.abs(qq - kd).astype(F32), -jnp.inf)
            bias_ref[1 + d] = true + slope * (qq - kk).astype(F32)
        rows = 512
        lane = lax.broadcasted_iota(jnp.int32, (rows, LANES), 1)
        krel = (lax.broadcasted_iota(jnp.int32, (rows, LANES), 0) % tk).astype(F32)
        lo = lane - DA_QK_DIM
        aux_a = jnp.where(lo < 3, krel, jnp.where(lo < 6, 1.0, 0.0))
        aux_b = jnp.where(lane < 3, krel, jnp.where(lane < 6, 1.0, 0.0))
        for r0 in range(0, seq, rows):
            k = k_ref[r0:r0 + rows, :].astype(F32)
            ka_ref[r0:r0 + rows, :] = jnp.where(lane < DA_QK_DIM, k, aux_a).astype(BF16)
            kb_ref[r0:r0 + rows, :] = jnp.where(lane >= DA_QK_DIM, k, aux_b).astype(BF16)

    qt = q_ref[...].astype(F32).T.astype(BF16)
    arow = lax.broadcasted_iota(jnp.int32, (aux, tq), 0)
    qrel = lax.broadcasted_iota(jnp.int32, (aux, tq), 1).astype(F32)
    pieces = _split3(jnp.full((aux, tq), slope, F32)) + _split3(-slope * qrel)
    extra = jnp.zeros((aux, tq), F32)
    for n, piece in enumerate(pieces):
        extra = jnp.where(arow == n, piece.astype(F32), extra)
    extra = extra.astype(BF16)
    pad = jnp.zeros((DA_QK_DIM - aux, tq), BF16)
    qc_ref[:, 0:tq] = jnp.concatenate([qt[0:DA_QK_DIM], extra, pad], axis=0)
    qc_ref[:, tq:2 * tq] = jnp.concatenate([extra, pad, qt[DA_QK_DIM:]], axis=0)

    m_ref[...] = jnp.full(m_ref.shape, -jnp.inf, F32)
    a_ref[...] = jnp.zeros(a_ref.shape, F32)

    def scores(j):
        rows = pl.ds(pl.multiple_of(j * tk, tk), tk)
        return (jnp.dot(ka_ref[rows, :], qc_ref[:, 0:tq], preferred_element_type=F32),
                jnp.dot(kb_ref[rows, :], qc_ref[:, tq:2 * tq], preferred_element_type=F32))

    def store_scores(blocks):
        s_ref[:, 0:tq] = blocks[0]
        s_ref[:, tq:2 * tq] = blocks[1]

    def softmax_step(sel, cj):
        for g in range(2 * tq // LANES):
            cols = slice(g * LANES, (g + 1) * LANES)
            bcols = slice((g * LANES) % tq, (g * LANES) % tq + LANES)
            s = s_ref[:, cols]
            if sel is not None:
                s = s + bias_ref[sel, :, bcols]
            m_prev = m_ref[:, cols]
            m_cur = jnp.maximum(m_prev, jnp.max(s, axis=0, keepdims=True) + cj)
            alpha = jnp.exp2(m_prev - m_cur)
            p = jnp.exp2(s + (cj - m_cur))
            m_ref[:, cols] = m_cur
            al_ref[:, cols] = alpha
            p_ref[:, cols] = p.astype(BF16)

    def pv_step(j):
        vt = jnp.concatenate([vt_ref[j], jnp.ones((SUM_ROWS, tk), BF16)], axis=0)
        for g in range(2 * tq // 256):
            cols = slice(g * 256, (g + 1) * 256)
            a_ref[:, cols] = al_ref[:, cols] * a_ref[:, cols] + jnp.dot(vt, p_ref[:, cols], preferred_element_type=F32)

    def past_cj(j):
        return -slope * (i * tq - j * tk).astype(F32)

    store_scores(scores(0))
    s_next = scores(1)
    softmax_step(jnp.where(i > 0, 0, 1), jnp.where(i > 0, past_cj(0), 0.0))
    store_scores(s_next)

    def body(j, carry):
        s_next = scores(j + 1)
        pv_step(j - 1)
        softmax_step(None, past_cj(j))
        store_scores(s_next)
        return carry

    lax.fori_loop(1, n_past, body, 0)

    for d in range(r):
        t = n_past + d

        def overlap_tile(t=t, d=d):
            if d + 1 < r:
                s_next = scores(t + 1)
            pv_step(t - 1)
            softmax_step(1 + d, 0.0)
            if d + 1 < r:
                store_scores(s_next)

        if d == 0:
            pl.when(i > 0)(overlap_tile)
        else:
            overlap_tile()
    pv_step(n_tiles - 1)

    on = a_ref[0:DA_V_DIM, :] / a_ref[DA_V_DIM:DA_V_DIM + 1, :]
    ot = on[:, :tq] - lam * on[:, tq:]
    ot = ot * lax.rsqrt(jnp.mean(ot * ot, axis=0, keepdims=True) + RMS_EPS)
    o_ref[...] = (ot.T * nw_ref[...] * out_scale).astype(BF16)


def _diff_attn(proj, v_t, slopes, lam, head_norm_w, batch, seq, lambda_init):
    tk = ATTN_TILE
    tq = min(ATTN_Q_TILE, seq)
    nq = seq // tq
    nk = seq // tk
    t = batch * seq
    qb, kb = COL_Q // LANES, COL_K // LANES
    assert tq % tk == 0 and tq >= 2 * tk, "the peeled first/last key tiles need at least two tiles per step"
    kern = functools.partial(_attn_kernel, tq=tq, tk=tk, out_scale=1.0 - lambda_init)
    smem = pl.BlockSpec(memory_space=pltpu.SMEM)
    return pl.pallas_call(
        kern,
        grid=(batch, DA_HEADS, nq),
        in_specs=[
            smem, smem,
            pl.BlockSpec((tq, LANES), lambda b, h, i: (b * nq + i, qb + h)),
            pl.BlockSpec((seq, LANES), lambda b, h, i: (b, kb + h)),
            pl.BlockSpec((nk, DA_V_DIM, tk), lambda b, h, i: (b, h, 0)),
            pl.BlockSpec((1, LANES), lambda b, h, i: (0, 0)),
        ],
        out_specs=pl.BlockSpec((tq, LANES), lambda b, h, i: (b * nq + i, h)),
        out_shape=jax.ShapeDtypeStruct((t, DA_HEADS * DA_V_DIM), BF16),
        scratch_shapes=[
            pltpu.VMEM((1 + tq // tk, tk, tq), F32),
            pltpu.VMEM((seq, LANES), BF16),
            pltpu.VMEM((seq, LANES), BF16),
            pltpu.VMEM((LANES, 2 * tq), BF16),
            pltpu.VMEM((tk, 2 * tq), F32),
            pltpu.VMEM((tk, 2 * tq), BF16),
            pltpu.VMEM((1, 2 * tq), F32),
            pltpu.VMEM((1, 2 * tq), F32),
            pltpu.VMEM((DA_V_DIM + SUM_ROWS, 2 * tq), F32),
        ],
        compiler_params=_cparams(("parallel", "parallel", "arbitrary")),
        name="diff_attn",
    )(slopes, lam, proj, proj, v_t, head_norm_w)


def _split2(x):
    hi = x.astype(BF16)
    lo = (x - hi.astype(F32)).astype(BF16)
    return hi, lo


def _split3(x):
    hi = x.astype(BF16)
    r = x - hi.astype(F32)
    mid = r.astype(BF16)
    lo = (r - mid.astype(F32)).astype(BF16)
    return hi, mid, lo


def _ssd_kernel(xbc_ref, z_ref, dt_ref, cw_ref, cb_ref, dtb_ref, alog_ref, dsk_ref, nw_ref, e_ref,
                o_ref, ubuf, shift_ref, xs_scr, bc_scr, acum_scr, acb, dfs_scr, xdt_scr, xw_scr, y_scr, state, *, tr):
    i = pl.program_id(1)
    nch = tr // CHUNK
    gw = SSD_D_INNER // SSD_GROUPS
    pad = 8

    head = 16

    @pl.when(i == 0)
    def _():
        ubuf[0:pad, :] = jnp.zeros((pad, SSD_CONV_DIM), F32)
        state[...] = jnp.zeros_like(state)
        r_s = lax.broadcasted_iota(jnp.int32, (tr, tr), 0)
        c_s = lax.broadcasted_iota(jnp.int32, (tr, tr), 1)
        for d in range(1, SSD_CONV):
            shift_ref[(d - 1) * tr:d * tr, :] = jnp.where(c_s == r_s - d, 1.0, 0.0).astype(BF16)

    ubuf[pad:pad + head, :] = xbc_ref[0:head, :].astype(F32)
    cs = 512
    for c0 in range(0, SSD_CONV_DIM, cs):
        ub = xbc_ref[:, c0:c0 + cs]
        sh = jnp.dot(shift_ref[...], ub, preferred_element_type=F32)
        acc = cb_ref[:, c0:c0 + cs] + cw_ref[SSD_CONV - 1:SSD_CONV, c0:c0 + cs] * ub.astype(F32)
        for d in range(1, SSD_CONV):
            acc = acc + cw_ref[SSD_CONV - 1 - d:SSD_CONV - d, c0:c0 + cs] * sh[(d - 1) * tr:d * tr, :]
        u = acc * _sigmoid(acc)
        acc_h = jnp.broadcast_to(cb_ref[:, c0:c0 + cs], (head, cs))
        for k in range(SSD_CONV):
            off = pad - (SSD_CONV - 1) + k
            acc_h = acc_h + cw_ref[k:k + 1, c0:c0 + cs] * ubuf[off:off + head, c0:c0 + cs]
        u_h = acc_h * _sigmoid(acc_h)
        if c0 < SSD_D_INNER:
            xs_scr[:, c0:c0 + cs] = u
            xs_scr[0:head, c0:c0 + cs] = u_h
        else:
            bc_scr[:, c0 - SSD_D_INNER:c0 - SSD_D_INNER + cs] = u.astype(BF16)
            bc_scr[0:head, c0 - SSD_D_INNER:c0 - SSD_D_INNER + cs] = u_h.astype(BF16)
    ubuf[0:pad, :] = xbc_ref[tr - 2 * pad:tr, :].astype(F32)[pad:, :]

    xdt_in = dt_ref[...] + dtb_ref[...]
    dtv = jnp.maximum(xdt_in, 0.0) + jnp.log1p(jnp.exp(-jnp.abs(xdt_in)))
    da = dtv * (-jnp.exp(alog_ref[...]))
    r_i = lax.broadcasted_iota(jnp.int32, (tr, tr), 0)
    c_i = lax.broadcasted_iota(jnp.int32, (tr, tr), 1)
    tril = jnp.where((c_i <= r_i) & ((c_i // CHUNK) == (r_i // CHUNK)), 1.0, 0.0).astype(BF16)
    acum = sum(jnp.dot(tril, piece, preferred_element_type=F32) for piece in _split3(da))
    acum_scr[...] = acum

    a_hi, a_lo = _split2(acum)
    d_hi, d_lo = _split2(dtv)
    for c0 in range(0, SSD_D_INNER, cs):
        e = e_ref[:, c0:c0 + cs]
        ab = jnp.dot(a_hi, e, preferred_element_type=F32) + jnp.dot(a_lo, e, preferred_element_type=F32)
        db = jnp.dot(d_hi, e, preferred_element_type=F32) + jnp.dot(d_lo, e, preferred_element_type=F32)
        acb[:, c0:c0 + cs] = ab
        dfs_scr[:, c0:c0 + cs] = jnp.exp(ab)
        xdt = xs_scr[:, c0:c0 + cs] * db
        xdt_scr[:, c0:c0 + cs] = xdt.astype(BF16)
        for c in range(nch):
            r0 = c * CHUNK
            last = ab[r0 + CHUNK - 1:r0 + CHUNK, :]
            xw_scr[r0:r0 + CHUNK, c0:c0 + cs] = (xdt[r0:r0 + CHUNK, :] * jnp.exp(last - ab[r0:r0 + CHUNK, :])).astype(BF16)

    lane2 = lax.broadcasted_iota(jnp.int32, (CHUNK, LANES), 1)
    row2 = lax.broadcasted_iota(jnp.int32, (CHUNK, LANES), 0)
    causal2 = row2 >= (lane2 % SSD_HEAD_DIM)
    lo_half = lane2 < SSD_HEAD_DIM
    zpad = jnp.zeros((LANES - CHUNK, LANES), F32)

    def chunk_body(c, carry):
        r0 = pl.multiple_of(c * CHUNK, CHUNK)
        rows = pl.ds(r0, CHUNK)
        a_c = acum_scr[rows, :]
        a_t = jnp.concatenate([a_c, zpad], axis=0).T
        a_t_r = pltpu.roll(a_t, SSD_HEAD_DIM, 1)
        last_row = acb[pl.ds(r0 + CHUNK - 1, 1), :]
        for g in range(SSD_GROUPS):
            bg = bc_scr[rows, g * SSD_STATE:(g + 1) * SSD_STATE]
            cg = bc_scr[rows, SSD_GROUPS * SSD_STATE + g * SSD_STATE:SSD_GROUPS * SSD_STATE + (g + 1) * SSD_STATE]
            cbt = lax.dot_general(cg, jnp.concatenate([bg, bg], axis=0), (((1,), (1,)), ((), ())),
                                  preferred_element_type=F32)
            g0 = g * gw
            st = state[g]
            yoff = jnp.dot(cg, st.astype(BF16), preferred_element_type=F32) * dfs_scr[rows, g0:g0 + gw]
            for jj in range(gw // LANES):
                pidx = g * (gw // LANES) + jj
                l0 = pidx * LANES
                arow = a_t[2 * pidx:2 * pidx + 1, :] + a_t_r[2 * pidx + 1:2 * pidx + 2, :]
                seg = acb[rows, l0:l0 + LANES] - arow
                decay = jnp.exp(jnp.where(causal2, seg, -jnp.inf))
                mp = (cbt * decay).astype(BF16)
                xd = xdt_scr[rows, l0:l0 + LANES]
                zb = jnp.zeros_like(xd)
                bd = jnp.concatenate([jnp.where(lo_half, xd, zb), jnp.where(lo_half, zb, xd)], axis=0)
                yd = jnp.dot(mp, bd, preferred_element_type=F32)
                y_scr[rows, l0:l0 + LANES] = yd + yoff[:, jj * LANES:(jj + 1) * LANES]
            cd = jnp.exp(last_row[:, g0:g0 + gw])
            upd = lax.dot_general(bg, xw_scr[rows, g0:g0 + gw], (((0,), (0,)), ((), ())),
                                  preferred_element_type=F32)
            state[g] = st * cd + upd
        return carry

    lax.fori_loop(0, nch, chunk_body, 0, unroll=True)

    for g in range(SSD_GROUPS):
        g0 = g * gw
        y = y_scr[:, g0:g0 + gw] + dsk_ref[:, g0:g0 + gw] * xs_scr[:, g0:g0 + gw]
        zz = z_ref[:, g0:g0 + gw].astype(F32)
        y = y * (zz * _sigmoid(zz))
        o_ref[:, g0:g0 + gw] = _rms(y, nw_ref[:, g0:g0 + gw]).astype(BF16)


def _ssd(proj, dt_raw, conv_w, conv_b, dt_bias, a_log, d_skip_b, norm_w, expand, batch, seq, tr=256):
    t = batch * seq
    nr = seq // tr
    kern = functools.partial(_ssd_kernel, tr=tr)
    const = lambda shape: pl.BlockSpec(shape, lambda b, i: (0, 0))
    return pl.pallas_call(
        kern,
        grid=(batch, nr),
        in_specs=[
            pl.BlockSpec((tr, SSD_CONV_DIM), lambda b, i: (b * nr + i, COL_XBC // SSD_CONV_DIM)),
            pl.BlockSpec((tr, SSD_D_INNER), lambda b, i: (b * nr + i, COL_Z // SSD_D_INNER)),
            pl.BlockSpec((tr, LANES), lambda b, i: (b * nr + i, 0)),
            const((SSD_CONV, SSD_CONV_DIM)),
            const((1, SSD_CONV_DIM)),
            const((1, LANES)),
            const((1, LANES)),
            const((1, SSD_D_INNER)),
            const((1, SSD_D_INNER)),
            const((LANES, SSD_D_INNER)),
        ],
        out_specs=pl.BlockSpec((tr, SSD_D_INNER), lambda b, i: (b * nr + i, 0)),
        out_shape=jax.ShapeDtypeStruct((t, SSD_D_INNER), BF16),
        scratch_shapes=[
            pltpu.VMEM((8 + 16, SSD_CONV_DIM), F32),
            pltpu.VMEM(((SSD_CONV - 1) * tr, tr), BF16),
            pltpu.VMEM((tr, SSD_D_INNER), F32),
            pltpu.VMEM((tr, 2 * SSD_GROUPS * SSD_STATE), BF16),
            pltpu.VMEM((tr, LANES), F32),
            pltpu.VMEM((tr, SSD_D_INNER), F32),
            pltpu.VMEM((tr, SSD_D_INNER), F32),
            pltpu.VMEM((tr, SSD_D_INNER), BF16),
            pltpu.VMEM((tr, SSD_D_INNER), BF16),
            pltpu.VMEM((tr, SSD_D_INNER), F32),
            pltpu.VMEM((SSD_GROUPS, SSD_STATE, SSD_D_INNER // SSD_GROUPS), F32),
        ],
        compiler_params=_cparams(("parallel", "arbitrary")),
        name="ssd",
    )(proj, proj, dt_raw, conv_w, conv_b, dt_bias, a_log, d_skip_b, norm_w, expand)


def _merge_kernel(ya_ref, ys_ref, ga_ref, gs_ref, x_ref, wa_ref, wb_ref, wo_ref, o_ref):
    a = jnp.dot(ya_ref[...], wa_ref[...], preferred_element_type=F32)
    s = jnp.dot(ys_ref[...], wb_ref[...], preferred_element_type=F32)
    merged = _sigmoid(ga_ref[...].astype(F32)) * a + _sigmoid(gs_ref[...].astype(F32)) * s
    o_ref[...] = x_ref[...] + jnp.dot(merged.astype(BF16), wo_ref[...], preferred_element_type=F32)


def _merge(y_attn, y_ssd, proj, x2d, w_a, w_b, w_o, tm=512):
    t = x2d.shape[0]
    tm = min(tm, t)
    full = lambda r, c: pl.BlockSpec((r, c), lambda i: (0, 0))
    return pl.pallas_call(
        _merge_kernel,
        grid=(t // tm,),
        in_specs=[
            pl.BlockSpec((tm, D_MODEL), lambda i: (i, 0)),
            pl.BlockSpec((tm, SSD_D_INNER), lambda i: (i, 0)),
            pl.BlockSpec((tm, D_MODEL), lambda i: (i, COL_GA // D_MODEL)),
            pl.BlockSpec((tm, D_MODEL), lambda i: (i, COL_GS // D_MODEL)),
            pl.BlockSpec((tm, D_MODEL), lambda i: (i, 0)),
            full(D_MODEL, D_MODEL), full(SSD_D_INNER, D_MODEL), full(D_MODEL, D_MODEL),
        ],
        out_specs=pl.BlockSpec((tm, D_MODEL), lambda i: (i, 0)),
        out_shape=jax.ShapeDtypeStruct((t, D_MODEL), F32),
        compiler_params=_cparams(("parallel",)),
        name="merge",
    )(y_attn, y_ssd, proj, proj, x2d, w_a, w_b, w_o)


def _router_kernel(x_ref, nw_ref, rw_ref, rb_ref, h_ref, idx_ref, gate_ref, rank_ref, cnt_ref, carry_ref, *, tt):
    i = pl.program_id(0)

    @pl.when(i == 0)
    def _():
        carry_ref[...] = jnp.zeros_like(carry_ref)

    h = _rms(x_ref[...], nw_ref[...])
    h_ref[...] = pltpu.einshape("r(cl)->(rc)l", h, c=ROW_CHUNKS)
    h_hi, h_lo = _split2(h)
    logits = (jnp.dot(h_hi, rw_ref[0], preferred_element_type=F32) + jnp.dot(h_hi, rw_ref[1], preferred_element_type=F32)
              + jnp.dot(h_lo, rw_ref[0], preferred_element_type=F32)) + rb_ref[...]
    lane = lax.broadcasted_iota(jnp.int32, (tt, LANES), 1)
    work = jnp.where(lane < N_EXPERTS, logits, -jnp.inf)

    vals, idxs, hots = [], [], []
    for _ in range(TOP_K):
        m = jnp.max(work, axis=-1, keepdims=True)
        idx = jnp.min(jnp.where(work == m, lane, LANES), axis=-1, keepdims=True)
        hot = lane == idx
        vals.append(m)
        idxs.append(idx)
        hots.append(hot)
        work = jnp.where(hot, -jnp.inf, work)

    exps = [jnp.exp(v - vals[0]) for v in vals]
    denom = exps[0] + exps[1] + exps[2] + exps[3]

    hot_sum = sum(jnp.where(hot, 1.0, 0.0) for hot in hots)
    r_i = lax.broadcasted_iota(jnp.int32, (tt, tt), 0)
    c_i = lax.broadcasted_iota(jnp.int32, (tt, tt), 1)
    strict = jnp.where(c_i < r_i, 1.0, 0.0).astype(BF16)
    prefix = jnp.dot(strict, hot_sum.astype(BF16), preferred_element_type=F32) + carry_ref[...]

    idx_out = jnp.zeros((tt, LANES), jnp.int32)
    rank_out = jnp.zeros((tt, LANES), jnp.int32)
    gate_out = jnp.zeros((tt, LANES), F32)
    for k in range(TOP_K):
        rank_k = jnp.sum(jnp.where(hots[k], prefix, 0.0), axis=-1, keepdims=True).astype(jnp.int32)
        sel = lane == k
        idx_out = jnp.where(sel, idxs[k], idx_out)
        rank_out = jnp.where(sel, rank_k, rank_out)
        gate_out = jnp.where(sel, exps[k] / denom, gate_out)
    idx_ref[...] = idx_out
    rank_ref[...] = rank_out
    gate_ref[...] = gate_out

    carry_ref[...] = carry_ref[...] + jnp.sum(hot_sum, axis=0, keepdims=True)
    cnt_ref[...] = carry_ref[...]


def _router(x1, norm_w, router_w_pad, router_b_pad, tt=512):
    t = x1.shape[0]
    tt = min(tt, t)
    kern = functools.partial(_router_kernel, tt=tt)
    row = pl.BlockSpec((tt, LANES), lambda i: (i, 0))
    return pl.pallas_call(
        kern,
        grid=(t // tt,),
        in_specs=[
            pl.BlockSpec((tt, D_MODEL), lambda i: (i, 0)),
            pl.BlockSpec((1, D_MODEL), lambda i: (0, 0)),
            pl.BlockSpec((2, D_MODEL, LANES), lambda i: (0, 0, 0)),
            pl.BlockSpec((1, LANES), lambda i: (0, 0)),
        ],
        out_specs=[pl.BlockSpec((tt * ROW_CHUNKS, LANES), lambda i: (i, 0)), row, row, row,
                   pl.BlockSpec((1, LANES), lambda i: (0, 0))],
        out_shape=[
            jax.ShapeDtypeStruct((t * ROW_CHUNKS, LANES), F32),
            jax.ShapeDtypeStruct((t, LANES), jnp.int32),
            jax.ShapeDtypeStruct((t, LANES), F32),
            jax.ShapeDtypeStruct((t, LANES), jnp.int32),
            jax.ShapeDtypeStruct((1, LANES), F32),
        ],
        scratch_shapes=[pltpu.VMEM((1, LANES), F32)],
        compiler_params=_cparams(("arbitrary",)),
        name="router",
    )(x1, norm_w, router_w_pad, router_b_pad)


def _dispatch_kernel(pos_ref, h_ref, o_ref, sem, *, tt):
    def copy(t, k):
        src = pl.multiple_of(t * ROW_CHUNKS, ROW_CHUNKS)
        dst = pl.multiple_of(pos_ref[t * TOP_K + k] * ROW_CHUNKS, ROW_CHUNKS)
        return pltpu.make_async_copy(h_ref.at[pl.ds(src, ROW_CHUNKS), :], o_ref.at[pl.ds(dst, ROW_CHUNKS), :], sem)

    def issue(t, c):
        for k in range(TOP_K):
            copy(t, k).start()
        return c

    lax.fori_loop(0, tt, issue, 0, unroll=8)

    for _ in range(TOP_K):
        pltpu.make_async_copy(h_ref, o_ref.at[pl.ds(0, tt * ROW_CHUNKS), :], sem).wait()


def _dispatch(h2, pos_flat, tt=1024):
    t = h2.shape[0] // ROW_CHUNKS
    tt = min(tt, t)
    kern = functools.partial(_dispatch_kernel, tt=tt)
    return pl.pallas_call(
        kern,
        grid=(t // tt,),
        in_specs=[
            pl.BlockSpec((tt * TOP_K,), lambda i: (i,), memory_space=pltpu.SMEM),
            pl.BlockSpec((tt * ROW_CHUNKS, LANES), lambda i: (i, 0)),
        ],
        out_specs=pl.BlockSpec(memory_space=pl.ANY),
        out_shape=jax.ShapeDtypeStruct((t * TOP_K * ROW_CHUNKS, LANES), F32),
        scratch_shapes=[pltpu.SemaphoreType.DMA(())],
        compiler_params=_cparams(("arbitrary",)),
        name="dispatch",
    )(pos_flat, h2)


def _experts_kernel(tile_ref, exp_ref, first_ref, efirst_ref, valid_ref, start_ref,
                    x_ref, wu_ref, bu_ref, wd_ref, bd_ref, o_ref, wub_ref, wdb_ref, *, tm):
    w = pl.program_id(0)

    @pl.when(efirst_ref[w] == 1)
    def _():
        for c0 in range(0, 2 * D_EXPERT, 512):
            wub_ref[:, c0:c0 + 512] = wu_ref[0, :, c0:c0 + 512].astype(BF16)
        for c0 in range(0, D_MODEL, 512):
            wdb_ref[:, c0:c0 + 512] = wd_ref[0, :, c0:c0 + 512].astype(BF16)

    @pl.when(valid_ref[w] == 1)
    def _():
        e = exp_ref[w]
        x = pltpu.einshape("(rc)l->r(cl)", x_ref[...], c=ROW_CHUNKS).astype(BF16)
        gu = jnp.dot(x, wub_ref[...], preferred_element_type=F32) + bu_ref[0]
        g = jnp.minimum(gu[:, :D_EXPERT], SWIGLU_LIMIT)
        u = jnp.clip(gu[:, D_EXPERT:], -SWIGLU_LIMIT, SWIGLU_LIMIT)
        act = (u + 1.0) * (g * _sigmoid(SWIGLU_ALPHA * g))
        out = jnp.dot(act.astype(BF16), wdb_ref[...], preferred_element_type=F32) + bd_ref[0]
        rows = tile_ref[w] * tm + lax.broadcasted_iota(jnp.int32, (tm, 1), 0)
        mine = (rows >= start_ref[e]) & (rows < start_ref[e + 1])

        def as_tiles(v):
            return pltpu.einshape("r(cl)->(rc)l", v, c=ROW_CHUNKS)

        @pl.when(first_ref[w] == 1)
        def _():
            o_ref[...] = as_tiles(jnp.where(mine, out, 0.0))

        @pl.when(first_ref[w] == 0)
        def _():
            o_ref[...] = as_tiles(jnp.where(mine, out, _tiles_to_rows(o_ref, tm)))


def _experts(x_sorted, item_tile, item_exp, item_first, item_efirst, item_valid, starts, w_up, b_up, w_down, b_down, tm):
    n = x_sorted.shape[0] // ROW_CHUNKS
    n_items = item_tile.shape[0]
    kern = functools.partial(_experts_kernel, tm=tm)
    grid_spec = pltpu.PrefetchScalarGridSpec(
        num_scalar_prefetch=6,
        grid=(n_items,),
        in_specs=[
            pl.BlockSpec((tm * ROW_CHUNKS, LANES), lambda w, tl, ex, *_: (tl[w], 0)),
            pl.BlockSpec((1, D_MODEL, 2 * D_EXPERT), lambda w, tl, ex, *_: (ex[w], 0, 0)),
            pl.BlockSpec((1, 1, 2 * D_EXPERT), lambda w, tl, ex, *_: (ex[w], 0, 0)),
            pl.BlockSpec((1, D_EXPERT, D_MODEL), lambda w, tl, ex, *_: (ex[w], 0, 0)),
            pl.BlockSpec((1, 1, D_MODEL), lambda w, tl, ex, *_: (ex[w], 0, 0)),
        ],
        out_specs=pl.BlockSpec((tm * ROW_CHUNKS, LANES), lambda w, tl, ex, *_: (tl[w], 0)),
        scratch_shapes=[pltpu.VMEM((D_MODEL, 2 * D_EXPERT), BF16), pltpu.VMEM((D_EXPERT, D_MODEL), BF16)],
    )
    return pl.pallas_call(
        kern,
        grid_spec=grid_spec,
        out_shape=jax.ShapeDtypeStruct((n * ROW_CHUNKS, LANES), F32),
        compiler_params=_cparams(("arbitrary",)),
        name="experts",
    )(item_tile, item_exp, item_first, item_efirst, item_valid, starts, x_sorted, w_up, b_up, w_down, b_down)


def _combine_kernel(pos_ref, posn_ref, gate_ref, x_ref, p_ref, pnw_ref, wpg_ref, wpp_ref, fnw_ref, ys_ref, o_ref,
                    rows_ref, sem, *, tt):
    i = pl.program_id(0)
    n = pl.num_programs(0)
    slot = i % 2

    def gather(idx_ref, dst_slot):
        def issue(t, c):
            for k in range(TOP_K):
                src = pl.multiple_of(idx_ref[t * TOP_K + k] * ROW_CHUNKS, ROW_CHUNKS)
                dst = pl.multiple_of((k * tt + t) * ROW_CHUNKS, ROW_CHUNKS)
                pltpu.make_async_copy(ys_ref.at[pl.ds(src, ROW_CHUNKS), :],
                                      rows_ref.at[dst_slot, pl.ds(dst, ROW_CHUNKS), :], sem.at[dst_slot]).start()
            return c

        lax.fori_loop(0, tt, issue, 0, unroll=8)

    @pl.when(i == 0)
    def _():
        gather(pos_ref, 0)

    @pl.when(i + 1 < n)
    def _():
        gather(posn_ref, 1 - slot)

    pltpu.make_async_copy(ys_ref.at[pl.ds(0, TOP_K * tt * ROW_CHUNKS), :], rows_ref.at[slot], sem.at[slot]).wait()

    gate = gate_ref[...]
    x = x_ref[...]
    for k in range(TOP_K):
        tiles = rows_ref[slot, pl.ds(k * tt * ROW_CHUNKS, tt * ROW_CHUNKS), :]
        x = x + gate[:, k:k + 1] * pltpu.einshape("(rc)l->r(cl)", tiles, c=ROW_CHUNKS)

    hg = _rms(x, pnw_ref[...]).astype(BF16)
    pg = _sigmoid(jnp.dot(hg, wpg_ref[...], preferred_element_type=F32))
    x = x + pg * jnp.dot(p_ref[...].astype(BF16), wpp_ref[...], preferred_element_type=F32)
    o_ref[...] = _rms(x, fnw_ref[...])


def _combine(pos_flat, gate, x1, p2d, ple_nw, w_pg, w_pp, final_nw, y_sorted, tt=256):
    t = x1.shape[0]
    tt = min(tt, t)
    kern = functools.partial(_combine_kernel, tt=tt)
    vec = pl.BlockSpec((1, D_MODEL), lambda i: (0, 0))
    nt = t // tt
    return pl.pallas_call(
        kern,
        grid=(nt,),
        in_specs=[
            pl.BlockSpec((tt * TOP_K,), lambda i: (i,), memory_space=pltpu.SMEM),
            pl.BlockSpec((tt * TOP_K,), lambda i: (jnp.minimum(i + 1, nt - 1),), memory_space=pltpu.SMEM),
            pl.BlockSpec((tt, LANES), lambda i: (i, 0)),
            pl.BlockSpec((tt, D_MODEL), lambda i: (i, 0)),
            pl.BlockSpec((tt, PLE_DIM), lambda i: (i, 0)),
            vec,
            pl.BlockSpec((D_MODEL, D_MODEL), lambda i: (0, 0)),
            pl.BlockSpec((PLE_DIM, D_MODEL), lambda i: (0, 0)),
            vec,
            pl.BlockSpec(memory_space=pl.ANY),
        ],
        out_specs=pl.BlockSpec((tt, D_MODEL), lambda i: (i, 0)),
        out_shape=jax.ShapeDtypeStruct((t, D_MODEL), F32),
        scratch_shapes=[pltpu.VMEM((2, TOP_K * tt * ROW_CHUNKS, LANES), F32), pltpu.SemaphoreType.DMA((2,))],
        compiler_params=_cparams(("arbitrary",)),
        name="combine",
    )(pos_flat, pos_flat, gate, x1, p2d, ple_nw, w_pg, w_pp, final_nw, y_sorted)


def _routing_plan(counts, idx, rank, tm, n_rows):
    counts = counts.astype(jnp.int32)
    ends = jnp.cumsum(counts)
    starts = ends - counts
    pos = (starts[idx] + rank).reshape(-1)
    n_tiles = n_rows // tm
    n_items = n_tiles + N_EXPERTS - 1
    first_tile = starts // tm
    last_tile = jnp.maximum(ends - 1, 0) // tm
    items_e = jnp.where(counts > 0, last_tile - first_tile + 1, 0)
    item_end = jnp.cumsum(items_e)
    item_start = item_end - items_e
    total = item_end[-1]
    w = jnp.arange(n_items, dtype=jnp.int32)
    e_w = jnp.minimum(jnp.sum(w[:, None] >= item_end[None, :], axis=1).astype(jnp.int32), N_EXPERTS - 1)
    tile_w = first_tile[e_w] + (w - item_start[e_w])
    valid = w < total
    last = jnp.maximum(total - 1, 0)
    e_w = jnp.where(valid, e_w, e_w[last])
    tile_w = jnp.where(valid, tile_w, tile_w[last])
    prev_tile = jnp.concatenate([jnp.full((1,), -1, jnp.int32), tile_w[:-1]])
    first = (tile_w != prev_tile) & valid
    prev_e = jnp.concatenate([jnp.full((1,), -1, jnp.int32), e_w[:-1]])
    efirst = (e_w != prev_e) & valid
    starts_ext = jnp.concatenate([starts, ends[-1:]]).astype(jnp.int32)
    return (pos.astype(jnp.int32), tile_w.astype(jnp.int32), e_w, first.astype(jnp.int32), efirst.astype(jnp.int32),
            valid.astype(jnp.int32), starts_ext)


def kernel(x, p, mix_norm_w, w_in, lambda_q1, lambda_k1, lambda_q2, lambda_k2, da_head_norm_w, w_attn_branch, conv_w, conv_b, dt_bias, a_log, d_skip, ssd_norm_w, w_ssd_branch, w_out, moe_norm_w, router_w, router_b, w_up, b_up, w_down, b_down, ple_norm_w, w_ple_gate, w_ple_proj, final_norm_w):
    batch, seq, _ = x.shape
    t = batch * seq
    depth = w_in.shape[0]
    assert depth == 1, "the final RMSNorm is fused into the layer's last kernel"
    x2d = x.reshape(t, D_MODEL)

    log2e = math.log2(math.e)
    slopes = jnp.asarray([log2e * 2.0 ** (-8.0 * (h + 1) / DA_HEADS) for h in range(DA_HEADS)], F32)
    q_scale = log2e * DA_QK_DIM ** -0.5
    head_ids = jnp.arange(SSD_D_INNER, dtype=jnp.int32) // SSD_HEAD_DIM
    expand = (jnp.arange(LANES, dtype=jnp.int32)[:, None] == head_ids[None, :]).astype(BF16)

    def pad_lanes(v, fill=0.0):
        return jnp.pad(v.reshape(1, -1), ((0, 0), (0, LANES - v.shape[-1])), constant_values=fill)

    for i in range(depth):
        wi = w_in[i]
        w_main = jnp.concatenate([
            wi[:, _R_Z:_R_Z + SSD_D_INNER], wi[:, _R_Q:_R_Q + 1024] * q_scale, wi[:, _R_K:_R_K + 1024],
            wi[:, _R_GA:_R_GA + 1024], wi[:, _R_GS:_R_GS + 1024], wi[:, _R_XBC:_R_XBC + SSD_CONV_DIM],
            wi[:, _R_V:_R_V + 1024]], axis=1).astype(BF16)
        w_dt = jnp.pad(wi[:, _R_DT:_R_DT + SSD_HEADS], ((0, 0), (0, LANES - SSD_HEADS))).astype(BF16)

        proj, dt_raw, v_t = _in_proj(x2d, mix_norm_w[i].reshape(1, -1), w_main, w_dt)

        lambda_init = 0.8 - 0.6 * math.exp(-0.3 * i)
        lam = (jnp.exp(jnp.sum(lambda_q1[i] * lambda_k1[i])) - jnp.exp(jnp.sum(lambda_q2[i] * lambda_k2[i]))
               + lambda_init).reshape(1).astype(F32)
        y_attn = _diff_attn(proj, v_t, slopes, lam, da_head_norm_w[i].reshape(1, -1), batch, seq, lambda_init)

        y_ssd = _ssd(proj, dt_raw, conv_w[i], conv_b[i].reshape(1, -1), pad_lanes(dt_bias[i]), pad_lanes(a_log[i]),
                     jnp.repeat(d_skip[i], SSD_HEAD_DIM).reshape(1, -1), ssd_norm_w[i].reshape(1, -1), expand,
                     batch, seq)

        x1 = _merge(y_attn, y_ssd, proj, x2d, w_attn_branch[i].astype(BF16), w_ssd_branch[i].astype(BF16),
                    w_out[i].astype(BF16))

        rw = jnp.stack(_split2(jnp.pad(router_w[i], ((0, 0), (0, LANES - N_EXPERTS)))))
        h2, idx, gate, rank, counts = _router(x1, moe_norm_w[i].reshape(1, -1), rw, pad_lanes(router_b[i]))

        tm = 512
        n_rows = t * TOP_K
        pos, item_tile, item_exp, item_first, item_efirst, item_valid, starts = _routing_plan(
            counts[0, :N_EXPERTS], idx[:, :TOP_K], rank[:, :TOP_K], tm, n_rows)

        x_sorted = _dispatch(h2, pos)
        y_sorted = _experts(x_sorted, item_tile, item_exp, item_first, item_efirst, item_valid, starts,
                            w_up[i], b_up[i].reshape(N_EXPERTS, 1, -1),
                            w_down[i], b_down[i].reshape(N_EXPERTS, 1, -1), tm)

        x2d = _combine(pos, gate, x1, p[i].reshape(t, PLE_DIM), ple_norm_w[i].reshape(1, -1),
                       w_ple_gate[i].astype(BF16), w_ple_proj[i].astype(BF16), final_norm_w.reshape(1, -1), y_sorted)
    return x2d.reshape(batch, seq, D_MODEL)
```

```python
import functools
import math

import jax
import jax.numpy as jnp
from jax import lax
from jax.experimental import pallas as pl
from jax.experimental.pallas import tpu as pltpu

F32 = jnp.float32
BF16 = jnp.bfloat16

D_MODEL = 1024
CHUNK = 64
PLE_DIM = 256
RMS_EPS = 1e-6

DA_HEADS = 8
DA_QK_DIM = 64
DA_V_DIM = 128

SSD_D_INNER = 2048
SSD_HEAD_DIM = 64
SSD_HEADS = 32
SSD_GROUPS = 4
SSD_STATE = 128
SSD_CONV = 4
SSD_CONV_DIM = 3072

N_EXPERTS = 32
TOP_K = 4
D_EXPERT = 1024
SWIGLU_LIMIT = 7.0
SWIGLU_ALPHA = 1.702

LANES = 128

COL_Z = 0
COL_Q = 2048
COL_K = 3072
COL_GA = 4096
COL_GS = 5120
COL_XBC = 6144
PROJ_W = 9216

ATTN_TILE = 256
ATTN_Q_TILE = 512
SUM_ROWS = 16

_R_Q, _R_K, _R_V, _R_Z, _R_XBC, _R_DT, _R_GA, _R_GS = 0, 1024, 2048, 3072, 5120, 8192, 8224, 9248

VMEM_LIMIT = 56 * 1024 * 1024


def _cparams(sem):
    return pltpu.CompilerParams(dimension_semantics=sem, vmem_limit_bytes=VMEM_LIMIT)


def _rms(x, w):
    return x * lax.rsqrt(jnp.mean(x * x, axis=-1, keepdims=True) + RMS_EPS) * w


ROW_CHUNKS = D_MODEL // LANES


def _rows_to_tiles(ref, val, n):
    for c in range(ROW_CHUNKS):
        ref[pl.ds(c, n, stride=ROW_CHUNKS), :] = val[:, c * LANES:(c + 1) * LANES]


def _tiles_to_rows(ref, n, base=0):
    return jnp.concatenate([ref[pl.ds(base + c, n, stride=ROW_CHUNKS), :] for c in range(ROW_CHUNKS)], axis=1)


def _sigmoid(x):
    return 1.0 / (1.0 + jnp.exp2(x * (-math.log2(math.e))))


def _inproj_kernel(x_ref, nw_ref, w_ref, wdt_ref, o_ref, dt_ref, vt_ref, h_ref, *, tm, n_proj):
    j = pl.program_id(1)

    @pl.when(j == 0)
    def _():
        hb = _rms(x_ref[...], nw_ref[...]).astype(BF16)
        h_ref[...] = hb
        dt_ref[...] = jnp.dot(hb, wdt_ref[...], preferred_element_type=F32)

    acc = jnp.dot(h_ref[...], w_ref[...], preferred_element_type=F32)

    @pl.when(j < n_proj)
    def _():
        o_ref[...] = acc.astype(BF16)

    @pl.when(j == n_proj)
    def _():
        vt = acc.T
        for c in range(tm // ATTN_TILE):
            vt_ref[c] = vt[:, c * ATTN_TILE:(c + 1) * ATTN_TILE].astype(BF16)


def _in_proj(x2d, norm_w, w_main, w_dt, tm=2048):
    t = x2d.shape[0]
    tm = min(tm, t)
    tn = DA_HEADS * DA_V_DIM
    n_proj = PROJ_W // tn
    return pl.pallas_call(
        functools.partial(_inproj_kernel, tm=tm, n_proj=n_proj),
        grid=(t // tm, n_proj + 1),
        in_specs=[
            pl.BlockSpec((tm, D_MODEL), lambda i, j: (i, 0)),
            pl.BlockSpec((1, D_MODEL), lambda i, j: (0, 0)),
            pl.BlockSpec((D_MODEL, tn), lambda i, j: (0, j)),
            pl.BlockSpec((D_MODEL, LANES), lambda i, j: (0, 0)),
        ],
        out_specs=[
            pl.BlockSpec((tm, tn), lambda i, j: (i, jnp.minimum(j, n_proj - 1))),
            pl.BlockSpec((tm, LANES), lambda i, j: (i, 0)),
            pl.BlockSpec((tm // ATTN_TILE, tn, ATTN_TILE), lambda i, j: (i, 0, 0)),
        ],
        out_shape=[
            jax.ShapeDtypeStruct((t, PROJ_W), BF16),
            jax.ShapeDtypeStruct((t, LANES), F32),
            jax.ShapeDtypeStruct((t // ATTN_TILE, tn, ATTN_TILE), BF16),
        ],
        scratch_shapes=[pltpu.VMEM((tm, D_MODEL), BF16)],
        compiler_params=_cparams(("parallel", "arbitrary")),
        name="in_proj",
    )(x2d, norm_w, w_main, w_dt)


def _attn_kernel(slopes_ref, lam_ref, q_ref, k_ref, vt_ref, nw_ref, o_ref,
                 bias_ref, ka_ref, kb_ref, qc_ref, s_ref, p_ref, al_ref, m_ref, a_ref, *, tq, tk, out_scale):
    h = pl.program_id(1)
    i = pl.program_id(2)
    slope = slopes_ref[h]
    lam = lam_ref[0]
    r = tq // tk
    n_past = r * i
    n_tiles = n_past + r
    seq = k_ref.shape[0]
    aux = 16

    @pl.when(i == 0)
    def _():
        kk = lax.broadcasted_iota(jnp.int32, (tk, tq), 0)
        qq = lax.broadcasted_iota(jnp.int32, (tk, tq), 1)
        bias_ref[0] = jnp.zeros((tk, tq), F32)
        for d in range(r):
            kd = kk + d * tk
            true = jnp.where((kd // CHUNK) <= (qq // CHUNK), -slope * jnp.abs(qq - kd).astype(F32), -jnp.inf)
            bias_ref[1 + d] = true + slope * (qq - kk).astype(F32)
        rows = 512
        lane = lax.broadcasted_iota(jnp.int32, (rows, LANES), 1)
        krel = (lax.broadcasted_iota(jnp.int32, (rows, LANES), 0) % tk).astype(F32)
        lo = lane - DA_QK_DIM
        aux_a = jnp.where(lo < 3, krel, jnp.where(lo < 6, 1.0, 0.0))
        aux_b = jnp.where(lane < 3, krel, jnp.where(lane < 6, 1.0, 0.0))
        for r0 in range(0, seq, rows):
            k = k_ref[r0:r0 + rows, :].astype(F32)
            ka_ref[r0:r0 + rows, :] = jnp.where(lane < DA_QK_DIM, k, aux_a).astype(BF16)
            kb_ref[r0:r0 + rows, :] = jnp.where(lane >= DA_QK_DIM, k, aux_b).astype(BF16)

    qt = q_ref[...].astype(F32).T.astype(BF16)
    arow = lax.broadcasted_iota(jnp.int32, (aux, tq), 0)
    qrel = lax.broadcasted_iota(jnp.int32, (aux, tq), 1).astype(F32)
    pieces = _split3(jnp.full((aux, tq), slope, F32)) + _split3(-slope * qrel)
    extra = jnp.zeros((aux, tq), F32)
    for n, piece in enumerate(pieces):
        extra = jnp.where(arow == n, piece.astype(F32), extra)
    extra = extra.astype(BF16)
    pad = jnp.zeros((DA_QK_DIM - aux, tq), BF16)
    qc_ref[:, 0:tq] = jnp.concatenate([qt[0:DA_QK_DIM], extra, pad], axis=0)
    qc_ref[:, tq:2 * tq] = jnp.concatenate([extra, pad, qt[DA_QK_DIM:]], axis=0)

    m_ref[...] = jnp.full(m_ref.shape, -jnp.inf, F32)
    a_ref[...] = jnp.zeros(a_ref.shape, F32)

    def scores(j):
        rows = pl.ds(pl.multiple_of(j * tk, tk), tk)
        return (jnp.dot(ka_ref[rows, :], qc_ref[:, 0:tq], preferred_element_type=F32),
                jnp.dot(kb_ref[rows, :], qc_ref[:, tq:2 * tq], preferred_element_type=F32))

    def store_scores(blocks):
        s_ref[:, 0:tq] = blocks[0]
        s_ref[:, tq:2 * tq] = blocks[1]

    def softmax_step(sel, cj):
        for g in range(2 * tq // LANES):
            cols = slice(g * LANES, (g + 1) * LANES)
            bcols = slice((g * LANES) % tq, (g * LANES) % tq + LANES)
            s = s_ref[:, cols]
            if sel is not None:
                s = s + bias_ref[sel, :, bcols]
            m_prev = m_ref[:, cols]
            m_cur = jnp.maximum(m_prev, jnp.max(s, axis=0, keepdims=True) + cj)
            alpha = jnp.exp2(m_prev - m_cur)
            p = jnp.exp2(s + (cj - m_cur))
            m_ref[:, cols] = m_cur
            al_ref[:, cols] = alpha
            p_ref[:, cols] = p.astype(BF16)

    def pv_step(j):
        vt = jnp.concatenate([vt_ref[j], jnp.ones((SUM_ROWS, tk), BF16)], axis=0)
        for g in range(2 * tq // 256):
            cols = slice(g * 256, (g + 1) * 256)
            a_ref[:, cols] = al_ref[:, cols] * a_ref[:, cols] + jnp.dot(vt, p_ref[:, cols], preferred_element_type=F32)

    def past_cj(j):
        return -slope * (i * tq - j * tk).astype(F32)

    store_scores(scores(0))
    s_next = scores(1)
    softmax_step(jnp.where(i > 0, 0, 1), jnp.where(i > 0, past_cj(0), 0.0))
    store_scores(s_next)

    def body(j, carry):
        s_next = scores(j + 1)
        pv_step(j - 1)
        softmax_step(None, past_cj(j))
        store_scores(s_next)
        return carry

    lax.fori_loop(1, n_past, body, 0)

    for d in range(r):
        t = n_past + d

        def overlap_tile(t=t, d=d):
            if d + 1 < r:
                s_next = scores(t + 1)
            pv_step(t - 1)
            softmax_step(1 + d, 0.0)
            if d + 1 < r:
                store_scores(s_next)

        if d == 0:
            pl.when(i > 0)(overlap_tile)
        else:
            overlap_tile()
    pv_step(n_tiles - 1)

    on = a_ref[0:DA_V_DIM, :] / a_ref[DA_V_DIM:DA_V_DIM + 1, :]
    ot = on[:, :tq] - lam * on[:, tq:]
    ot = ot * lax.rsqrt(jnp.mean(ot * ot, axis=0, keepdims=True) + RMS_EPS)
    o_ref[...] = (ot.T * nw_ref[...] * out_scale).astype(BF16)


def _diff_attn(proj, v_t, slopes, lam, head_norm_w, batch, seq, lambda_init):
    tk = ATTN_TILE
    tq = min(ATTN_Q_TILE, seq)
    nq = seq // tq
    nk = seq // tk
    t = batch * seq
    qb, kb = COL_Q // LANES, COL_K // LANES
    assert tq % tk == 0 and tq >= 2 * tk, "the peeled first/last key tiles need at least two tiles per step"
    kern = functools.partial(_attn_kernel, tq=tq, tk=tk, out_scale=1.0 - lambda_init)
    smem = pl.BlockSpec(memory_space=pltpu.SMEM)
    return pl.pallas_call(
        kern,
        grid=(batch, DA_HEADS, nq),
        in_specs=[
            smem, smem,
            pl.BlockSpec((tq, LANES), lambda b, h, i: (b * nq + i, qb + h)),
            pl.BlockSpec((seq, LANES), lambda b, h, i: (b, kb + h)),
            pl.BlockSpec((nk, DA_V_DIM, tk), lambda b, h, i: (b, h, 0)),
            pl.BlockSpec((1, LANES), lambda b, h, i: (0, 0)),
        ],
        out_specs=pl.BlockSpec((tq, LANES), lambda b, h, i: (b * nq + i, h)),
        out_shape=jax.ShapeDtypeStruct((t, DA_HEADS * DA_V_DIM), BF16),
        scratch_shapes=[
            pltpu.VMEM((1 + tq // tk, tk, tq), F32),
            pltpu.VMEM((seq, LANES), BF16),
            pltpu.VMEM((seq, LANES), BF16),
            pltpu.VMEM((LANES, 2 * tq), BF16),
            pltpu.VMEM((tk, 2 * tq), F32),
            pltpu.VMEM((tk, 2 * tq), BF16),
            pltpu.VMEM((1, 2 * tq), F32),
            pltpu.VMEM((1, 2 * tq), F32),
            pltpu.VMEM((DA_V_DIM + SUM_ROWS, 2 * tq), F32),
        ],
        compiler_params=_cparams(("parallel", "parallel", "arbitrary")),
        name="diff_attn",
    )(slopes, lam, proj, proj, v_t, head_norm_w)


def _split2(x):
    hi = x.astype(BF16)
    lo = (x - hi.astype(F32)).astype(BF16)
    return hi, lo


def _split3(x):
    hi = x.astype(BF16)
    r = x - hi.astype(F32)
    mid = r.astype(BF16)
    lo = (r - mid.astype(F32)).astype(BF16)
    return hi, mid, lo


def _ssd_kernel(xbc_ref, z_ref, dt_ref, cw_ref, cb_ref, dtb_ref, alog_ref, dsk_ref, nw_ref, e_ref,
                o_ref, ubuf, shift_ref, xs_scr, bc_scr, acum_scr, acb, dfs_scr, xdt_scr, xw_scr, y_scr, state, *, tr):
    i = pl.program_id(1)
    nch = tr // CHUNK
    gw = SSD_D_INNER // SSD_GROUPS
    pad = 8

    head = 16

    @pl.when(i == 0)
    def _():
        ubuf[0:pad, :] = jnp.zeros((pad, SSD_CONV_DIM), F32)
        state[...] = jnp.zeros_like(state)
        r_s = lax.broadcasted_iota(jnp.int32, (tr, tr), 0)
        c_s = lax.broadcasted_iota(jnp.int32, (tr, tr), 1)
        for d in range(1, SSD_CONV):
            shift_ref[(d - 1) * tr:d * tr, :] = jnp.where(c_s == r_s - d, 1.0, 0.0).astype(BF16)

    ubuf[pad:pad + head, :] = xbc_ref[0:head, :].astype(F32)
    cs = 512
    for c0 in range(0, SSD_CONV_DIM, cs):
        ub = xbc_ref[:, c0:c0 + cs]
        sh = jnp.dot(shift_ref[...], ub, preferred_element_type=F32)
        acc = cb_ref[:, c0:c0 + cs] + cw_ref[SSD_CONV - 1:SSD_CONV, c0:c0 + cs] * ub.astype(F32)
        for d in range(1, SSD_CONV):
            acc = acc + cw_ref[SSD_CONV - 1 - d:SSD_CONV - d, c0:c0 + cs] * sh[(d - 1) * tr:d * tr, :]
        u = acc * _sigmoid(acc)
        acc_h = jnp.broadcast_to(cb_ref[:, c0:c0 + cs], (head, cs))
        for k in range(SSD_CONV):
            off = pad - (SSD_CONV - 1) + k
            acc_h = acc_h + cw_ref[k:k + 1, c0:c0 + cs] * ubuf[off:off + head, c0:c0 + cs]
        u_h = acc_h * _sigmoid(acc_h)
        if c0 < SSD_D_INNER:
            xs_scr[:, c0:c0 + cs] = u
            xs_scr[0:head, c0:c0 + cs] = u_h
        else:
            bc_scr[:, c0 - SSD_D_INNER:c0 - SSD_D_INNER + cs] = u.astype(BF16)
            bc_scr[0:head, c0 - SSD_D_INNER:c0 - SSD_D_INNER + cs] = u_h.astype(BF16)
    ubuf[0:pad, :] = xbc_ref[tr - 2 * pad:tr, :].astype(F32)[pad:, :]

    xdt_in = dt_ref[...] + dtb_ref[...]
    dtv = jnp.maximum(xdt_in, 0.0) + jnp.log1p(jnp.exp(-jnp.abs(xdt_in)))
    da = dtv * (-jnp.exp(alog_ref[...]))
    r_i = lax.broadcasted_iota(jnp.int32, (tr, tr), 0)
    c_i = lax.broadcasted_iota(jnp.int32, (tr, tr), 1)
    tril = jnp.where((c_i <= r_i) & ((c_i // CHUNK) == (r_i // CHUNK)), 1.0, 0.0).astype(BF16)
    acum = sum(jnp.dot(tril, piece, preferred_element_type=F32) for piece in _split3(da))
    acum_scr[...] = acum

    a_hi, a_lo = _split2(acum)
    d_hi, d_lo = _split2(dtv)
    for c0 in range(0, SSD_D_INNER, cs):
        e = e_ref[:, c0:c0 + cs]
        ab = jnp.dot(a_hi, e, preferred_element_type=F32) + jnp.dot(a_lo, e, preferred_element_type=F32)
        db = jnp.dot(d_hi, e, preferred_element_type=F32) + jnp.dot(d_lo, e, preferred_element_type=F32)
        acb[:, c0:c0 + cs] = ab
        dfs_scr[:, c0:c0 + cs] = jnp.exp(ab)
        xdt = xs_scr[:, c0:c0 + cs] * db
        xdt_scr[:, c0:c0 + cs] = xdt.astype(BF16)
        for c in range(nch):
            r0 = c * CHUNK
            last = ab[r0 + CHUNK - 1:r0 + CHUNK, :]
            xw_scr[r0:r0 + CHUNK, c0:c0 + cs] = (xdt[r0:r0 + CHUNK, :] * jnp.exp(last - ab[r0:r0 + CHUNK, :])).astype(BF16)

    lane2 = lax.broadcasted_iota(jnp.int32, (CHUNK, LANES), 1)
    row2 = lax.broadcasted_iota(jnp.int32, (CHUNK, LANES), 0)
    causal2 = row2 >= (lane2 % SSD_HEAD_DIM)
    lo_half = lane2 < SSD_HEAD_DIM
    zpad = jnp.zeros((LANES - CHUNK, LANES), F32)

    def chunk_body(c, carry):
        r0 = pl.multiple_of(c * CHUNK, CHUNK)
        rows = pl.ds(r0, CHUNK)
        a_c = acum_scr[rows, :]
        a_t = jnp.concatenate([a_c, zpad], axis=0).T
        a_t_r = pltpu.roll(a_t, SSD_HEAD_DIM, 1)
        last_row = acb[pl.ds(r0 + CHUNK - 1, 1), :]
        for g in range(SSD_GROUPS):
            bg = bc_scr[rows, g * SSD_STATE:(g + 1) * SSD_STATE]
            cg = bc_scr[rows, SSD_GROUPS * SSD_STATE + g * SSD_STATE:SSD_GROUPS * SSD_STATE + (g + 1) * SSD_STATE]
            cbt = lax.dot_general(cg, jnp.concatenate([bg, bg], axis=0), (((1,), (1,)), ((), ())),
                                  preferred_element_type=F32)
            g0 = g * gw
            st = state[g]
            yoff = jnp.dot(cg, st.astype(BF16), preferred_element_type=F32) * dfs_scr[rows, g0:g0 + gw]
            for jj in range(gw // LANES):
                pidx = g * (gw // LANES) + jj
                l0 = pidx * LANES
                arow = a_t[2 * pidx:2 * pidx + 1, :] + a_t_r[2 * pidx + 1:2 * pidx + 2, :]
                seg = acb[rows, l0:l0 + LANES] - arow
                decay = jnp.exp(jnp.where(causal2, seg, -jnp.inf))
                mp = (cbt * decay).astype(BF16)
                xd = xdt_scr[rows, l0:l0 + LANES]
                zb = jnp.zeros_like(xd)
                bd = jnp.concatenate([jnp.where(lo_half, xd, zb), jnp.where(lo_half, zb, xd)], axis=0)
                yd = jnp.dot(mp, bd, preferred_element_type=F32)
                y_scr[rows, l0:l0 + LANES] = yd + yoff[:, jj * LANES:(jj + 1) * LANES]
            cd = jnp.exp(last_row[:, g0:g0 + gw])
            upd = lax.dot_general(bg, xw_scr[rows, g0:g0 + gw], (((0,), (0,)), ((), ())),
                                  preferred_element_type=F32)
            state[g] = st * cd + upd
        return carry

    lax.fori_loop(0, nch, chunk_body, 0, unroll=True)

    for g in range(SSD_GROUPS):
        g0 = g * gw
        y = y_scr[:, g0:g0 + gw] + dsk_ref[:, g0:g0 + gw] * xs_scr[:, g0:g0 + gw]
        zz = z_ref[:, g0:g0 + gw].astype(F32)
        y = y * (zz * _sigmoid(zz))
        o_ref[:, g0:g0 + gw] = _rms(y, nw_ref[:, g0:g0 + gw]).astype(BF16)


def _ssd(proj, dt_raw, conv_w, conv_b, dt_bias, a_log, d_skip_b, norm_w, expand, batch, seq, tr=256):
    t = batch * seq
    nr = seq // tr
    kern = functools.partial(_ssd_kernel, tr=tr)
    const = lambda shape: pl.BlockSpec(shape, lambda b, i: (0, 0))
    return pl.pallas_call(
        kern,
        grid=(batch, nr),
        in_specs=[
            pl.BlockSpec((tr, SSD_CONV_DIM), lambda b, i: (b * nr + i, COL_XBC // SSD_CONV_DIM)),
            pl.BlockSpec((tr, SSD_D_INNER), lambda b, i: (b * nr + i, COL_Z // SSD_D_INNER)),
            pl.BlockSpec((tr, LANES), lambda b, i: (b * nr + i, 0)),
            const((SSD_CONV, SSD_CONV_DIM)),
            const((1, SSD_CONV_DIM)),
            const((1, LANES)),
            const((1, LANES)),
            const((1, SSD_D_INNER)),
            const((1, SSD_D_INNER)),
            const((LANES, SSD_D_INNER)),
        ],
        out_specs=pl.BlockSpec((tr, SSD_D_INNER), lambda b, i: (b * nr + i, 0)),
        out_shape=jax.ShapeDtypeStruct((t, SSD_D_INNER), BF16),
        scratch_shapes=[
            pltpu.VMEM((8 + 16, SSD_CONV_DIM), F32),
            pltpu.VMEM(((SSD_CONV - 1) * tr, tr), BF16),
            pltpu.VMEM((tr, SSD_D_INNER), F32),
            pltpu.VMEM((tr, 2 * SSD_GROUPS * SSD_STATE), BF16),
            pltpu.VMEM((tr, LANES), F32),
            pltpu.VMEM((tr, SSD_D_INNER), F32),
            pltpu.VMEM((tr, SSD_D_INNER), F32),
            pltpu.VMEM((tr, SSD_D_INNER), BF16),
            pltpu.VMEM((tr, SSD_D_INNER), BF16),
            pltpu.VMEM((tr, SSD_D_INNER), F32),
            pltpu.VMEM((SSD_GROUPS, SSD_STATE, SSD_D_INNER // SSD_GROUPS), F32),
        ],
        compiler_params=_cparams(("parallel", "arbitrary")),
        name="ssd",
    )(proj, proj, dt_raw, conv_w, conv_b, dt_bias, a_log, d_skip_b, norm_w, expand)


def _merge_kernel(ya_ref, ys_ref, ga_ref, gs_ref, x_ref, wa_ref, wb_ref, wo_ref, o_ref):
    a = jnp.dot(ya_ref[...], wa_ref[...], preferred_element_type=F32)
    s = jnp.dot(ys_ref[...], wb_ref[...], preferred_element_type=F32)
    merged = _sigmoid(ga_ref[...].astype(F32)) * a + _sigmoid(gs_ref[...].astype(F32)) * s
    o_ref[...] = x_ref[...] + jnp.dot(merged.astype(BF16), wo_ref[...], preferred_element_type=F32)


def _merge(y_attn, y_ssd, proj, x2d, w_a, w_b, w_o, tm=512):
    t = x2d.shape[0]
    tm = min(tm, t)
    full = lambda r, c: pl.BlockSpec((r, c), lambda i: (0, 0))
    return pl.pallas_call(
        _merge_kernel,
        grid=(t // tm,),
        in_specs=[
            pl.BlockSpec((tm, D_MODEL), lambda i: (i, 0)),
            pl.BlockSpec((tm, SSD_D_INNER), lambda i: (i, 0)),
            pl.BlockSpec((tm, D_MODEL), lambda i: (i, COL_GA // D_MODEL)),
            pl.BlockSpec((tm, D_MODEL), lambda i: (i, COL_GS // D_MODEL)),
            pl.BlockSpec((tm, D_MODEL), lambda i: (i, 0)),
            full(D_MODEL, D_MODEL), full(SSD_D_INNER, D_MODEL), full(D_MODEL, D_MODEL),
        ],
        out_specs=pl.BlockSpec((tm, D_MODEL), lambda i: (i, 0)),
        out_shape=jax.ShapeDtypeStruct((t, D_MODEL), F32),
        compiler_params=_cparams(("parallel",)),
        name="merge",
    )(y_attn, y_ssd, proj, proj, x2d, w_a, w_b, w_o)


def _router_kernel(x_ref, nw_ref, rw_ref, rb_ref, h_ref, idx_ref, gate_ref, rank_ref, cnt_ref, carry_ref, *, tt):
    i = pl.program_id(0)

    @pl.when(i == 0)
    def _():
        carry_ref[...] = jnp.zeros_like(carry_ref)

    h = _rms(x_ref[...], nw_ref[...])
    h_ref[...] = pltpu.einshape("r(cl)->(rc)l", h, c=ROW_CHUNKS)
    h_hi, h_lo = _split2(h)
    logits = (jnp.dot(h_hi, rw_ref[0], preferred_element_type=F32) + jnp.dot(h_hi, rw_ref[1], preferred_element_type=F32)
              + jnp.dot(h_lo, rw_ref[0], preferred_element_type=F32)) + rb_ref[...]
    lane = lax.broadcasted_iota(jnp.int32, (tt, LANES), 1)
    work = jnp.where(lane < N_EXPERTS, logits, -jnp.inf)

    vals, idxs, hots = [], [], []
    for _ in range(TOP_K):
        m = jnp.max(work, axis=-1, keepdims=True)
        idx = jnp.min(jnp.where(work == m, lane, LANES), axis=-1, keepdims=True)
        hot = lane == idx
        vals.append(m)
        idxs.append(idx)
        hots.append(hot)
        work = jnp.where(hot, -jnp.inf, work)

    exps = [jnp.exp(v - vals[0]) for v in vals]
    denom = exps[0] + exps[1] + exps[2] + exps[3]

    hot_sum = sum(jnp.where(hot, 1.0, 0.0) for hot in hots)
    r_i = lax.broadcasted_iota(jnp.int32, (tt, tt), 0)
    c_i = lax.broadcasted_iota(jnp.int32, (tt, tt), 1)
    strict = jnp.where(c_i < r_i, 1.0, 0.0).astype(BF16)
    prefix = jnp.dot(strict, hot_sum.astype(BF16), preferred_element_type=F32) + carry_ref[...]

    idx_out = jnp.zeros((tt, LANES), jnp.int32)
    rank_out = jnp.zeros((tt, LANES), jnp.int32)
    gate_out = jnp.zeros((tt, LANES), F32)
    for k in range(TOP_K):
        rank_k = jnp.sum(jnp.where(hots[k], prefix, 0.0), axis=-1, keepdims=True).astype(jnp.int32)
        sel = lane == k
        idx_out = jnp.where(sel, idxs[k], idx_out)
        rank_out = jnp.where(sel, rank_k, rank_out)
        gate_out = jnp.where(sel, exps[k] / denom, gate_out)
    idx_ref[...] = idx_out
    rank_ref[...] = rank_out
    gate_ref[...] = gate_out

    carry_ref[...] = carry_ref[...] + jnp.sum(hot_sum, axis=0, keepdims=True)
    cnt_ref[...] = carry_ref[...]


def _router(x1, norm_w, router_w_pad, router_b_pad, tt=512):
    t = x1.shape[0]
    tt = min(tt, t)
    kern = functools.partial(_router_kernel, tt=tt)
    row = pl.BlockSpec((tt, LANES), lambda i: (i, 0))
    return pl.pallas_call(
        kern,
        grid=(t // tt,),
        in_specs=[
            pl.BlockSpec((tt, D_MODEL), lambda i: (i, 0)),
            pl.BlockSpec((1, D_MODEL), lambda i: (0, 0)),
            pl.BlockSpec((2, D_MODEL, LANES), lambda i: (0, 0, 0)),
            pl.BlockSpec((1, LANES), lambda i: (0, 0)),
        ],
        out_specs=[pl.BlockSpec((tt * ROW_CHUNKS, LANES), lambda i: (i, 0)), row, row, row,
                   pl.BlockSpec((1, LANES), lambda i: (0, 0))],
        out_shape=[
            jax.ShapeDtypeStruct((t * ROW_CHUNKS, LANES), F32),
            jax.ShapeDtypeStruct((t, LANES), jnp.int32),
            jax.ShapeDtypeStruct((t, LANES), F32),
            jax.ShapeDtypeStruct((t, LANES), jnp.int32),
            jax.ShapeDtypeStruct((1, LANES), F32),
        ],
        scratch_shapes=[pltpu.VMEM((1, LANES), F32)],
        compiler_params=_cparams(("arbitrary",)),
        name="router",
    )(x1, norm_w, router_w_pad, router_b_pad)


def _dispatch_kernel(pos_ref, h_ref, o_ref, sem, *, tt):
    def copy(t, k):
        src = pl.multiple_of(t * ROW_CHUNKS, ROW_CHUNKS)
        dst = pl.multiple_of(pos_ref[t * TOP_K + k] * ROW_CHUNKS, ROW_CHUNKS)
        return pltpu.make_async_copy(h_ref.at[pl.ds(src, ROW_CHUNKS), :], o_ref.at[pl.ds(dst, ROW_CHUNKS), :], sem)

    def issue(t, c):
        for k in range(TOP_K):
            copy(t, k).start(priority=k % 2)
        return c

    lax.fori_loop(0, tt, issue, 0, unroll=8)

    for _ in range(TOP_K):
        pltpu.make_async_copy(h_ref, o_ref.at[pl.ds(0, tt * ROW_CHUNKS), :], sem).wait()


def _dispatch(h2, pos_flat, tt=1024):
    t = h2.shape[0] // ROW_CHUNKS
    tt = min(tt, t)
    kern = functools.partial(_dispatch_kernel, tt=tt)
    return pl.pallas_call(
        kern,
        grid=(t // tt,),
        in_specs=[
            pl.BlockSpec((tt * TOP_K,), lambda i: (i,), memory_space=pltpu.SMEM),
            pl.BlockSpec((tt * ROW_CHUNKS, LANES), lambda i: (i, 0)),
        ],
        out_specs=pl.BlockSpec(memory_space=pl.ANY),
        out_shape=jax.ShapeDtypeStruct((t * TOP_K * ROW_CHUNKS, LANES), F32),
        scratch_shapes=[pltpu.SemaphoreType.DMA(())],
        compiler_params=_cparams(("arbitrary",)),
        name="dispatch",
    )(pos_flat, h2)


def _experts_kernel(tile_ref, exp_ref, first_ref, efirst_ref, valid_ref, start_ref,
                    x_ref, wu_ref, bu_ref, wd_ref, bd_ref, o_ref, wub_ref, wdb_ref, *, tm):
    w = pl.program_id(0)

    @pl.when(efirst_ref[w] == 1)
    def _():
        for c0 in range(0, 2 * D_EXPERT, 512):
            wub_ref[:, c0:c0 + 512] = wu_ref[0, :, c0:c0 + 512].astype(BF16)
        for c0 in range(0, D_MODEL, 512):
            wdb_ref[:, c0:c0 + 512] = wd_ref[0, :, c0:c0 + 512].astype(BF16)

    @pl.when(valid_ref[w] == 1)
    def _():
        e = exp_ref[w]
        x = pltpu.einshape("(rc)l->r(cl)", x_ref[...], c=ROW_CHUNKS).astype(BF16)
        gu = jnp.dot(x, wub_ref[...], preferred_element_type=F32) + bu_ref[0]
        g = jnp.minimum(gu[:, :D_EXPERT], SWIGLU_LIMIT)
        u = jnp.clip(gu[:, D_EXPERT:], -SWIGLU_LIMIT, SWIGLU_LIMIT)
        act = (u + 1.0) * (g * _sigmoid(SWIGLU_ALPHA * g))
        out = jnp.dot(act.astype(BF16), wdb_ref[...], preferred_element_type=F32) + bd_ref[0]
        rows = tile_ref[w] * tm + lax.broadcasted_iota(jnp.int32, (tm, 1), 0)
        mine = (rows >= start_ref[e]) & (rows < start_ref[e + 1])

        def as_tiles(v):
            return pltpu.einshape("r(cl)->(rc)l", v, c=ROW_CHUNKS)

        @pl.when(first_ref[w] == 1)
        def _():
            o_ref[...] = as_tiles(jnp.where(mine, out, 0.0))

        @pl.when(first_ref[w] == 0)
        def _():
            o_ref[...] = as_tiles(jnp.where(mine, out, _tiles_to_rows(o_ref, tm)))


def _experts(x_sorted, item_tile, item_exp, item_first, item_efirst, item_valid, starts, w_up, b_up, w_down, b_down, tm):
    n = x_sorted.shape[0] // ROW_CHUNKS
    n_items = item_tile.shape[0]
    kern = functools.partial(_experts_kernel, tm=tm)
    grid_spec = pltpu.PrefetchScalarGridSpec(
        num_scalar_prefetch=6,
        grid=(n_items,),
        in_specs=[
            pl.BlockSpec((tm * ROW_CHUNKS, LANES), lambda w, tl, ex, *_: (tl[w], 0)),
            pl.BlockSpec((1, D_MODEL, 2 * D_EXPERT), lambda w, tl, ex, *_: (ex[w], 0, 0)),
            pl.BlockSpec((1, 1, 2 * D_EXPERT), lambda w, tl, ex, *_: (ex[w], 0, 0)),
            pl.BlockSpec((1, D_EXPERT, D_MODEL), lambda w, tl, ex, *_: (ex[w], 0, 0)),
            pl.BlockSpec((1, 1, D_MODEL), lambda w, tl, ex, *_: (ex[w], 0, 0)),
        ],
        out_specs=pl.BlockSpec((tm * ROW_CHUNKS, LANES), lambda w, tl, ex, *_: (tl[w], 0)),
        scratch_shapes=[pltpu.VMEM((D_MODEL, 2 * D_EXPERT), BF16), pltpu.VMEM((D_EXPERT, D_MODEL), BF16)],
    )
    return pl.pallas_call(
        kern,
        grid_spec=grid_spec,
        out_shape=jax.ShapeDtypeStruct((n * ROW_CHUNKS, LANES), F32),
        compiler_params=_cparams(("arbitrary",)),
        name="experts",
    )(item_tile, item_exp, item_first, item_efirst, item_valid, starts, x_sorted, w_up, b_up, w_down, b_down)


def _combine_kernel(pos_ref, posn_ref, gate_ref, x_ref, p_ref, pnw_ref, wpg_ref, wpp_ref, fnw_ref, ys_ref, o_ref,
                    rows_ref, sem, *, tt):
    i = pl.program_id(0)
    n = pl.num_programs(0)
    slot = i % 2

    def gather(idx_ref, dst_slot):
        def issue(t, c):
            for k in range(TOP_K):
                src = pl.multiple_of(idx_ref[t * TOP_K + k] * ROW_CHUNKS, ROW_CHUNKS)
                dst = pl.multiple_of((k * tt + t) * ROW_CHUNKS, ROW_CHUNKS)
                pltpu.make_async_copy(ys_ref.at[pl.ds(src, ROW_CHUNKS), :],
                                      rows_ref.at[dst_slot, pl.ds(dst, ROW_CHUNKS), :], sem.at[dst_slot]
                                      ).start(priority=k % 2)
            return c

        lax.fori_loop(0, tt, issue, 0, unroll=8)

    @pl.when(i == 0)
    def _():
        gather(pos_ref, 0)

    @pl.when(i + 1 < n)
    def _():
        gather(posn_ref, 1 - slot)

    pltpu.make_async_copy(ys_ref.at[pl.ds(0, TOP_K * tt * ROW_CHUNKS), :], rows_ref.at[slot], sem.at[slot]).wait()

    gate = gate_ref[...]
    x = x_ref[...]
    for k in range(TOP_K):
        tiles = rows_ref[slot, pl.ds(k * tt * ROW_CHUNKS, tt * ROW_CHUNKS), :]
        x = x + gate[:, k:k + 1] * pltpu.einshape("(rc)l->r(cl)", tiles, c=ROW_CHUNKS)

    hg = _rms(x, pnw_ref[...]).astype(BF16)
    pg = _sigmoid(jnp.dot(hg, wpg_ref[...], preferred_element_type=F32))
    x = x + pg * jnp.dot(p_ref[...].astype(BF16), wpp_ref[...], preferred_element_type=F32)
    o_ref[...] = _rms(x, fnw_ref[...])


def _combine(pos_flat, gate, x1, p2d, ple_nw, w_pg, w_pp, final_nw, y_sorted, tt=256):
    t = x1.shape[0]
    tt = min(tt, t)
    kern = functools.partial(_combine_kernel, tt=tt)
    vec = pl.BlockSpec((1, D_MODEL), lambda i: (0, 0))
    nt = t // tt
    return pl.pallas_call(
        kern,
        grid=(nt,),
        in_specs=[
            pl.BlockSpec((tt * TOP_K,), lambda i: (i,), memory_space=pltpu.SMEM),
            pl.BlockSpec((tt * TOP_K,), lambda i: (jnp.minimum(i + 1, nt - 1),), memory_space=pltpu.SMEM),
            pl.BlockSpec((tt, LANES), lambda i: (i, 0)),
            pl.BlockSpec((tt, D_MODEL), lambda i: (i, 0)),
            pl.BlockSpec((tt, PLE_DIM), lambda i: (i, 0)),
            vec,
            pl.BlockSpec((D_MODEL, D_MODEL), lambda i: (0, 0)),
            pl.BlockSpec((PLE_DIM, D_MODEL), lambda i: (0, 0)),
            vec,
            pl.BlockSpec(memory_space=pl.ANY),
        ],
        out_specs=pl.BlockSpec((tt, D_MODEL), lambda i: (i, 0)),
        out_shape=jax.ShapeDtypeStruct((t, D_MODEL), F32),
        scratch_shapes=[pltpu.VMEM((2, TOP_K * tt * ROW_CHUNKS, LANES), F32), pltpu.SemaphoreType.DMA((2,))],
        compiler_params=_cparams(("arbitrary",)),
        name="combine",
    )(pos_flat, pos_flat, gate, x1, p2d, ple_nw, w_pg, w_pp, final_nw, y_sorted)


def _routing_plan(counts, idx, rank, tm, n_rows):
    counts = counts.astype(jnp.int32)
    ends = jnp.cumsum(counts)
    starts = ends - counts
    pos = (starts[idx] + rank).reshape(-1)
    n_tiles = n_rows // tm
    n_items = n_tiles + N_EXPERTS - 1
    first_tile = starts // tm
    last_tile = jnp.maximum(ends - 1, 0) // tm
    items_e = jnp.where(counts > 0, last_tile - first_tile + 1, 0)
    item_end = jnp.cumsum(items_e)
    item_start = item_end - items_e
    total = item_end[-1]
    w = jnp.arange(n_items, dtype=jnp.int32)
    e_w = jnp.minimum(jnp.sum(w[:, None] >= item_end[None, :], axis=1).astype(jnp.int32), N_EXPERTS - 1)
    tile_w = first_tile[e_w] + (w - item_start[e_w])
    valid = w < total
    last = jnp.maximum(total - 1, 0)
    e_w = jnp.where(valid, e_w, e_w[last])
    tile_w = jnp.where(valid, tile_w, tile_w[last])
    prev_tile = jnp.concatenate([jnp.full((1,), -1, jnp.int32), tile_w[:-1]])
    first = (tile_w != prev_tile) & valid
    prev_e = jnp.concatenate([jnp.full((1,), -1, jnp.int32), e_w[:-1]])
    efirst = (e_w != prev_e) & valid
    starts_ext = jnp.concatenate([starts, ends[-1:]]).astype(jnp.int32)
    return (pos.astype(jnp.int32), tile_w.astype(jnp.int32), e_w, first.astype(jnp.int32), efirst.astype(jnp.int32),
            valid.astype(jnp.int32), starts_ext)


def kernel(x, p, mix_norm_w, w_in, lambda_q1, lambda_k1, lambda_q2, lambda_k2, da_head_norm_w, w_attn_branch, conv_w, conv_b, dt_bias, a_log, d_skip, ssd_norm_w, w_ssd_branch, w_out, moe_norm_w, router_w, router_b, w_up, b_up, w_down, b_down, ple_norm_w, w_ple_gate, w_ple_proj, final_norm_w):
    batch, seq, _ = x.shape
    t = batch * seq
    depth = w_in.shape[0]
    assert depth == 1, "the final RMSNorm is fused into the layer's last kernel"
    x2d = x.reshape(t, D_MODEL)

    log2e = math.log2(math.e)
    slopes = jnp.asarray([log2e * 2.0 ** (-8.0 * (h + 1) / DA_HEADS) for h in range(DA_HEADS)], F32)
    q_scale = log2e * DA_QK_DIM ** -0.5
    head_ids = jnp.arange(SSD_D_INNER, dtype=jnp.int32) // SSD_HEAD_DIM
    expand = (jnp.arange(LANES, dtype=jnp.int32)[:, None] == head_ids[None, :]).astype(BF16)

    def pad_lanes(v, fill=0.0):
        return jnp.pad(v.reshape(1, -1), ((0, 0), (0, LANES - v.shape[-1])), constant_values=fill)

    for i in range(depth):
        wi = w_in[i]
        w_main = jnp.concatenate([
            wi[:, _R_Z:_R_Z + SSD_D_INNER], wi[:, _R_Q:_R_Q + 1024] * q_scale, wi[:, _R_K:_R_K + 1024],
            wi[:, _R_GA:_R_GA + 1024], wi[:, _R_GS:_R_GS + 1024], wi[:, _R_XBC:_R_XBC + SSD_CONV_DIM],
            wi[:, _R_V:_R_V + 1024]], axis=1).astype(BF16)
        w_dt = jnp.pad(wi[:, _R_DT:_R_DT + SSD_HEADS], ((0, 0), (0, LANES - SSD_HEADS))).astype(BF16)

        proj, dt_raw, v_t = _in_proj(x2d, mix_norm_w[i].reshape(1, -1), w_main, w_dt)

        lambda_init = 0.8 - 0.6 * math.exp(-0.3 * i)
        lam = (jnp.exp(jnp.sum(lambda_q1[i] * lambda_k1[i])) - jnp.exp(jnp.sum(lambda_q2[i] * lambda_k2[i]))
               + lambda_init).reshape(1).astype(F32)
        y_attn = _diff_attn(proj, v_t, slopes, lam, da_head_norm_w[i].reshape(1, -1), batch, seq, lambda_init)

        y_ssd = _ssd(proj, dt_raw, conv_w[i], conv_b[i].reshape(1, -1), pad_lanes(dt_bias[i]), pad_lanes(a_log[i]),
                     jnp.repeat(d_skip[i], SSD_HEAD_DIM).reshape(1, -1), ssd_norm_w[i].reshape(1, -1), expand,
                     batch, seq)

        x1 = _merge(y_attn, y_ssd, proj, x2d, w_attn_branch[i].astype(BF16), w_ssd_branch[i].astype(BF16),
                    w_out[i].astype(BF16))

        rw = jnp.stack(_split2(jnp.pad(router_w[i], ((0, 0), (0, LANES - N_EXPERTS)))))
        h2, idx, gate, rank, counts = _router(x1, moe_norm_w[i].reshape(1, -1), rw, pad_lanes(router_b[i]))

        tm = 512
        n_rows = t * TOP_K
        pos, item_tile, item_exp, item_first, item_efirst, item_valid, starts = _routing_plan(
            counts[0, :N_EXPERTS], idx[:, :TOP_K], rank[:, :TOP_K], tm, n_rows)

        x_sorted = _dispatch(h2, pos)
        y_sorted = _experts(x_sorted, item_tile, item_exp, item_first, item_efirst, item_valid, starts,
                            w_up[i], b_up[i].reshape(N_EXPERTS, 1, -1),
                            w_down[i], b_down[i].reshape(N_EXPERTS, 1, -1), tm)

        x2d = _combine(pos, gate, x1, p[i].reshape(t, PLE_DIM), ple_norm_w[i].reshape(1, -1),
                       w_ple_gate[i].astype(BF16), w_ple_proj[i].astype(BF16), final_norm_w.reshape(1, -1), y_sorted)
    return x2d.reshape(batch, seq, D_MODEL)
```
